```python
import jax
import jax.numpy as jnp
from jax import lax
import numpy as np

D_MODEL = 1024
BATCH = 8
SEQ = 4096
DEPTH = 2

GDN_HEADS = 4
GDN_HEAD_DIM = 128
GDN_WIDTH = GDN_HEADS * GDN_HEAD_DIM
GDN_CONV = 4
GDN_CHUNK = 64
RWKV_HEADS = 8
RWKV_HEAD_DIM = 64
RWKV_WIDTH = RWKV_HEADS * RWKV_HEAD_DIM
RWKV_W_LORA = 64
RWKV_A_LORA = 64
RWKV_G_LORA = 128
RWKV_IN = 3 * RWKV_WIDTH + RWKV_W_LORA + RWKV_A_LORA + RWKV_G_LORA
SB_HEADS = 8
SB_HEAD_DIM = 64
SB_WIDTH = SB_HEADS * SB_HEAD_DIM
SB_BLOCK = 128
N_BRANCH = 3
IN_SPLITS = (3 * GDN_WIDTH, GDN_WIDTH, GDN_HEADS, GDN_HEADS, RWKV_IN, 3 * SB_WIDTH, N_BRANCH * D_MODEL)
D_IN = 3 * GDN_WIDTH + GDN_WIDTH + 2 * GDN_HEADS + RWKV_IN + 3 * SB_WIDTH + N_BRANCH * D_MODEL
D_FF_DENSE = 2816
N_EXPERTS = 8
TOP_K = 2
D_FF_EXPERT = 3584
N_DENSE_LAYERS = (DEPTH + 1) // 2
N_MOE_LAYERS = DEPTH // 2
RMS_EPS = 1e-6
GN_EPS = 64e-5
L2_EPS = 1e-6

kernel_name = "hybrid_gdn_rwkv7_stickbreak_moe"


def _split(h, sizes):
    out, start = [], 0
    for s in sizes:
        out.append(h[..., start:start + s])
        start += s
    return out


def rmsnorm(x, g):
    xf = x.astype(jnp.float32)
    y = xf * lax.rsqrt(jnp.mean(xf * xf, axis=-1, keepdims=True) + RMS_EPS)
    return (y * g.astype(jnp.float32)).astype(x.dtype)


def l2norm(x):
    return x * lax.rsqrt(jnp.sum(x * x, axis=-1, keepdims=True) + L2_EPS)


def causal_shift(h):
    return jnp.pad(h, ((0, 0), (1, 0), (0, 0)))[:, :-1]


def causal_short_conv(h, w):
    T = h.shape[1]
    K = w.shape[0]
    hp = jnp.pad(h, ((0, 0), (K - 1, 0), (0, 0)))
    y = hp[:, 0:T] * w[0]
    for i in range(1, K):
        y = y + hp[:, i:i + T] * w[i]
    return jax.nn.silu(y)


def gated_delta_rule_chunked(q, k, v, g, beta):
    B_, H, T, Dk = q.shape
    Dv = v.shape[-1]
    C = GDN_CHUNK
    N = T // C
    q = q.reshape(B_, H, N, C, Dk)
    k = k.reshape(B_, H, N, C, Dk)
    v = v.reshape(B_, H, N, C, Dv)
    beta = beta.reshape(B_, H, N, C)
    g = jnp.cumsum(g.reshape(B_, H, N, C), axis=-1)
    tri_incl = jnp.tril(jnp.ones((C, C), dtype=bool))
    tri_strict = jnp.tril(jnp.ones((C, C), dtype=bool), -1)
    decay = jnp.exp(jnp.where(tri_incl, g[..., :, None] - g[..., None, :], -jnp.inf))
    k_beta = k * beta[..., None]
    v_beta = v * beta[..., None]
    m = jnp.where(tri_strict, jnp.einsum('bhnid,bhnjd->bhnij', k_beta, k) * decay, 0.0)
    eye = jnp.eye(C, dtype=q.dtype)
    rhs = jnp.concatenate([v_beta, k_beta * jnp.exp(g)[..., None]], axis=-1)
    sol = lax.linalg.triangular_solve(eye + m, rhs, left_side=True, lower=True, unit_diagonal=True)
    u, w = sol[..., :Dv], sol[..., Dv:]
    attn = jnp.where(tri_incl, jnp.einsum('bhnid,bhnjd->bhnij', q, k) * decay, 0.0)
    g_last = g[..., -1]
    q_dec = q * jnp.exp(g)[..., None]
    k_dec = k * jnp.exp(g_last[..., None] - g)[..., None]
    xs = tuple(jnp.moveaxis(t, 2, 0) for t in (q_dec, k_dec, u, w, attn, g_last))

    def step(S, inp):
        qd, kd, u_c, w_c, a_c, gl = inp
        v_new = u_c - jnp.einsum('bhck,bhkv->bhcv', w_c, S)
        o = jnp.einsum('bhck,bhkv->bhcv', qd, S) + jnp.einsum('bhij,bhjv->bhiv', a_c, v_new)
        S = S * jnp.exp(gl)[..., None, None] + jnp.einsum('bhck,bhcv->bhkv', kd, v_new)
        return S, o

    S0 = jnp.zeros((B_, H, Dk, Dv), q.dtype)
    _, o = lax.scan(step, S0, xs)
    return jnp.moveaxis(o, 0, 2).reshape(B_, H, T, Dv)


def gated_deltanet(h_qkv, h_z, h_b, h_a, conv_w, a_log, dt_bias, norm_g):
    B_, T, _ = h_qkv.shape
    qkv = causal_short_conv(h_qkv, conv_w).astype(jnp.float32)
    q, k, v = jnp.split(qkv, 3, axis=-1)

    def heads(t):
        return t.reshape(B_, T, GDN_HEADS, GDN_HEAD_DIM).transpose(0, 2, 1, 3)

    q = l2norm(heads(q)) * (GDN_HEAD_DIM ** -0.5)
    k = l2norm(heads(k))
    v = heads(v)
    beta = jax.nn.sigmoid(h_b.astype(jnp.float32)).transpose(0, 2, 1)
    g = -jnp.exp(a_log.astype(jnp.float32)) * jax.nn.softplus(
        h_a.astype(jnp.float32) + dt_bias.astype(jnp.float32))
    g = g.transpose(0, 2, 1)
    o = gated_delta_rule_chunked(q, k, v, g, beta).transpose(0, 2, 1, 3)
    o = o * lax.rsqrt(jnp.mean(o * o, axis=-1, keepdims=True) + RMS_EPS) * norm_g.astype(jnp.float32)
    z = h_z.astype(jnp.float32).reshape(B_, T, GDN_HEADS, GDN_HEAD_DIM)
    o = o * jax.nn.silu(z)
    return o.reshape(B_, T, GDN_WIDTH).astype(h_qkv.dtype)


def rwkv7_time_mix(h, mu, w0, w2, a0, a2, g2, k_k, k_a, r_k, ln_g, ln_b):
    B_, T, _ = h.shape
    hf = h.astype(jnp.float32)
    hl = hf + (causal_shift(hf) - hf) * mu.astype(jnp.float32)
    r, k, v, xw, xa, xg = _split(hl, (RWKV_WIDTH, RWKV_WIDTH, RWKV_WIDTH, RWKV_W_LORA, RWKV_A_LORA, RWKV_G_LORA))
    w_log = -jax.nn.softplus(-(w0 + jnp.tanh(xw) @ w2)) - 0.5
    decay = jnp.exp(-jnp.exp(w_log))
    a = jax.nn.sigmoid(a0 + xa @ a2)
    gate = jax.nn.sigmoid(xg) @ g2
    kk = k * k_k
    k = k * (1.0 + (a - 1.0) * k_a)

    def heads(t):
        return t.reshape(B_, T, RWKV_HEADS, RWKV_HEAD_DIM)

    r_h, w_h, k_h, v_h, a_h = heads(r), heads(decay), heads(k), heads(v), heads(a)
    kk_h = l2norm(heads(kk))
    xs = tuple(t.transpose(1, 0, 2, 3) for t in (r_h, w_h, k_h, v_h, kk_h, a_h))

    def step(S, inp):
        r_t, w_t, k_t, v_t, kk_t, a_t = inp
        sa = jnp.einsum('bhvk,bhk->bhv', S, kk_t)
        S = (S * w_t[:, :, None, :]
             - sa[..., None] * (kk_t * a_t)[:, :, None, :]
             + v_t[..., None] * k_t[:, :, None, :])
        y = jnp.einsum('bhvk,bhk->bhv', S, r_t)
        return S, y

    S0 = jnp.zeros((B_, RWKV_HEADS, RWKV_HEAD_DIM, RWKV_HEAD_DIM), jnp.float32)
    _, y = lax.scan(step, S0, xs)
    y = y.transpose(1, 0, 2, 3)
    mean = jnp.mean(y, axis=-1, keepdims=True)
    var = jnp.mean(jnp.square(y - mean), axis=-1, keepdims=True)
    y = ((y - mean) * lax.rsqrt(var + GN_EPS)).reshape(B_, T, RWKV_WIDTH) * ln_g + ln_b
    bonus = jnp.sum(r_h * k_h * r_k, axis=-1, keepdims=True) * v_h
    y = (y + bonus.reshape(B_, T, RWKV_WIDTH)) * gate
    return y.astype(h.dtype)


def stick_breaking_attention(h_qkv):
    B_, T, _ = h_qkv.shape
    q, k, v = jnp.split(h_qkv, 3, axis=-1)

    def heads(t):
        return t.reshape(B_, T, SB_HEADS, SB_HEAD_DIM).transpose(0, 2, 1, 3)

    q, k, v = heads(q), heads(k), heads(v)
    scale = SB_HEAD_DIM ** -0.5
    outs = []
    for start in range(0, T, SB_BLOCK):
        end = start + SB_BLOCK
        z = jnp.einsum('bhqd,bhkd->bhqk', q[:, :, start:end], k[:, :, :end]).astype(jnp.float32) * scale
        mask = jnp.arange(end)[None, :] < (start + jnp.arange(SB_BLOCK))[:, None]
        log_1mb = jnp.where(mask, jax.nn.log_sigmoid(-z), 0.0)
        log_a = jax.nn.log_sigmoid(z) + lax.cumsum(log_1mb, axis=3, reverse=True) - log_1mb
        att = jnp.where(mask, jnp.exp(log_a), 0.0).astype(v.dtype)
        outs.append(jnp.einsum('bhqk,bhkd->bhqd', att, v[:, :, :end]))
    o = jnp.concatenate(outs, axis=2)
    return o.transpose(0, 2, 1, 3).reshape(B_, T, SB_WIDTH)


def swiglu(h, w_gate, w_up, w_down):
    return (jax.nn.silu(h @ w_gate) * (h @ w_up)) @ w_down


def moe_swiglu(h, router_w, w_gate, w_up, w_down):
    B_, T, D = h.shape
    tokens = h.reshape(B_ * T, D)
    logits = (tokens @ router_w).astype(jnp.float32)
    top_vals, top_idx = lax.top_k(logits, TOP_K)
    top_w = jax.nn.softmax(top_vals, axis=-1)
    combine = jnp.sum(jax.nn.one_hot(top_idx, N_EXPERTS, dtype=jnp.float32) * top_w[..., None], axis=1)
    out = jnp.zeros_like(tokens)
    for e in range(N_EXPERTS):
        y = swiglu(tokens, w_gate[e], w_up[e], w_down[e])
        out = out + combine[:, e:e + 1].astype(tokens.dtype) * y
    return out.reshape(B_, T, D)


def setup_inputs(seed: int = 0) -> dict:
    key = jax.random.key(seed)
    ks = iter(jax.random.split(key, 48))

    def nrm(shape, scale):
        return jax.random.normal(next(ks), shape, jnp.float32) * scale

    def uni(shape, lo, hi):
        return jax.random.uniform(next(ks), shape, jnp.float32, minval=lo, maxval=hi)

    L = DEPTH
    return {
        "x": nrm((BATCH, SEQ, D_MODEL), 1.0),
        "norm_mix_g": 1.0 + nrm((L, D_MODEL), 0.02),
        "w_in": nrm((L, D_MODEL, D_IN), D_MODEL ** -0.5),
        "gdn_conv_w": nrm((L, GDN_CONV, 3 * GDN_WIDTH), GDN_CONV ** -0.5),
        "gdn_a_log": jnp.log(uni((L, GDN_HEADS), 1.0, 16.0)),
        "gdn_dt_bias": nrm((L, GDN_HEADS), 0.1),
        "gdn_norm_g": 1.0 + nrm((L, GDN_HEAD_DIM), 0.02),
        "rwkv_mu": uni((L, RWKV_IN), 0.0, 1.0),
        "rwkv_w0": uni((L, RWKV_WIDTH), -6.0, -1.0),
        "rwkv_w2": nrm((L, RWKV_W_LORA, RWKV_WIDTH), 0.1 * RWKV_W_LORA ** -0.5),
        "rwkv_a0": nrm((L, RWKV_WIDTH), 0.1),
        "rwkv_a2": nrm((L, RWKV_A_LORA, RWKV_WIDTH), RWKV_A_LORA ** -0.5),
        "rwkv_g2": nrm((L, RWKV_G_LORA, RWKV_WIDTH), RWKV_G_LORA ** -0.5),
        "rwkv_k_k": 0.85 + nrm((L, RWKV_WIDTH), 0.02),
        "rwkv_k_a": 1.0 + nrm((L, RWKV_WIDTH), 0.02),
        "rwkv_r_k": nrm((L, RWKV_HEADS, RWKV_HEAD_DIM), 0.1),
        "rwkv_ln_g": 1.0 + nrm((L, RWKV_WIDTH), 0.02),
        "rwkv_ln_b": nrm((L, RWKV_WIDTH), 0.02),
        "w_branch_gdn": nrm((L, GDN_WIDTH, D_MODEL), GDN_WIDTH ** -0.5),
        "w_branch_rwkv": nrm((L, RWKV_WIDTH, D_MODEL), RWKV_WIDTH ** -0.5),
        "w_branch_sb": nrm((L, SB_WIDTH, D_MODEL), SB_WIDTH ** -0.5),
        "w_out": nrm((L, D_MODEL, D_MODEL), D_MODEL ** -0.5),
        "norm_ffn_g": 1.0 + nrm((L, D_MODEL), 0.02),
        "ffn_w_gate": nrm((N_DENSE_LAYERS, D_MODEL, D_FF_DENSE), D_MODEL ** -0.5),
        "ffn_w_up": nrm((N_DENSE_LAYERS, D_MODEL, D_FF_DENSE), D_MODEL ** -0.5),
        "ffn_w_down": nrm((N_DENSE_LAYERS, D_FF_DENSE, D_MODEL), D_FF_DENSE ** -0.5),
        "router_w": nrm((N_MOE_LAYERS, D_MODEL, N_EXPERTS), D_MODEL ** -0.5),
        "moe_w_gate": nrm((N_MOE_LAYERS, N_EXPERTS, D_MODEL, D_FF_EXPERT), D_MODEL ** -0.5),
        "moe_w_up": nrm((N_MOE_LAYERS, N_EXPERTS, D_MODEL, D_FF_EXPERT), D_MODEL ** -0.5),
        "moe_w_down": nrm((N_MOE_LAYERS, N_EXPERTS, D_FF_EXPERT, D_MODEL), D_FF_EXPERT ** -0.5),
        "final_norm_g": 1.0 + nrm((D_MODEL,), 0.02),
    }


def reference(x, norm_mix_g, w_in, gdn_conv_w, gdn_a_log, gdn_dt_bias, gdn_norm_g,
              rwkv_mu, rwkv_w0, rwkv_w2, rwkv_a0, rwkv_a2, rwkv_g2, rwkv_k_k, rwkv_k_a,
              rwkv_r_k, rwkv_ln_g, rwkv_ln_b, w_branch_gdn, w_branch_rwkv, w_branch_sb,
              w_out, norm_ffn_g, ffn_w_gate, ffn_w_up, ffn_w_down, router_w,
              moe_w_gate, moe_w_up, moe_w_down, final_norm_g):
    B_, T, D = x.shape
    for layer in range(DEPTH):
        hn = rmsnorm(x, norm_mix_g[layer])
        proj = hn @ w_in[layer]
        h_gdn_qkv, h_gdn_z, h_gdn_b, h_gdn_a, h_rwkv, h_sb, h_gate = _split(proj, IN_SPLITS)
        o_a = gated_deltanet(h_gdn_qkv, h_gdn_z, h_gdn_b, h_gdn_a, gdn_conv_w[layer],
                             gdn_a_log[layer], gdn_dt_bias[layer], gdn_norm_g[layer])
        o_b = rwkv7_time_mix(h_rwkv, rwkv_mu[layer], rwkv_w0[layer], rwkv_w2[layer],
                             rwkv_a0[layer], rwkv_a2[layer], rwkv_g2[layer], rwkv_k_k[layer],
                             rwkv_k_a[layer], rwkv_r_k[layer], rwkv_ln_g[layer], rwkv_ln_b[layer])
        o_c = stick_breaking_attention(h_sb)
        gates = jax.nn.sigmoid(h_gate.astype(jnp.float32)).astype(x.dtype).reshape(B_, T, N_BRANCH, D)
        merged = (gates[:, :, 0] * (o_a @ w_branch_gdn[layer])
                  + gates[:, :, 1] * (o_b @ w_branch_rwkv[layer])
                  + gates[:, :, 2] * (o_c @ w_branch_sb[layer]))
        x = x + merged @ w_out[layer]
        hn = rmsnorm(x, norm_ffn_g[layer])
        if layer % 2 == 0:
            i = layer // 2
            x = x + swiglu(hn, ffn_w_gate[i], ffn_w_up[i], ffn_w_down[i])
        else:
            i = layer // 2
            x = x + moe_swiglu(hn, router_w[i], moe_w_gate[i], moe_w_up[i], moe_w_down[i])
    return rmsnorm(x, final_norm_g)
```

```python
import functools

import jax
import jax.numpy as jnp
from jax import lax
from jax.experimental import pallas as pl
from jax.experimental.pallas import tpu as pltpu

F32 = jnp.float32
BF16 = jnp.bfloat16

RMS_EPS = 1e-6
GN_EPS = 64e-5
L2_EPS = 1e-6

D_MODEL = 1024
GDN_HEADS = 4
GDN_HEAD_DIM = 128
GDN_WIDTH = 512
RWKV_WIDTH = 512
RWKV_HEAD_DIM = 64
RWKV_IN = 1792
SB_WIDTH = 512
SB_HEAD_DIM = 64
N_EXPERTS = 8
CHUNK = 64
VMEM_LIMIT_BYTES = 56 * 1024 * 1024


def _dot(a, b):
    return jnp.dot(a.astype(BF16), b.astype(BF16), preferred_element_type=F32)


def _dot_nt(a, b):
    return lax.dot_general(a.astype(BF16), b.astype(BF16), (((1,), (1,)), ((), ())),
                           preferred_element_type=F32)


def _split2(a):
    hi = a.astype(BF16)
    lo = (a - hi.astype(F32)).astype(BF16)
    return hi, lo


def _split3(a):
    hi = a.astype(BF16)
    r = a - hi.astype(F32)
    mid = r.astype(BF16)
    lo = (r - mid.astype(F32)).astype(BF16)
    return hi, mid, lo


def _dot3(a, b):
    ah, al = _split2(a)
    bh, bl = _split2(b)
    return (jnp.dot(ah, bh, preferred_element_type=F32)
            + jnp.dot(ah, bl, preferred_element_type=F32)
            + jnp.dot(al, bh, preferred_element_type=F32))


def _dot_exact_lhs(m, x, parts=3):
    xs = _split3(x) if parts == 3 else _split2(x)
    out = jnp.dot(m, xs[0], preferred_element_type=F32)
    for p in xs[1:]:
        out = out + jnp.dot(m, p, preferred_element_type=F32)
    return out


def _dot_exact_rhs(x, m, parts=2):
    xs = _split3(x) if parts == 3 else _split2(x)
    out = jnp.dot(xs[0], m, preferred_element_type=F32)
    for p in xs[1:]:
        out = out + jnp.dot(p, m, preferred_element_type=F32)
    return out


def _transpose_bf16(x, eye):
    return lax.dot_general(eye, x.astype(BF16), (((1,), (1,)), ((), ())), preferred_element_type=F32)


def _transpose_f32(x, eye):
    hi, mid, lo = _split3(x)
    dn = (((1,), (1,)), ((), ()))
    return (lax.dot_general(eye, hi, dn, preferred_element_type=F32)
            + lax.dot_general(eye, mid, dn, preferred_element_type=F32)
            + lax.dot_general(eye, lo, dn, preferred_element_type=F32))


def _iota2(shape, dim):
    return lax.broadcasted_iota(jnp.int32, shape, dim)


def _eye(n, dtype=F32):
    return (_iota2((n, n), 0) == _iota2((n, n), 1)).astype(dtype)


def _softplus(x):
    return jnp.maximum(x, 0.0) + jnp.log(1.0 + jnp.exp(-jnp.abs(x)))


def _sigmoid(x):
    return 1.0 / (1.0 + jnp.exp(-x))


def _silu(x):
    return x * _sigmoid(x)


def _rms(x, g):
    return x * lax.rsqrt(jnp.mean(x * x, axis=-1, keepdims=True) + RMS_EPS) * g


def _nilpotent_inverse(n, eye, dot):
    t = eye + n
    x = n
    for _ in range(5):
        x = dot(x, x)
        t = t + dot(t, x)
    return t


def _chunk_cumsum_matrix(tt):
    i = _iota2((tt, tt), 0)
    j = _iota2((tt, tt), 1)
    return ((j <= i) & ((i >> 6) == (j >> 6))).astype(BF16)


def _normproj_kernel(x_ref, g_ref, *refs, n_out):
    w_refs = refs[:n_out]
    o_refs = refs[n_out:]
    hn = _rms(x_ref[...], g_ref[...]).astype(BF16)
    for w_ref, o_ref in zip(w_refs, o_refs):
        o_ref[...] = jnp.dot(hn, w_ref[...], preferred_element_type=F32)


def _normproj(x2d, g, weights, tm=256):
    n, d = x2d.shape
    n_out = len(weights)
    in_specs = [pl.BlockSpec((tm, d), lambda i: (i, 0)), pl.BlockSpec((1, d), lambda i: (0, 0))]
    in_specs += [pl.BlockSpec(w.shape, lambda i: (0, 0)) for w in weights]
    out_specs = [pl.BlockSpec((tm, w.shape[1]), lambda i: (i, 0)) for w in weights]
    out_shape = [jax.ShapeDtypeStruct((n, w.shape[1]), F32) for w in weights]
    return pl.pallas_call(
        functools.partial(_normproj_kernel, n_out=n_out),
        grid=(n // tm,),
        in_specs=in_specs,
        out_specs=out_specs,
        out_shape=out_shape,
        compiler_params=pltpu.CompilerParams(
            dimension_semantics=("arbitrary",), vmem_limit_bytes=VMEM_LIMIT_BYTES),
        name="normproj",
    )(x2d, g.reshape(1, d), *weights)


def _gdn_kernel(qkv_ref, z_ref, b_ref, a_ref, cw_ref, alog_ref, dtb_ref, ng_ref, o_ref,
                ext_s, state_s, q_s, k_s, v_s, beta_s, gc_s, p_s, qq_s, r_s, zz_s, oc_s, *, tt):
    nc = tt // CHUNK
    w3 = 3 * GDN_WIDTH
    t = pl.program_id(1)

    @pl.when(t == 0)
    def _():
        ext_s[0:8, :] = jnp.zeros((8, w3), F32)
        state_s[...] = jnp.zeros_like(state_s)

    raw = qkv_ref[0]
    ext_s[8:8 + tt, :] = raw
    cw = cw_ref[...]
    y = raw * cw[3:4, :]
    for i in range(3):
        y = y + ext_s[5 + i:5 + i + tt, :] * cw[i:i + 1, :]
    ext_s[0:8, :] = raw[tt - 8:tt, :]
    y = _silu(y)

    for h in range(GDN_HEADS):
        sl = slice(128 * h, 128 * h + 128)
        qh = y[:, 128 * h:128 * h + 128]
        kh = y[:, GDN_WIDTH + 128 * h:GDN_WIDTH + 128 * h + 128]
        vh = y[:, 2 * GDN_WIDTH + 128 * h:2 * GDN_WIDTH + 128 * h + 128]
        qh = qh * lax.rsqrt(jnp.sum(qh * qh, axis=-1, keepdims=True) + L2_EPS) * (GDN_HEAD_DIM ** -0.5)
        kh = kh * lax.rsqrt(jnp.sum(kh * kh, axis=-1, keepdims=True) + L2_EPS)
        q_s[:, sl] = qh
        k_s[:, sl] = kh
        v_s[:, sl] = vh

    beta_s[...] = _sigmoid(b_ref[0])
    g = -jnp.exp(alog_ref[...]) * _softplus(a_ref[0] + dtb_ref[...])
    gc_s[...] = _dot_exact_lhs(_chunk_cumsum_matrix(tt), g)

    eye64 = _eye(CHUNK)
    eye128 = _eye(128)
    eye128_bf = _eye(128, BF16)
    ii = _iota2((CHUNK, CHUNK), 0)
    jj = _iota2((CHUNK, CHUNK), 1)

    def chunk_body(c, carry):
        rows = pl.ds(pl.multiple_of(c * CHUNK, CHUNK), CHUNK)
        for h in range(GDN_HEADS):
            sl = slice(128 * h, 128 * h + 128)
            q = q_s[rows, sl]
            k = k_s[rows, sl]
            v = v_s[rows, sl]
            beta = beta_s[rows, sl]
            gcc = gc_s[rows, sl]
            gl = gcc[CHUNK - 1:CHUNK, :]
            gcr = _transpose_f32(gcc, eye128_bf)[0:CHUNK, :]
            diff = gcc[:, 0:CHUNK] - gcr
            dec_incl = jnp.exp(jnp.where(ii >= jj, diff, -jnp.inf))
            dec_strict = jnp.where(ii > jj, dec_incl, 0.0)
            eg = jnp.exp(gcc)
            egl = jnp.exp(gl - gcc)
            kb = k * beta
            vb = v * beta
            a_mat = _dot_nt(kb, k) * dec_strict
            tinv = _nilpotent_inverse(-a_mat, eye64, _dot3)
            u = _dot(tinv, vb)
            w = _dot(tinv, kb * eg)
            attn = _dot_nt(q, k) * dec_incl
            kdt = _transpose_bf16(k * egl, eye128_bf)
            p_s[c, h] = jnp.exp(gl) * eye128 - _dot(kdt, w)
            qq_s[c, h] = _dot(kdt, u)
            r_s[c, h] = q * eg - _dot(attn, w)
            zz_s[c, h] = _dot(attn, u)
        return carry

    lax.fori_loop(0, nc, chunk_body, 0)

    def scan_body(c, carry):
        rows = pl.ds(pl.multiple_of(c * CHUNK, CHUNK), CHUNK)
        for h in range(GDN_HEADS):
            sl = slice(128 * h, 128 * h + 128)
            s = state_s[h]
            oc_s[rows, sl] = _dot(r_s[c, h], s) + zz_s[c, h]
            state_s[h] = _dot3(p_s[c, h], s) + qq_s[c, h]
        return carry

    lax.fori_loop(0, nc, scan_body, 0)

    o = oc_s[...]
    z = z_ref[0]
    ng = ng_ref[...]
    for h in range(GDN_HEADS):
        sl = slice(128 * h, 128 * h + 128)
        oh = o[:, sl]
        oh = oh * lax.rsqrt(jnp.mean(oh * oh, axis=-1, keepdims=True) + RMS_EPS) * ng[:, sl]
        o_ref[0, :, sl] = oh * _silu(z[:, sl])


def _gdn(proj_a, conv_w, a_log, dt_bias, norm_g, tt=512):
    b, t, _ = proj_a.shape
    nc = tt // CHUNK
    w3 = 3 * GDN_WIDTH
    rep = lambda p: jnp.repeat(p.astype(F32), 128).reshape(1, GDN_WIDTH)
    small = lambda shape: pl.BlockSpec(shape, lambda i, j: (0, 0))
    return pl.pallas_call(
        functools.partial(_gdn_kernel, tt=tt),
        grid=(b, t // tt),
        in_specs=[
            pl.BlockSpec((1, tt, w3), lambda i, j: (i, j, 0)),
            pl.BlockSpec((1, tt, GDN_WIDTH), lambda i, j: (i, j, 3)),
            pl.BlockSpec((1, tt, GDN_WIDTH), lambda i, j: (i, j, 4)),
            pl.BlockSpec((1, tt, GDN_WIDTH), lambda i, j: (i, j, 5)),
            small((4, w3)), small((1, GDN_WIDTH)), small((1, GDN_WIDTH)), small((1, GDN_WIDTH)),
        ],
        out_specs=pl.BlockSpec((1, tt, GDN_WIDTH), lambda i, j: (i, j, 0)),
        out_shape=jax.ShapeDtypeStruct((b, t, GDN_WIDTH), F32),
        scratch_shapes=[
            pltpu.VMEM((tt + 8, w3), F32),
            pltpu.VMEM((GDN_HEADS, 128, 128), F32),
            pltpu.VMEM((tt, GDN_WIDTH), F32), pltpu.VMEM((tt, GDN_WIDTH), F32),
            pltpu.VMEM((tt, GDN_WIDTH), F32), pltpu.VMEM((tt, GDN_WIDTH), F32),
            pltpu.VMEM((tt, GDN_WIDTH), F32),
            pltpu.VMEM((nc, GDN_HEADS, 128, 128), F32), pltpu.VMEM((nc, GDN_HEADS, 128, 128), F32),
            pltpu.VMEM((nc, GDN_HEADS, CHUNK, 128), F32), pltpu.VMEM((nc, GDN_HEADS, CHUNK, 128), F32),
            pltpu.VMEM((tt, GDN_WIDTH), F32),
        ],
        compiler_params=pltpu.CompilerParams(
            dimension_semantics=("arbitrary", "arbitrary"), vmem_limit_bytes=VMEM_LIMIT_BYTES),
        name="gdn",
    )(proj_a, proj_a, proj_a, proj_a, conv_w.astype(F32), rep(a_log), rep(dt_bias),
      jnp.tile(norm_g.astype(F32), GDN_HEADS).reshape(1, GDN_WIDTH))


def _rwkv_kernel(h_ref, mu_ref, w0_ref, a0_ref, kk_ref, ka_ref, rk_ref, lng_ref, lnb_ref,
                 wa_ref, g2_ref, o_ref,
                 ext_s, state_s, r_s, kn_s, k2_s, v_s, a_s, lw_s, lc_s, p_s, qq_s, rh_s, yc_s, y_s,
                 *, tt):
    nc = tt // CHUNK
    npair = RWKV_WIDTH // 128
    t = pl.program_id(1)

    @pl.when(t == 0)
    def _():
        ext_s[0:8, :] = jnp.zeros((8, RWKV_IN), F32)
        state_s[...] = jnp.zeros_like(state_s)

    raw = h_ref[0]
    ext_s[8:8 + tt, :] = raw
    prev = ext_s[7:7 + tt, :]
    ext_s[0:8, :] = raw[tt - 8:tt, :]
    hl = raw + (prev - raw) * mu_ref[...]
    r = hl[:, 0:512]
    k = hl[:, 512:1024]
    v = hl[:, 1024:1536]
    xwa = hl[:, 1536:1664]
    xg = hl[:, 1664:1792]
    lane128 = _iota2((1, 128), 1)
    xwa = jnp.where(lane128 < 64, jnp.tanh(xwa), xwa)
    lora = _dot3(xwa, wa_ref[...])
    w_log = -_softplus(-(w0_ref[...] + lora[:, 0:512])) - 0.5
    lw = -jnp.exp(w_log)
    a = _sigmoid(a0_ref[...] + lora[:, 512:1024])
    gate = _dot3(_sigmoid(xg), g2_ref[...])
    kk = k * kk_ref[...]
    k2 = k * (1.0 + (a - 1.0) * ka_ref[...])
    si = _iota2((RWKV_WIDTH, RWKV_WIDTH), 0)
    sj = _iota2((RWKV_WIDTH, RWKV_WIDTH), 1)
    seg = ((si >> 6) == (sj >> 6)).astype(BF16)
    kn = kk * lax.rsqrt(_dot_exact_rhs(kk * kk, seg) + L2_EPS)
    r_s[...] = r
    kn_s[...] = kn
    k2_s[...] = k2
    v_s[...] = v
    a_s[...] = a
    lw_s[...] = lw
    lc_s[...] = _dot_exact_lhs(_chunk_cumsum_matrix(tt), lw)

    eye64 = _eye(CHUNK)
    eye128 = _eye(128)
    eye128_bf = _eye(128, BF16)
    ii = _iota2((CHUNK, CHUNK), 0)
    jj = _iota2((CHUNK, CHUNK), 1)
    strict = ii > jj
    incl = ii >= jj
    lane = _iota2((CHUNK, 128), 1)
    bi = _iota2((128, 128), 0)
    bj = _iota2((128, 128), 1)
    blockdiag = (bi < 64) == (bj < 64)

    def chunk_body(c, carry):
        rows = pl.ds(pl.multiple_of(c * CHUNK, CHUNK), CHUNK)
        for p in range(npair):
            sl = slice(128 * p, 128 * p + 128)
            lcc = lc_s[rows, sl]
            lwc = lw_s[rows, sl]
            rc = r_s[rows, sl]
            knc = kn_s[rows, sl]
            k2c = k2_s[rows, sl]
            vc = v_s[rows, sl]
            ac = a_s[rows, sl]
            lcl = lcc[CHUNK - 1:CHUNK, :]
            ginv = jnp.exp(-lcc)
            gend = jnp.exp(lcl - lcc)
            kna = knc * ac
            at = -knc * jnp.exp(lcc - lwc)
            bt = kna * ginv
            kt = k2c * ginv
            rt = rc * jnp.exp(lcc)
            ahat = None
            rhat = None
            uv = None
            yc = None
            for j in range(2):
                m = (lane < 64) if j == 0 else (lane >= 64)
                at_m = jnp.where(m, at, 0.0)
                rt_m = jnp.where(m, rt, 0.0)
                a_ab = jnp.where(strict, _dot3_nt(at_m, bt), 0.0)
                a_ak = jnp.where(strict, _dot3_nt(at_m, kt), 0.0)
                a_rb = jnp.where(incl, _dot3_nt(rt_m, bt), 0.0)
                a_rk = jnp.where(incl, _dot3_nt(rt_m, kt), 0.0)
                tinv = _nilpotent_inverse(a_ab, eye64, _dot3)
                ahat_j = _dot(tinv, at_m)
                uv_j = _dot(tinv, _dot(a_ak, vc))
                rhat_j = rt_m + _dot(a_rb, ahat_j)
                yc_j = _dot(a_rb, uv_j) + _dot(a_rk, vc)
                if j == 0:
                    ahat, rhat, uv, yc = ahat_j, rhat_j, uv_j, yc_j
                else:
                    ahat = ahat + ahat_j
                    rhat = rhat + rhat_j
                    uv = jnp.where(lane < 64, uv, uv_j)
                    yc = jnp.where(lane < 64, yc, yc_j)
            bbt = _transpose_bf16(kna * gend, eye128_bf)
            kbt = _transpose_bf16(k2c * gend, eye128_bf)
            p_s[c, p] = jnp.where(blockdiag, _dot(bbt, ahat), 0.0) + eye128 * jnp.exp(lcl)
            qq_s[c, p] = jnp.where(blockdiag, _dot(bbt, uv) + _dot(kbt, vc), 0.0)
            rh_s[c, p] = rhat
            yc_s[c, p] = yc
        return carry

    lax.fori_loop(0, nc, chunk_body, 0)

    def scan_body(c, carry):
        rows = pl.ds(pl.multiple_of(c * CHUNK, CHUNK), CHUNK)
        for p in range(npair):
            sl = slice(128 * p, 128 * p + 128)
            s = state_s[p]
            y_s[rows, sl] = _dot(rh_s[c, p], s) + yc_s[c, p]
            state_s[p] = _dot3(p_s[c, p], s) + qq_s[c, p]
        return carry

    lax.fori_loop(0, nc, scan_body, 0)

    y = y_s[...]
    mean = _dot_exact_rhs(y, seg) * (1.0 / RWKV_HEAD_DIM)
    yc = y - mean
    var = _dot_exact_rhs(yc * yc, seg) * (1.0 / RWKV_HEAD_DIM)
    yn = yc * lax.rsqrt(var + GN_EPS) * lng_ref[...] + lnb_ref[...]
    bonus = _dot_exact_rhs(r * k2 * rk_ref[...], seg) * v
    o_ref[0] = (yn + bonus) * gate


def _dot3_nt(a, b):
    ah, al = _split2(a)
    bh, bl = _split2(b)
    dn = (((1,), (1,)), ((), ()))
    return (lax.dot_general(ah, bh, dn, preferred_element_type=F32)
            + lax.dot_general(ah, bl, dn, preferred_element_type=F32)
            + lax.dot_general(al, bh, dn, preferred_element_type=F32))


def _rwkv(h, mu, w0, w2, a0, a2, g2, k_k, k_a, r_k, ln_g, ln_b, tt=512):
    b, t, _ = h.shape
    nc = tt // CHUNK
    npair = RWKV_WIDTH // 128
    row = lambda p: p.astype(F32).reshape(1, -1)
    wa = jnp.zeros((128, 2 * RWKV_WIDTH), F32)
    wa = wa.at[0:64, 0:RWKV_WIDTH].set(w2.astype(F32)).at[64:128, RWKV_WIDTH:].set(a2.astype(F32))
    small = lambda shape: pl.BlockSpec(shape, lambda i, j: (0, 0))
    vec = small((1, RWKV_WIDTH))
    return pl.pallas_call(
        functools.partial(_rwkv_kernel, tt=tt),
        grid=(b, t // tt),
        in_specs=[pl.BlockSpec((1, tt, RWKV_IN), lambda i, j: (i, j, 0)), small((1, RWKV_IN)),
                  vec, vec, vec, vec, vec, vec, vec,
                  small((128, 2 * RWKV_WIDTH)), small((128, RWKV_WIDTH))],
        out_specs=pl.BlockSpec((1, tt, RWKV_WIDTH), lambda i, j: (i, j, 0)),
        out_shape=jax.ShapeDtypeStruct((b, t, RWKV_WIDTH), F32),
        scratch_shapes=[
            pltpu.VMEM((tt + 8, RWKV_IN), F32),
            pltpu.VMEM((npair, 128, 128), F32),
        ] + [pltpu.VMEM((tt, RWKV_WIDTH), F32)] * 7 + [
            pltpu.VMEM((nc, npair, 128, 128), F32), pltpu.VMEM((nc, npair, 128, 128), F32),
            pltpu.VMEM((nc, npair, CHUNK, 128), F32), pltpu.VMEM((nc, npair, CHUNK, 128), F32),
            pltpu.VMEM((tt, RWKV_WIDTH), F32),
        ],
        compiler_params=pltpu.CompilerParams(
            dimension_semantics=("arbitrary", "arbitrary"), vmem_limit_bytes=VMEM_LIMIT_BYTES),
        name="rwkv7",
    )(h, row(mu), row(w0), row(a0), row(k_k), row(k_a), row(r_k), row(ln_g), row(ln_b),
      wa, g2.astype(F32))


def _sb_kernel(q_ref, k_ref, v_ref, o_ref, acc_s, run_s, *, bq, bk):
    qi = pl.program_id(2)
    nsub = bq // bk
    q = q_ref[0] * (SB_HEAD_DIM ** -0.5)
    lane = _iota2((1, 128), 1)
    qm = [jnp.where(lane < 64, q, 0.0).astype(BF16), jnp.where(lane >= 64, q, 0.0).astype(BF16)]
    ti = _iota2((bk, 2 * bk), 0)
    tj = _iota2((bk, 2 * bk), 1)
    tri_ones = ((ti > tj) | (tj >= bk)).astype(BF16)
    qpos = qi * bq + _iota2((bq, bk), 0)
    kcol = _iota2((bq, bk), 1)
    acc_s[...] = jnp.zeros_like(acc_s)
    run_s[...] = jnp.zeros_like(run_s)

    def step(kb, masked):
        start = pl.multiple_of(kb * bk, bk)
        k = k_ref[0, pl.ds(start, bk), :].astype(BF16)
        v = v_ref[0, pl.ds(start, bk), :].astype(BF16)
        if masked:
            msk = (start + kcol) < qpos
        for j in range(2):
            z = lax.dot_general(qm[j], k, (((1,), (1,)), ((), ())), preferred_element_type=F32)
            lsig = jnp.minimum(z, 0.0) - jnp.log(1.0 + jnp.exp(-jnp.abs(z)))
            l1 = lsig - z
            if masked:
                l1 = jnp.where(msk, l1, 0.0)
            cr = _dot_exact_rhs(l1, tri_ones)
            att = jnp.exp(lsig + cr[:, 0:bk] + run_s[j])
            if masked:
                att = jnp.where(msk, att, 0.0)
            acc_s[j] += jnp.dot(att.astype(BF16), v, preferred_element_type=F32)
            run_s[j] += cr[:, bk:2 * bk]

    for d in range(nsub - 1, -1, -1):
        step(qi * nsub + d, True)

    def body(i, carry):
        step(qi * nsub - 1 - i, False)
        return carry

    lax.fori_loop(0, qi * nsub, body, 0)
    o_ref[0] = jnp.where(lane < 64, acc_s[0], acc_s[1])


def _stick_breaking(h_sb, bq=512, bk=128):
    b, t, _ = h_sb.shape
    npair = SB_WIDTH // 128
    return pl.pallas_call(
        functools.partial(_sb_kernel, bq=bq, bk=bk),
        grid=(b, npair, t // bq),
        in_specs=[
            pl.BlockSpec((1, bq, 128), lambda i, p, j: (i, j, p)),
            pl.BlockSpec((1, t, 128), lambda i, p, j: (i, 0, npair + p)),
            pl.BlockSpec((1, t, 128), lambda i, p, j: (i, 0, 2 * npair + p)),
        ],
        out_specs=pl.BlockSpec((1, bq, 128), lambda i, p, j: (i, j, p)),
        out_shape=jax.ShapeDtypeStruct((b, t, SB_WIDTH), F32),
        scratch_shapes=[pltpu.VMEM((2, bq, 128), F32), pltpu.VMEM((2, bq, bk), F32)],
        compiler_params=pltpu.CompilerParams(
            dimension_semantics=("arbitrary", "arbitrary", "arbitrary"),
            vmem_limit_bytes=VMEM_LIMIT_BYTES),
        name="stickbreak",
    )(h_sb, h_sb, h_sb)


def _merge_kernel(x_ref, oa_ref, ob_ref, oc_ref, gt_ref, wa_ref, wb_ref, wc_ref, wo_ref, o_ref):
    d = D_MODEL
    g = _sigmoid(gt_ref[...])
    m = (g[:, 0:d] * _dot(oa_ref[...], wa_ref[...])
         + g[:, d:2 * d] * _dot(ob_ref[...], wb_ref[...])
         + g[:, 2 * d:3 * d] * _dot(oc_ref[...], wc_ref[...]))
    o_ref[...] = x_ref[...] + _dot(m, wo_ref[...])


def _merge(x2d, oa, ob, oc, gates, wa, wb, wc, wo, tm=512):
    n, d = x2d.shape
    rowspec = lambda w: pl.BlockSpec((tm, w), lambda i: (i, 0))
    full = lambda w: pl.BlockSpec(w.shape, lambda i: (0, 0))
    return pl.pallas_call(
        _merge_kernel,
        grid=(n // tm,),
        in_specs=[rowspec(d), rowspec(512), rowspec(512), rowspec(512), rowspec(3 * d),
                  full(wa), full(wb), full(wc), full(wo)],
        out_specs=rowspec(d),
        out_shape=jax.ShapeDtypeStruct((n, d), F32),
        compiler_params=pltpu.CompilerParams(
            dimension_semantics=("arbitrary",), vmem_limit_bytes=VMEM_LIMIT_BYTES),
        name="merge",
    )(x2d, oa, ob, oc, gates, wa, wb, wc, wo)


def _router_kernel(x_ref, g_ref, rw_ref, o_ref):
    hn = _rms(x_ref[...], g_ref[...])
    logits = _dot3(hn, rw_ref[...])
    lane = _iota2(logits.shape, 1)
    m1 = jnp.max(logits, axis=-1, keepdims=True)
    i1 = jnp.min(jnp.where(logits == m1, lane, N_EXPERTS), axis=-1, keepdims=True)
    sel1 = lane == i1
    rest = jnp.where(sel1, -jnp.inf, logits)
    m2 = jnp.max(rest, axis=-1, keepdims=True)
    i2 = jnp.min(jnp.where(rest == m2, lane, N_EXPERTS), axis=-1, keepdims=True)
    sel2 = lane == i2
    e2 = jnp.exp(m2 - m1)
    w1 = 1.0 / (1.0 + e2)
    o_ref[...] = jnp.where(sel1, w1, 0.0) + jnp.where(sel2, e2 * w1, 0.0)


def _router(x2d, g, rw, tm=512):
    n, d = x2d.shape
    return pl.pallas_call(
        _router_kernel,
        grid=(n // tm,),
        in_specs=[pl.BlockSpec((tm, d), lambda i: (i, 0)), pl.BlockSpec((1, d), lambda i: (0, 0)),
                  pl.BlockSpec(rw.shape, lambda i: (0, 0))],
        out_specs=pl.BlockSpec((tm, N_EXPERTS), lambda i: (i, 0)),
        out_shape=jax.ShapeDtypeStruct((n, N_EXPERTS), F32),
        compiler_params=pltpu.CompilerParams(dimension_semantics=("arbitrary",)),
        name="router",
    )(x2d, g.reshape(1, d), rw.astype(F32))


def _ffn_kernel(*refs, moe, final):
    if moe:
        x_ref, g_ref, comb_ref, wg_ref, wu_ref, wd_ref = refs[:6]
        rest = refs[6:]
    else:
        x_ref, g_ref, wg_ref, wu_ref, wd_ref = refs[:5]
        rest = refs[5:]
    if final:
        fg_ref, o_ref, hn_s, acc_s = rest
    else:
        o_ref, hn_s, acc_s = rest
    if moe:
        e = pl.program_id(1)
        f = pl.program_id(2)
        first = (e == 0) & (f == 0)
        last = (e == pl.num_programs(1) - 1) & (f == pl.num_programs(2) - 1)
    else:
        f = pl.program_id(1)
        first = f == 0
        last = f == pl.num_programs(1) - 1

    @pl.when(first)
    def _():
        hn_s[...] = _rms(x_ref[...], g_ref[...]).astype(BF16)
        acc_s[...] = jnp.zeros_like(acc_s)

    hn = hn_s[...]
    act = _silu(jnp.dot(hn, wg_ref[...], preferred_element_type=F32)) * jnp.dot(
        hn, wu_ref[...], preferred_element_type=F32)
    if moe:
        comb = comb_ref[...]
        lane = _iota2(comb.shape, 1)
        act = act * jnp.sum(jnp.where(lane == e, comb, 0.0), axis=-1, keepdims=True)
    acc_s[...] += jnp.dot(act.astype(BF16), wd_ref[...], preferred_element_type=F32)

    @pl.when(last)
    def _():
        out = x_ref[...] + acc_s[...]
        if final:
            out = _rms(out, fg_ref[...])
        o_ref[...] = out


def _ffn(x2d, g, wg, wu, wd, comb=None, final_g=None, tm=1024, tf=256):
    n, d = x2d.shape
    moe = comb is not None
    final = final_g is not None
    if moe:
        ne, _, dff = wg.shape
        grid = (n // tm, ne, dff // tf)
        xmap = lambda i, e, f: (i, 0)
        cmap = lambda i, e, f: (0, 0)
        in_specs = [pl.BlockSpec((tm, d), xmap), pl.BlockSpec((1, d), cmap),
                    pl.BlockSpec((tm, N_EXPERTS), xmap),
                    pl.BlockSpec((None, d, tf), lambda i, e, f: (e, 0, f)),
                    pl.BlockSpec((None, d, tf), lambda i, e, f: (e, 0, f)),
                    pl.BlockSpec((None, tf, d), lambda i, e, f: (e, f, 0))]
        args = [x2d, g.reshape(1, d), comb, wg, wu, wd]
        sem = ("arbitrary", "arbitrary", "arbitrary")
    else:
        dff = wg.shape[1]
        grid = (n // tm, dff // tf)
        xmap = lambda i, f: (i, 0)
        cmap = lambda i, f: (0, 0)
        in_specs = [pl.BlockSpec((tm, d), xmap), pl.BlockSpec((1, d), cmap),
                    pl.BlockSpec((d, tf), lambda i, f: (0, f)),
                    pl.BlockSpec((d, tf), lambda i, f: (0, f)),
                    pl.BlockSpec((tf, d), lambda i, f: (f, 0))]
        args = [x2d, g.reshape(1, d), wg, wu, wd]
        sem = ("arbitrary", "arbitrary")
    if final:
        in_specs.append(pl.BlockSpec((1, d), cmap))
        args.append(final_g.reshape(1, d))
    return pl.pallas_call(
        functools.partial(_ffn_kernel, moe=moe, final=final),
        grid=grid,
        in_specs=in_specs,
        out_specs=pl.BlockSpec((tm, d), xmap),
        out_shape=jax.ShapeDtypeStruct((n, d), F32),
        scratch_shapes=[pltpu.VMEM((tm, d), BF16), pltpu.VMEM((tm, d), F32)],
        compiler_params=pltpu.CompilerParams(dimension_semantics=sem, vmem_limit_bytes=VMEM_LIMIT_BYTES),
        name="moe_ffn" if moe else "dense_ffn",
    )(*args)


def kernel(x, norm_mix_g, w_in, gdn_conv_w, gdn_a_log, gdn_dt_bias, gdn_norm_g, rwkv_mu, rwkv_w0, rwkv_w2, rwkv_a0, rwkv_a2, rwkv_g2, rwkv_k_k, rwkv_k_a, rwkv_r_k, rwkv_ln_g, rwkv_ln_b, w_branch_gdn, w_branch_rwkv, w_branch_sb, w_out, norm_ffn_g, ffn_w_gate, ffn_w_up, ffn_w_down, router_w, moe_w_gate, moe_w_up, moe_w_down, final_norm_g):
    b, t, d = x.shape
    n = b * t
    depth = w_in.shape[0]
    x2 = x.reshape(n, d).astype(F32)
    for layer in range(depth):
        w = w_in[layer]
        w_gdn = jnp.concatenate(
            [w[:, 0:2048], jnp.repeat(w[:, 2048:2052], 128, axis=1), jnp.repeat(w[:, 2052:2056], 128, axis=1)],
            axis=1).astype(BF16)
        w_rwkv = w[:, 2056:3848].astype(BF16)
        w_sb = w[:, 3848:5384].astype(BF16)
        w_gates = w[:, 5384:8456].astype(BF16)
        g_mix = norm_mix_g[layer].astype(F32)
        p_gdn, p_rwkv = _normproj(x2, g_mix, [w_gdn, w_rwkv])
        p_sb, p_gates = _normproj(x2, g_mix, [w_sb, w_gates])
        o_a = _gdn(p_gdn.reshape(b, t, -1), gdn_conv_w[layer], gdn_a_log[layer], gdn_dt_bias[layer],
                   gdn_norm_g[layer])
        o_b = _rwkv(p_rwkv.reshape(b, t, -1), rwkv_mu[layer], rwkv_w0[layer], rwkv_w2[layer],
                    rwkv_a0[layer], rwkv_a2[layer], rwkv_g2[layer], rwkv_k_k[layer], rwkv_k_a[layer],
                    rwkv_r_k[layer].reshape(-1), rwkv_ln_g[layer], rwkv_ln_b[layer])
        o_c = _stick_breaking(p_sb.reshape(b, t, -1))
        x2 = _merge(x2, o_a.reshape(n, -1), o_b.reshape(n, -1), o_c.reshape(n, -1), p_gates,
                    w_branch_gdn[layer].astype(BF16), w_branch_rwkv[layer].astype(BF16),
                    w_branch_sb[layer].astype(BF16), w_out[layer].astype(BF16))
        g_ffn = norm_ffn_g[layer].astype(F32)
        final_g = final_norm_g.astype(F32) if layer == depth - 1 else None
        i = layer // 2
        if layer % 2 == 0:
            x2 = _ffn(x2, g_ffn, ffn_w_gate[i].astype(BF16), ffn_w_up[i].astype(BF16),
                      ffn_w_down[i].astype(BF16), final_g=final_g, tf=256)
        else:
            comb = _router(x2, g_ffn, router_w[i])
            x2 = _ffn(x2, g_ffn, moe_w_gate[i].astype(BF16), moe_w_up[i].astype(BF16),
                      moe_w_down[i].astype(BF16), comb=comb, final_g=final_g, tf=512)
    return x2.reshape(b, t, d)
```

```python
import functools

import jax
import jax.numpy as jnp
from jax import lax
from jax.experimental import pallas as pl
from jax.experimental.pallas import tpu as pltpu

F32 = jnp.float32
BF16 = jnp.bfloat16

RMS_EPS = 1e-6
GN_EPS = 64e-5
L2_EPS = 1e-6

D_MODEL = 1024
GDN_HEADS = 4
GDN_HEAD_DIM = 128
GDN_WIDTH = 512
RWKV_WIDTH = 512
RWKV_HEAD_DIM = 64
RWKV_IN = 1792
SB_WIDTH = 512
SB_HEAD_DIM = 64
N_EXPERTS = 8
CHUNK = 64
LOG2E = 1.4426950408889634
SB_LOG2_CUTOFF = -160.0
SB_ROW_TILE = 128
GDN_CHUNKS_PER_STEP = 2
VMEM_LIMIT_BYTES = 56 * 1024 * 1024


def _dot(a, b):
    return jnp.dot(a.astype(BF16), b.astype(BF16), preferred_element_type=F32)


def _dot_nt(a, b):
    return lax.dot_general(a.astype(BF16), b.astype(BF16), (((1,), (1,)), ((), ())),
                           preferred_element_type=F32)


def _split2(a):
    hi = a.astype(BF16)
    lo = (a - hi.astype(F32)).astype(BF16)
    return hi, lo


def _split3(a):
    hi = a.astype(BF16)
    r = a - hi.astype(F32)
    mid = r.astype(BF16)
    lo = (r - mid.astype(F32)).astype(BF16)
    return hi, mid, lo


def _dot3(a, b):
    ah, al = _split2(a)
    bh, bl = _split2(b)
    return (jnp.dot(ah, bh, preferred_element_type=F32)
            + jnp.dot(ah, bl, preferred_element_type=F32)
            + jnp.dot(al, bh, preferred_element_type=F32))


def _dot_exact_lhs(m, x, parts=3):
    xs = _split3(x) if parts == 3 else _split2(x)
    out = jnp.dot(m, xs[0], preferred_element_type=F32)
    for p in xs[1:]:
        out = out + jnp.dot(m, p, preferred_element_type=F32)
    return out


def _dot_exact_rhs(x, m, parts=2):
    xs = _split3(x) if parts == 3 else _split2(x)
    out = jnp.dot(xs[0], m, preferred_element_type=F32)
    for p in xs[1:]:
        out = out + jnp.dot(p, m, preferred_element_type=F32)
    return out


def _transpose_bf16(x, eye):
    return lax.dot_general(eye, x.astype(BF16), (((1,), (1,)), ((), ())), preferred_element_type=F32)


def _transpose_f32(x, eye):
    hi, mid, lo = _split3(x)
    dn = (((1,), (1,)), ((), ()))
    return (lax.dot_general(eye, hi, dn, preferred_element_type=F32)
            + lax.dot_general(eye, mid, dn, preferred_element_type=F32)
            + lax.dot_general(eye, lo, dn, preferred_element_type=F32))


def _iota2(shape, dim):
    return lax.broadcasted_iota(jnp.int32, shape, dim)


def _eye(n, dtype=F32):
    return (_iota2((n, n), 0) == _iota2((n, n), 1)).astype(dtype)


def _softplus(x):
    return jnp.maximum(x, 0.0) + jnp.log(1.0 + jnp.exp(-jnp.abs(x)))


def _sigmoid(x):
    return 1.0 / (1.0 + jnp.exp(-x))


def _silu(x):
    return x * _sigmoid(x)


def _rms(x, g):
    return x * lax.rsqrt(jnp.mean(x * x, axis=-1, keepdims=True) + RMS_EPS) * g


def _nilpotent_inverse(n, eye, dot):
    t = eye + n
    x = n
    for _ in range(5):
        x = dot(x, x)
        t = t + dot(t, x)
    return t


def _nilpotent_inverse_many(ns, eye, dot):
    ts = [eye + n for n in ns]
    xs = list(ns)
    for _ in range(5):
        xs = [dot(x, x) for x in xs]
        ts = [t + dot(t, x) for t, x in zip(ts, xs)]
    return ts


def _chunk_cumsum(x):
    i = _iota2((128, 128), 0)
    j = _iota2((128, 128), 1)
    m = ((j <= i) & ((i >> 6) == (j >> 6))).astype(BF16)
    return jnp.concatenate(
        [_dot_exact_lhs(m, x[r:r + 128]) for r in range(0, x.shape[0], 128)], axis=0)


def _head_sum(x):
    i = _iota2((128, 128), 0)
    j = _iota2((128, 128), 1)
    m = ((i >> 6) == (j >> 6)).astype(BF16)
    return jnp.concatenate(
        [_dot_exact_rhs(x[:, c:c + 128], m) for c in range(0, x.shape[1], 128)], axis=1)


def _normproj_kernel(x_ref, g_ref, *refs, n_out):
    w_refs = refs[:n_out]
    o_refs = refs[n_out:]
    hn = _rms(x_ref[...], g_ref[...]).astype(BF16)
    for w_ref, o_ref in zip(w_refs, o_refs):
        o_ref[...] = jnp.dot(hn, w_ref[...], preferred_element_type=F32)


def _normproj(x2d, g, weights, tm=256):
    n, d = x2d.shape
    n_out = len(weights)
    in_specs = [pl.BlockSpec((tm, d), lambda i: (i, 0)), pl.BlockSpec((1, d), lambda i: (0, 0))]
    in_specs += [pl.BlockSpec(w.shape, lambda i: (0, 0)) for w in weights]
    out_specs = [pl.BlockSpec((tm, w.shape[1]), lambda i: (i, 0)) for w in weights]
    out_shape = [jax.ShapeDtypeStruct((n, w.shape[1]), F32) for w in weights]
    return pl.pallas_call(
        functools.partial(_normproj_kernel, n_out=n_out),
        grid=(n // tm,),
        in_specs=in_specs,
        out_specs=out_specs,
        out_shape=out_shape,
        compiler_params=pltpu.CompilerParams(
            dimension_semantics=("arbitrary",), vmem_limit_bytes=VMEM_LIMIT_BYTES),
        name="normproj",
    )(x2d, g.reshape(1, d), *weights)


def _gdn_kernel(qkv_ref, z_ref, b_ref, a_ref, cw_ref, alog_ref, dtb_ref, ng_ref, o_ref,
                ext_s, state_s, q_s, k_s, v_s, beta_s, gc_s, p_s, qq_s, r_s, zz_s, oc_s, *, tt):
    nc = tt // CHUNK
    w3 = 3 * GDN_WIDTH
    t = pl.program_id(1)

    @pl.when(t == 0)
    def _():
        ext_s[0:8, :] = jnp.zeros((8, w3), F32)
        state_s[...] = jnp.zeros_like(state_s)

    raw = qkv_ref[0]
    ext_s[8:8 + tt, :] = raw
    cw = cw_ref[...]
    y = raw * cw[3:4, :]
    for i in range(3):
        y = y + ext_s[5 + i:5 + i + tt, :] * cw[i:i + 1, :]
    ext_s[0:8, :] = raw[tt - 8:tt, :]
    y = _silu(y)

    for h in range(GDN_HEADS):
        sl = slice(128 * h, 128 * h + 128)
        qh = y[:, 128 * h:128 * h + 128]
        kh = y[:, GDN_WIDTH + 128 * h:GDN_WIDTH + 128 * h + 128]
        vh = y[:, 2 * GDN_WIDTH + 128 * h:2 * GDN_WIDTH + 128 * h + 128]
        qh = qh * lax.rsqrt(jnp.sum(qh * qh, axis=-1, keepdims=True) + L2_EPS) * (GDN_HEAD_DIM ** -0.5)
        kh = kh * lax.rsqrt(jnp.sum(kh * kh, axis=-1, keepdims=True) + L2_EPS)
        q_s[:, sl] = qh
        k_s[:, sl] = kh
        v_s[:, sl] = vh

    beta_s[...] = _sigmoid(b_ref[0])
    g = -jnp.exp(alog_ref[...]) * _softplus(a_ref[0] + dtb_ref[...])
    gc_s[...] = _chunk_cumsum(g)

    eye64 = _eye(CHUNK)
    eye128 = _eye(128)
    eye128_bf = _eye(128, BF16)
    ii = _iota2((CHUNK, CHUNK), 0)
    jj = _iota2((CHUNK, CHUNK), 1)

    def chunk_body(ci, carry):
        probs = [(ci * GDN_CHUNKS_PER_STEP + u, h) for u in range(GDN_CHUNKS_PER_STEP)
                 for h in range(GDN_HEADS)]
        rows = [pl.ds(pl.multiple_of(c * CHUNK, CHUNK), CHUNK) for c, _ in probs]
        lanes = [slice(128 * h, 128 * h + 128) for _, h in probs]
        idx = range(len(probs))
        q = [q_s[rows[i], lanes[i]] for i in idx]
        k = [k_s[rows[i], lanes[i]] for i in idx]
        gcc = [gc_s[rows[i], lanes[i]] for i in idx]
        gl = [g[CHUNK - 1:CHUNK, :] for g in gcc]
        gcr = [_transpose_f32(g, eye128_bf)[0:CHUNK, :] for g in gcc]
        dec_incl = [jnp.exp(jnp.where(ii >= jj, gcc[i][:, 0:CHUNK] - gcr[i], -jnp.inf)) for i in idx]
        kb = [k[i] * beta_s[rows[i], lanes[i]] for i in idx]
        a_mat = [_dot_nt(kb[i], k[i]) for i in idx]
        attn = [_dot_nt(q[i], k[i]) * dec_incl[i] for i in idx]
        kdt = [_transpose_bf16(k[i] * jnp.exp(gl[i] - gcc[i]), eye128_bf) for i in idx]
        tinv = _nilpotent_inverse_many(
            [-a_mat[i] * jnp.where(ii > jj, dec_incl[i], 0.0) for i in idx], eye64, _dot)
        u = [_dot(tinv[i], v_s[rows[i], lanes[i]] * beta_s[rows[i], lanes[i]]) for i in idx]
        w = [_dot(tinv[i], kb[i] * jnp.exp(gcc[i])) for i in idx]
        for i, (c, h) in enumerate(probs):
            p_s[c, h] = jnp.exp(gl[i]) * eye128 - _dot(kdt[i], w[i])
        for i, (c, h) in enumerate(probs):
            qq_s[c, h] = _dot(kdt[i], u[i])
        for i, (c, h) in enumerate(probs):
            r_s[c, h] = q[i] * jnp.exp(gcc[i]) - _dot(attn[i], w[i])
        for i, (c, h) in enumerate(probs):
            zz_s[c, h] = _dot(attn[i], u[i])
        return carry

    lax.fori_loop(0, nc // GDN_CHUNKS_PER_STEP, chunk_body, 0)

    def scan_body(c, carry):
        rows = pl.ds(pl.multiple_of(c * CHUNK, CHUNK), CHUNK)
        s = [state_s[h] for h in range(GDN_HEADS)]
        s_new = [_dot3(p_s[c, h], s[h]) for h in range(GDN_HEADS)]
        o = [_dot(r_s[c, h], s[h]) for h in range(GDN_HEADS)]
        for h in range(GDN_HEADS):
            state_s[h] = s_new[h] + qq_s[c, h]
            oc_s[rows, 128 * h:128 * h + 128] = o[h] + zz_s[c, h]
        return carry

    lax.fori_loop(0, nc, scan_body, 0)

    o = oc_s[...]
    z = z_ref[0]
    ng = ng_ref[...]
    for h in range(GDN_HEADS):
        sl = slice(128 * h, 128 * h + 128)
        oh = o[:, sl]
        oh = oh * lax.rsqrt(jnp.mean(oh * oh, axis=-1, keepdims=True) + RMS_EPS) * ng[:, sl]
        o_ref[0, :, sl] = oh * _silu(z[:, sl])


def _gdn(proj_a, conv_w, a_log, dt_bias, norm_g, tt=512):
    b, t, _ = proj_a.shape
    nc = tt // CHUNK
    w3 = 3 * GDN_WIDTH
    rep = lambda p: jnp.repeat(p.astype(F32), 128).reshape(1, GDN_WIDTH)
    small = lambda shape: pl.BlockSpec(shape, lambda i, j: (0, 0))
    return pl.pallas_call(
        functools.partial(_gdn_kernel, tt=tt),
        grid=(b, t // tt),
        in_specs=[
            pl.BlockSpec((1, tt, w3), lambda i, j: (i, j, 0)),
            pl.BlockSpec((1, tt, GDN_WIDTH), lambda i, j: (i, j, 3)),
            pl.BlockSpec((1, tt, GDN_WIDTH), lambda i, j: (i, j, 4)),
            pl.BlockSpec((1, tt, GDN_WIDTH), lambda i, j: (i, j, 5)),
            small((4, w3)), small((1, GDN_WIDTH)), small((1, GDN_WIDTH)), small((1, GDN_WIDTH)),
        ],
        out_specs=pl.BlockSpec((1, tt, GDN_WIDTH), lambda i, j: (i, j, 0)),
        out_shape=jax.ShapeDtypeStruct((b, t, GDN_WIDTH), F32),
        scratch_shapes=[
            pltpu.VMEM((tt + 8, w3), F32),
            pltpu.VMEM((GDN_HEADS, 128, 128), F32),
            pltpu.VMEM((tt, GDN_WIDTH), F32), pltpu.VMEM((tt, GDN_WIDTH), F32),
            pltpu.VMEM((tt, GDN_WIDTH), F32), pltpu.VMEM((tt, GDN_WIDTH), F32),
            pltpu.VMEM((tt, GDN_WIDTH), F32),
            pltpu.VMEM((nc, GDN_HEADS, 128, 128), F32), pltpu.VMEM((nc, GDN_HEADS, 128, 128), F32),
            pltpu.VMEM((nc, GDN_HEADS, CHUNK, 128), F32), pltpu.VMEM((nc, GDN_HEADS, CHUNK, 128), F32),
            pltpu.VMEM((tt, GDN_WIDTH), F32),
        ],
        compiler_params=pltpu.CompilerParams(
            dimension_semantics=("arbitrary", "arbitrary"), vmem_limit_bytes=VMEM_LIMIT_BYTES),
        name="gdn",
    )(proj_a, proj_a, proj_a, proj_a, conv_w.astype(F32), rep(a_log), rep(dt_bias),
      jnp.tile(norm_g.astype(F32), GDN_HEADS).reshape(1, GDN_WIDTH))


def _rwkv_kernel(h_ref, mu_ref, w0_ref, a0_ref, kk_ref, ka_ref, rk_ref, lng_ref, lnb_ref,
                 wa_ref, g2_ref, o_ref,
                 ext_s, state_s, r_s, kn_s, k2_s, v_s, a_s, lw_s, lc_s, p_s, qq_s, rh_s, yc_s, y_s,
                 *, tt):
    nc = tt // CHUNK
    npair = RWKV_WIDTH // 128
    t = pl.program_id(1)

    @pl.when(t == 0)
    def _():
        ext_s[0:8, :] = jnp.zeros((8, RWKV_IN), F32)
        state_s[...] = jnp.zeros_like(state_s)

    raw = h_ref[0]
    ext_s[8:8 + tt, :] = raw
    prev = ext_s[7:7 + tt, :]
    ext_s[0:8, :] = raw[tt - 8:tt, :]
    hl = raw + (prev - raw) * mu_ref[...]
    r = hl[:, 0:512]
    k = hl[:, 512:1024]
    v = hl[:, 1024:1536]
    xwa = hl[:, 1536:1664]
    xg = hl[:, 1664:1792]
    lane128 = _iota2((1, 128), 1)
    xwa = jnp.where(lane128 < 64, jnp.tanh(xwa), xwa)
    lora = _dot3(xwa, wa_ref[...])
    w_log = -_softplus(-(w0_ref[...] + lora[:, 0:512])) - 0.5
    lw = -jnp.exp(w_log)
    a = _sigmoid(a0_ref[...] + lora[:, 512:1024])
    gate = _dot3(_sigmoid(xg), g2_ref[...])
    kk = k * kk_ref[...]
    k2 = k * (1.0 + (a - 1.0) * ka_ref[...])
    kn = kk * lax.rsqrt(_head_sum(kk * kk) + L2_EPS)
    r_s[...] = r
    kn_s[...] = kn
    k2_s[...] = k2
    v_s[...] = v
    a_s[...] = a
    lw_s[...] = lw
    lc_s[...] = _chunk_cumsum(lw)

    eye64 = _eye(CHUNK)
    eye128 = _eye(128)
    eye128_bf = _eye(128, BF16)
    ii = _iota2((CHUNK, CHUNK), 0)
    jj = _iota2((CHUNK, CHUNK), 1)
    strict = ii > jj
    incl = ii >= jj
    lane = _iota2((CHUNK, 128), 1)
    bi = _iota2((128, 128), 0)
    bj = _iota2((128, 128), 1)
    blockdiag = (bi < 64) == (bj < 64)

    def chunk_body(c, carry):
        rows = pl.ds(pl.multiple_of(c * CHUNK, CHUNK), CHUNK)
        pairs = range(npair)
        sls = [slice(128 * p, 128 * p + 128) for p in pairs]
        lcc = [lc_s[rows, sl] for sl in sls]
        vc = [v_s[rows, sl] for sl in sls]
        lcl = [x[CHUNK - 1:CHUNK, :] for x in lcc]
        ginv = [jnp.exp(-x) for x in lcc]
        gend = [jnp.exp(lcl[p] - lcc[p]) for p in pairs]
        kna = [kn_s[rows, sls[p]] * a_s[rows, sls[p]] for p in pairs]
        at = [-kn_s[rows, sls[p]] * jnp.exp(lcc[p] - lw_s[rows, sls[p]]) for p in pairs]
        bt = [kna[p] * ginv[p] for p in pairs]
        kt = [k2_s[rows, sls[p]] * ginv[p] for p in pairs]
        rt = [r_s[rows, sls[p]] * jnp.exp(lcc[p]) for p in pairs]
        probs = [(p, j) for p in pairs for j in range(2)]
        masks = [(lane < 64) if j == 0 else (lane >= 64) for _, j in probs]
        at_m = [jnp.where(masks[i], at[p], 0.0) for i, (p, _) in enumerate(probs)]
        rt_m = [jnp.where(masks[i], rt[p], 0.0) for i, (p, _) in enumerate(probs)]
        ar = [jnp.concatenate([at_m[i], rt_m[i]], axis=0) for i in range(len(probs))]
        xb = [_dot_nt(ar[i], bt[p]) for i, (p, _) in enumerate(probs)]
        xk = [_dot_nt(ar[i], kt[p]) for i, (p, _) in enumerate(probs)]
        a_ab = [jnp.where(strict, x[0:CHUNK], 0.0) for x in xb]
        a_rb = [jnp.where(incl, x[CHUNK:2 * CHUNK], 0.0) for x in xb]
        a_ak = [jnp.where(strict, x[0:CHUNK], 0.0) for x in xk]
        a_rk = [jnp.where(incl, x[CHUNK:2 * CHUNK], 0.0) for x in xk]
        akv = [_dot(a_ak[i], vc[p]) for i, (p, _) in enumerate(probs)]
        arkv = [_dot(a_rk[i], vc[p]) for i, (p, _) in enumerate(probs)]
        bbt = [_transpose_bf16(kna[p] * gend[p], eye128_bf) for p in pairs]
        kbt = [_transpose_bf16(k2_s[rows, sls[p]] * gend[p], eye128_bf) for p in pairs]
        tinv = _nilpotent_inverse_many(a_ab, eye64, _dot)
        au = [_dot(tinv[i], jnp.concatenate([at_m[i], akv[i]], axis=1)) for i in range(len(probs))]
        ry = [_dot(a_rb[i], au[i]) for i in range(len(probs))]
        kv = [_dot(kbt[p], vc[p]) for p in pairs]
        pq = []
        for p in pairs:
            i0, i1 = 2 * p, 2 * p + 1
            ahat = au[i0][:, 0:128] + au[i1][:, 0:128]
            uv = jnp.where(lane < 64, au[i0][:, 128:256], au[i1][:, 128:256])
            rh_s[c, p] = rt_m[i0] + rt_m[i1] + ry[i0][:, 0:128] + ry[i1][:, 0:128]
            yc_s[c, p] = jnp.where(lane < 64, ry[i0][:, 128:256] + arkv[i0], ry[i1][:, 128:256] + arkv[i1])
            pq.append(_dot(bbt[p], jnp.concatenate([ahat, uv], axis=1)))
        for p in pairs:
            p_s[c, p] = jnp.where(blockdiag, pq[p][:, 0:128], 0.0) + eye128 * jnp.exp(lcl[p])
            qq_s[c, p] = jnp.where(blockdiag, pq[p][:, 128:256] + kv[p], 0.0)
        return carry

    lax.fori_loop(0, nc, chunk_body, 0)

    def scan_body(c, carry):
        rows = pl.ds(pl.multiple_of(c * CHUNK, CHUNK), CHUNK)
        s = [state_s[p] for p in range(npair)]
        s_new = [_dot3(p_s[c, p], s[p]) for p in range(npair)]
        y = [_dot(rh_s[c, p], s[p]) for p in range(npair)]
        for p in range(npair):
            state_s[p] = s_new[p] + qq_s[c, p]
            y_s[rows, 128 * p:128 * p + 128] = y[p] + yc_s[c, p]
        return carry

    lax.fori_loop(0, nc, scan_body, 0)

    y = y_s[...]
    mean = _head_sum(y) * (1.0 / RWKV_HEAD_DIM)
    yc = y - mean
    var = _head_sum(yc * yc) * (1.0 / RWKV_HEAD_DIM)
    yn = yc * lax.rsqrt(var + GN_EPS) * lng_ref[...] + lnb_ref[...]
    bonus = _head_sum(r * k2 * rk_ref[...]) * v
    o_ref[0] = (yn + bonus) * gate


def _dot3_nt(a, b):
    ah, al = _split2(a)
    bh, bl = _split2(b)
    dn = (((1,), (1,)), ((), ()))
    return (lax.dot_general(ah, bh, dn, preferred_element_type=F32)
            + lax.dot_general(ah, bl, dn, preferred_element_type=F32)
            + lax.dot_general(al, bh, dn, preferred_element_type=F32))


def _rwkv(h, mu, w0, w2, a0, a2, g2, k_k, k_a, r_k, ln_g, ln_b, tt=512):
    b, t, _ = h.shape
    nc = tt // CHUNK
    npair = RWKV_WIDTH // 128
    row = lambda p: p.astype(F32).reshape(1, -1)
    wa = jnp.zeros((128, 2 * RWKV_WIDTH), F32)
    wa = wa.at[0:64, 0:RWKV_WIDTH].set(w2.astype(F32)).at[64:128, RWKV_WIDTH:].set(a2.astype(F32))
    small = lambda shape: pl.BlockSpec(shape, lambda i, j: (0, 0))
    vec = small((1, RWKV_WIDTH))
    return pl.pallas_call(
        functools.partial(_rwkv_kernel, tt=tt),
        grid=(b, t // tt),
        in_specs=[pl.BlockSpec((1, tt, RWKV_IN), lambda i, j: (i, j, 0)), small((1, RWKV_IN)),
                  vec, vec, vec, vec, vec, vec, vec,
                  small((128, 2 * RWKV_WIDTH)), small((128, RWKV_WIDTH))],
        out_specs=pl.BlockSpec((1, tt, RWKV_WIDTH), lambda i, j: (i, j, 0)),
        out_shape=jax.ShapeDtypeStruct((b, t, RWKV_WIDTH), F32),
        scratch_shapes=[
            pltpu.VMEM((tt + 8, RWKV_IN), F32),
            pltpu.VMEM((npair, 128, 128), F32),
        ] + [pltpu.VMEM((tt, RWKV_WIDTH), F32)] * 7 + [
            pltpu.VMEM((nc, npair, 128, 128), F32), pltpu.VMEM((nc, npair, 128, 128), F32),
            pltpu.VMEM((nc, npair, CHUNK, 128), F32), pltpu.VMEM((nc, npair, CHUNK, 128), F32),
            pltpu.VMEM((tt, RWKV_WIDTH), F32),
        ],
        compiler_params=pltpu.CompilerParams(
            dimension_semantics=("arbitrary", "arbitrary"), vmem_limit_bytes=VMEM_LIMIT_BYTES),
        name="rwkv7",
    )(h, row(mu), row(w0), row(a0), row(k_k), row(k_a), row(r_k), row(ln_g), row(ln_b),
      wa, g2.astype(F32))


def _sb_kernel(q_ref, k_ref, v_ref, o_ref, acc_s, aux_s, *, bq, bk):
    qi = pl.program_id(2)
    nsub = bq // bk
    q = q_ref[0] * (SB_HEAD_DIM ** -0.5 * LOG2E)
    lane = _iota2((1, 128), 1)
    qm = [jnp.where(lane < 64, q, 0.0).astype(BF16), jnp.where(lane >= 64, q, 0.0).astype(BF16)]
    ti = _iota2((2 * bk, bk), 0)
    tj = _iota2((2 * bk, bk), 1)
    cum_mat = (((ti < bk) & (ti > tj)) | (ti == bk)).astype(BF16)
    acc_s[...] = jnp.zeros_like(acc_s)
    aux_s[...] = jnp.zeros_like(aux_s)

    def load_kv(kb):
        start = pl.multiple_of(kb * bk, bk)
        return (start, k_ref[0, pl.ds(start, bk), :].astype(BF16), v_ref[0, pl.ds(start, bk), :].astype(BF16))

    def stage_scores(item):
        (start, k, v), ra, nr, j, masked = item
        z = lax.dot_general(qm[j][ra:ra + nr], k, (((1,), (1,)), ((), ())), preferred_element_type=F32)
        return z

    def stage_cumsum(item, z):
        (start, k, v), ra, nr, j, masked = item
        rows = slice(ra, ra + nr)
        lsig = jnp.minimum(z, 0.0) - jnp.log(1.0 + jnp.exp2(-jnp.abs(z))) * LOG2E
        l1 = lsig - z
        msk = None
        if masked:
            msk = (start + _iota2((nr, bk), 1)) < (qi * bq + ra + _iota2((nr, bk), 0))
            l1 = jnp.where(msk, l1, 0.0)
        aux = aux_s[j, rows, :]
        l1_hi = l1.astype(BF16)
        aux_hi = aux.astype(BF16)
        hi = jnp.concatenate([l1_hi, aux_hi], axis=1)
        lo = jnp.concatenate([(l1 - l1_hi.astype(F32)).astype(BF16),
                              (aux - aux_hi.astype(F32)).astype(BF16)], axis=1)
        cr = (jnp.dot(hi, cum_mat, preferred_element_type=F32)
              + jnp.dot(lo, cum_mat, preferred_element_type=F32))
        aux_s[j, rows, :] = cr + l1
        return lsig, cr, msk

    def stage_values(item, state):
        (start, k, v), ra, nr, j, masked = item
        lsig, cr, msk = state
        att = jnp.exp2(lsig + cr)
        if masked:
            att = jnp.where(msk, att, 0.0)
        acc_s[j, ra:ra + nr, :] += jnp.dot(att.astype(BF16), v, preferred_element_type=F32)

    def run_items(items):
        n = len(items)
        zs = {}
        states = {}
        for s in range(n + 2):
            if s < n:
                zs[s] = stage_scores(items[s])
            if 0 <= s - 1 < n:
                states[s - 1] = stage_cumsum(items[s - 1], zs.pop(s - 1))
            if 0 <= s - 2 < n:
                stage_values(items[s - 2], states.pop(s - 2))

    def max_carry():
        r = jnp.maximum(aux_s[0], aux_s[1])
        r = jnp.max(jnp.where(lane == 0, r, -jnp.inf), axis=0, keepdims=True)
        return jnp.max(r, axis=1, keepdims=True)[0, 0]

    items = []
    for d in range(nsub - 1, -1, -1):
        kv = load_kv(qi * nsub + d)
        ra = d * bk
        while ra < bq:
            nr = SB_ROW_TILE if (bq - ra) % (2 * SB_ROW_TILE) else 2 * SB_ROW_TILE
            items += [(kv, ra, nr, 0, True), (kv, ra, nr, 1, True)]
            ra += nr
    run_items(items)

    def cond(c):
        kb, rmax = c
        return (kb >= 0) & (rmax > SB_LOG2_CUTOFF)

    def body(c):
        kb, _ = c
        items = []
        for kv in (load_kv(kb), load_kv(kb - 1)):
            for ra in range(0, bq, 2 * SB_ROW_TILE):
                items += [(kv, ra, 2 * SB_ROW_TILE, 0, False), (kv, ra, 2 * SB_ROW_TILE, 1, False)]
        run_items(items)
        return kb - 2, max_carry()

    lax.while_loop(cond, body, (qi * nsub - 1, max_carry()))
    o_ref[0] = jnp.where(lane < 64, acc_s[0], acc_s[1])


def _stick_breaking(h_sb, bq=512, bk=128):
    b, t, _ = h_sb.shape
    npair = SB_WIDTH // 128
    return pl.pallas_call(
        functools.partial(_sb_kernel, bq=bq, bk=bk),
        grid=(b, npair, t // bq),
        in_specs=[
            pl.BlockSpec((1, bq, 128), lambda i, p, j: (i, j, p)),
            pl.BlockSpec((1, t, 128), lambda i, p, j: (i, 0, npair + p)),
            pl.BlockSpec((1, t, 128), lambda i, p, j: (i, 0, 2 * npair + p)),
        ],
        out_specs=pl.BlockSpec((1, bq, 128), lambda i, p, j: (i, j, p)),
        out_shape=jax.ShapeDtypeStruct((b, t, SB_WIDTH), F32),
        scratch_shapes=[pltpu.VMEM((2, bq, 128), F32), pltpu.VMEM((2, bq, bk), F32)],
        compiler_params=pltpu.CompilerParams(
            dimension_semantics=("arbitrary", "arbitrary", "arbitrary"),
            vmem_limit_bytes=VMEM_LIMIT_BYTES),
        name="stickbreak",
    )(h_sb, h_sb, h_sb)


def _merge_kernel(x_ref, oa_ref, ob_ref, oc_ref, gt_ref, wa_ref, wb_ref, wc_ref, wo_ref, o_ref):
    d = D_MODEL
    g = _sigmoid(gt_ref[...])
    m = (g[:, 0:d] * _dot(oa_ref[...], wa_ref[...])
         + g[:, d:2 * d] * _dot(ob_ref[...], wb_ref[...])
         + g[:, 2 * d:3 * d] * _dot(oc_ref[...], wc_ref[...]))
    o_ref[...] = x_ref[...] + _dot(m, wo_ref[...])


def _merge(x2d, oa, ob, oc, gates, wa, wb, wc, wo, tm=512):
    n, d = x2d.shape
    rowspec = lambda w: pl.BlockSpec((tm, w), lambda i: (i, 0))
    full = lambda w: pl.BlockSpec(w.shape, lambda i: (0, 0))
    return pl.pallas_call(
        _merge_kernel,
        grid=(n // tm,),
        in_specs=[rowspec(d), rowspec(512), rowspec(512), rowspec(512), rowspec(3 * d),
                  full(wa), full(wb), full(wc), full(wo)],
        out_specs=rowspec(d),
        out_shape=jax.ShapeDtypeStruct((n, d), F32),
        compiler_params=pltpu.CompilerParams(
            dimension_semantics=("arbitrary",), vmem_limit_bytes=VMEM_LIMIT_BYTES),
        name="merge",
    )(x2d, oa, ob, oc, gates, wa, wb, wc, wo)


def _router_kernel(x_ref, g_ref, rw_ref, o_ref):
    hn = _rms(x_ref[...], g_ref[...])
    logits = _dot3(hn, rw_ref[...])
    lane = _iota2(logits.shape, 1)
    m1 = jnp.max(logits, axis=-1, keepdims=True)
    i1 = jnp.min(jnp.where(logits == m1, lane, N_EXPERTS), axis=-1, keepdims=True)
    sel1 = lane == i1
    rest = jnp.where(sel1, -jnp.inf, logits)
    m2 = jnp.max(rest, axis=-1, keepdims=True)
    i2 = jnp.min(jnp.where(rest == m2, lane, N_EXPERTS), axis=-1, keepdims=True)
    sel2 = lane == i2
    e2 = jnp.exp(m2 - m1)
    w1 = 1.0 / (1.0 + e2)
    o_ref[...] = jnp.where(sel1, w1, 0.0) + jnp.where(sel2, e2 * w1, 0.0)


def _router(x2d, g, rw, tm=512):
    n, d = x2d.shape
    return pl.pallas_call(
        _router_kernel,
        grid=(n // tm,),
        in_specs=[pl.BlockSpec((tm, d), lambda i: (i, 0)), pl.BlockSpec((1, d), lambda i: (0, 0)),
                  pl.BlockSpec(rw.shape, lambda i: (0, 0))],
        out_specs=pl.BlockSpec((tm, N_EXPERTS), lambda i: (i, 0)),
        out_shape=jax.ShapeDtypeStruct((n, N_EXPERTS), F32),
        compiler_params=pltpu.CompilerParams(dimension_semantics=("arbitrary",)),
        name="router",
    )(x2d, g.reshape(1, d), rw.astype(F32))


def _ffn_kernel(*refs, moe, final):
    if moe:
        x_ref, g_ref, comb_ref, wg_ref, wu_ref, wd_ref = refs[:6]
        rest = refs[6:]
    else:
        x_ref, g_ref, wg_ref, wu_ref, wd_ref = refs[:5]
        rest = refs[5:]
    if final:
        fg_ref, o_ref, hn_s, acc_s = rest
    else:
        o_ref, hn_s, acc_s = rest
    if moe:
        e = pl.program_id(1)
        f = pl.program_id(2)
        first = (e == 0) & (f == 0)
        last = (e == pl.num_programs(1) - 1) & (f == pl.num_programs(2) - 1)
    else:
        f = pl.program_id(1)
        first = f == 0
        last = f == pl.num_programs(1) - 1

    @pl.when(first)
    def _():
        hn_s[...] = _rms(x_ref[...], g_ref[...]).astype(BF16)
        acc_s[...] = jnp.zeros_like(acc_s)

    hn = hn_s[...]
    act = _silu(jnp.dot(hn, wg_ref[...], preferred_element_type=F32)) * jnp.dot(
        hn, wu_ref[...], preferred_element_type=F32)
    if moe:
        comb = comb_ref[...]
        lane = _iota2(comb.shape, 1)
        act = act * jnp.sum(jnp.where(lane == e, comb, 0.0), axis=-1, keepdims=True)
    acc_s[...] += jnp.dot(act.astype(BF16), wd_ref[...], preferred_element_type=F32)

    @pl.when(last)
    def _():
        out = x_ref[...] + acc_s[...]
        if final:
            out = _rms(out, fg_ref[...])
        o_ref[...] = out


def _ffn(x2d, g, wg, wu, wd, comb=None, final_g=None, tm=1024, tf=256):
    n, d = x2d.shape
    moe = comb is not None
    final = final_g is not None
    if moe:
        ne, _, dff = wg.shape
        grid = (n // tm, ne, dff // tf)
        xmap = lambda i, e, f: (i, 0)
        cmap = lambda i, e, f: (0, 0)
        in_specs = [pl.BlockSpec((tm, d), xmap), pl.BlockSpec((1, d), cmap),
                    pl.BlockSpec((tm, N_EXPERTS), xmap),
                    pl.BlockSpec((None, d, tf), lambda i, e, f: (e, 0, f)),
                    pl.BlockSpec((None, d, tf), lambda i, e, f: (e, 0, f)),
                    pl.BlockSpec((None, tf, d), lambda i, e, f: (e, f, 0))]
        args = [x2d, g.reshape(1, d), comb, wg, wu, wd]
        sem = ("arbitrary", "arbitrary", "arbitrary")
    else:
        dff = wg.shape[1]
        grid = (n // tm, dff // tf)
        xmap = lambda i, f: (i, 0)
        cmap = lambda i, f: (0, 0)
        in_specs = [pl.BlockSpec((tm, d), xmap), pl.BlockSpec((1, d), cmap),
                    pl.BlockSpec((d, tf), lambda i, f: (0, f)),
                    pl.BlockSpec((d, tf), lambda i, f: (0, f)),
                    pl.BlockSpec((tf, d), lambda i, f: (f, 0))]
        args = [x2d, g.reshape(1, d), wg, wu, wd]
        sem = ("arbitrary", "arbitrary")
    if final:
        in_specs.append(pl.BlockSpec((1, d), cmap))
        args.append(final_g.reshape(1, d))
    return pl.pallas_call(
        functools.partial(_ffn_kernel, moe=moe, final=final),
        grid=grid,
        in_specs=in_specs,
        out_specs=pl.BlockSpec((tm, d), xmap),
        out_shape=jax.ShapeDtypeStruct((n, d), F32),
        scratch_shapes=[pltpu.VMEM((tm, d), BF16), pltpu.VMEM((tm, d), F32)],
        compiler_params=pltpu.CompilerParams(dimension_semantics=sem, vmem_limit_bytes=VMEM_LIMIT_BYTES),
        name="moe_ffn" if moe else "dense_ffn",
    )(*args)


def kernel(x, norm_mix_g, w_in, gdn_conv_w, gdn_a_log, gdn_dt_bias, gdn_norm_g, rwkv_mu, rwkv_w0, rwkv_w2, rwkv_a0, rwkv_a2, rwkv_g2, rwkv_k_k, rwkv_k_a, rwkv_r_k, rwkv_ln_g, rwkv_ln_b, w_branch_gdn, w_branch_rwkv, w_branch_sb, w_out, norm_ffn_g, ffn_w_gate, ffn_w_up, ffn_w_down, router_w, moe_w_gate, moe_w_up, moe_w_down, final_norm_g):
    b, t, d = x.shape
    n = b * t
    depth = w_in.shape[0]
    x2 = x.reshape(n, d).astype(F32)
    for layer in range(depth):
        w = w_in[layer]
        w_gdn = jnp.concatenate(
            [w[:, 0:2048], jnp.repeat(w[:, 2048:2052], 128, axis=1), jnp.repeat(w[:, 2052:2056], 128, axis=1)],
            axis=1).astype(BF16)
        w_rwkv = w[:, 2056:3848].astype(BF16)
        w_sb = w[:, 3848:5384].astype(BF16)
        w_gates = w[:, 5384:8456].astype(BF16)
        g_mix = norm_mix_g[layer].astype(F32)
        p_gdn, p_rwkv = _normproj(x2, g_mix, [w_gdn, w_rwkv])
        p_sb, p_gates = _normproj(x2, g_mix, [w_sb, w_gates])
        o_a = _gdn(p_gdn.reshape(b, t, -1), gdn_conv_w[layer], gdn_a_log[layer], gdn_dt_bias[layer],
                   gdn_norm_g[layer])
        o_b = _rwkv(p_rwkv.reshape(b, t, -1), rwkv_mu[layer], rwkv_w0[layer], rwkv_w2[layer],
                    rwkv_a0[layer], rwkv_a2[layer], rwkv_g2[layer], rwkv_k_k[layer], rwkv_k_a[layer],
                    rwkv_r_k[layer].reshape(-1), rwkv_ln_g[layer], rwkv_ln_b[layer])
        o_c = _stick_breaking(p_sb.reshape(b, t, -1))
        x2 = _merge(x2, o_a.reshape(n, -1), o_b.reshape(n, -1), o_c.reshape(n, -1), p_gates,
                    w_branch_gdn[layer].astype(BF16), w_branch_rwkv[layer].astype(BF16),
                    w_branch_sb[layer].astype(BF16), w_out[layer].astype(BF16))
        g_ffn = norm_ffn_g[layer].astype(F32)
        final_g = final_norm_g.astype(F32) if layer == depth - 1 else None
        i = layer // 2
        if layer % 2 == 0:
            x2 = _ffn(x2, g_ffn, ffn_w_gate[i].astype(BF16), ffn_w_up[i].astype(BF16),
                      ffn_w_down[i].astype(BF16), final_g=final_g, tf=256)
        else:
            comb = _router(x2, g_ffn, router_w[i])
            x2 = _ffn(x2, g_ffn, moe_w_gate[i].astype(BF16), moe_w_up[i].astype(BF16),
                      moe_w_down[i].astype(BF16), comb=comb, final_g=final_g, tf=512)
    return x2.reshape(b, t, d)
```

```python
import functools

import jax
import jax.numpy as jnp
from jax import lax
from jax.experimental import pallas as pl
from jax.experimental.pallas import tpu as pltpu

F32 = jnp.float32
BF16 = jnp.bfloat16

RMS_EPS = 1e-6
GN_EPS = 64e-5
L2_EPS = 1e-6

D_MODEL = 1024
GDN_HEADS = 4
GDN_HEAD_DIM = 128
GDN_WIDTH = 512
RWKV_WIDTH = 512
RWKV_HEAD_DIM = 64
RWKV_IN = 1792
SB_WIDTH = 512
SB_HEAD_DIM = 64
N_EXPERTS = 8
CHUNK = 64
LOG2E = 1.4426950408889634
SB_LOG2_CUTOFF = -160.0
SB_ROW_TILE = 128
GDN_CHUNKS_PER_STEP = 2
MOE_ROW_TILE = 512
VMEM_LIMIT_BYTES = 56 * 1024 * 1024


def _dot(a, b):
    return jnp.dot(a.astype(BF16), b.astype(BF16), preferred_element_type=F32)


def _dot_nt(a, b):
    return lax.dot_general(a.astype(BF16), b.astype(BF16), (((1,), (1,)), ((), ())),
                           preferred_element_type=F32)


def _split2(a):
    hi = a.astype(BF16)
    lo = (a - hi.astype(F32)).astype(BF16)
    return hi, lo


def _split3(a):
    hi = a.astype(BF16)
    r = a - hi.astype(F32)
    mid = r.astype(BF16)
    lo = (r - mid.astype(F32)).astype(BF16)
    return hi, mid, lo


def _dot3(a, b):
    ah, al = _split2(a)
    bh, bl = _split2(b)
    return (jnp.dot(ah, bh, preferred_element_type=F32)
            + jnp.dot(ah, bl, preferred_element_type=F32)
            + jnp.dot(al, bh, preferred_element_type=F32))


def _dot_exact_lhs(m, x, parts=3):
    xs = _split3(x) if parts == 3 else _split2(x)
    out = jnp.dot(m, xs[0], preferred_element_type=F32)
    for p in xs[1:]:
        out = out + jnp.dot(m, p, preferred_element_type=F32)
    return out


def _dot_exact_rhs(x, m, parts=2):
    xs = _split3(x) if parts == 3 else _split2(x)
    out = jnp.dot(xs[0], m, preferred_element_type=F32)
    for p in xs[1:]:
        out = out + jnp.dot(p, m, preferred_element_type=F32)
    return out


def _transpose_bf16(x, eye):
    return lax.dot_general(eye, x.astype(BF16), (((1,), (1,)), ((), ())), preferred_element_type=F32)


def _transpose_f32(x, eye):
    hi, mid, lo = _split3(x)
    dn = (((1,), (1,)), ((), ()))
    return (lax.dot_general(eye, hi, dn, preferred_element_type=F32)
            + lax.dot_general(eye, mid, dn, preferred_element_type=F32)
            + lax.dot_general(eye, lo, dn, preferred_element_type=F32))


def _iota2(shape, dim):
    return lax.broadcasted_iota(jnp.int32, shape, dim)


def _eye(n, dtype=F32):
    return (_iota2((n, n), 0) == _iota2((n, n), 1)).astype(dtype)


def _softplus(x):
    return jnp.maximum(x, 0.0) + jnp.log(1.0 + jnp.exp(-jnp.abs(x)))


def _sigmoid(x):
    return 1.0 / (1.0 + jnp.exp(-x))


def _silu(x):
    return x * _sigmoid(x)


def _rms(x, g):
    return x * lax.rsqrt(jnp.mean(x * x, axis=-1, keepdims=True) + RMS_EPS) * g


def _nilpotent_inverse(n, eye, dot):
    t = eye + n
    x = n
    for _ in range(5):
        x = dot(x, x)
        t = t + dot(t, x)
    return t


def _nilpotent_inverse_many(ns, eye, dot):
    ts = [eye + n for n in ns]
    xs = list(ns)
    for _ in range(5):
        xs = [dot(x, x) for x in xs]
        ts = [t + dot(t, x) for t, x in zip(ts, xs)]
    return ts


def _chunk_cumsum(x):
    i = _iota2((128, 128), 0)
    j = _iota2((128, 128), 1)
    m = ((j <= i) & ((i >> 6) == (j >> 6))).astype(BF16)
    return jnp.concatenate(
        [_dot_exact_lhs(m, x[r:r + 128]) for r in range(0, x.shape[0], 128)], axis=0)


def _head_sum(x):
    i = _iota2((128, 128), 0)
    j = _iota2((128, 128), 1)
    m = ((i >> 6) == (j >> 6)).astype(BF16)
    return jnp.concatenate(
        [_dot_exact_rhs(x[:, c:c + 128], m) for c in range(0, x.shape[1], 128)], axis=1)


def _normproj_kernel(x_ref, g_ref, *refs, n_out):
    w_refs = refs[:n_out]
    o_refs = refs[n_out:]
    hn = _rms(x_ref[...], g_ref[...]).astype(BF16)
    for w_ref, o_ref in zip(w_refs, o_refs):
        o_ref[...] = jnp.dot(hn, w_ref[...], preferred_element_type=F32)


def _normproj(x2d, g, weights, tm=256):
    n, d = x2d.shape
    n_out = len(weights)
    in_specs = [pl.BlockSpec((tm, d), lambda i: (i, 0)), pl.BlockSpec((1, d), lambda i: (0, 0))]
    in_specs += [pl.BlockSpec(w.shape, lambda i: (0, 0)) for w in weights]
    out_specs = [pl.BlockSpec((tm, w.shape[1]), lambda i: (i, 0)) for w in weights]
    out_shape = [jax.ShapeDtypeStruct((n, w.shape[1]), F32) for w in weights]
    return pl.pallas_call(
        functools.partial(_normproj_kernel, n_out=n_out),
        grid=(n // tm,),
        in_specs=in_specs,
        out_specs=out_specs,
        out_shape=out_shape,
        compiler_params=pltpu.CompilerParams(
            dimension_semantics=("arbitrary",), vmem_limit_bytes=VMEM_LIMIT_BYTES),
        name="normproj",
    )(x2d, g.reshape(1, d), *weights)


def _gdn_kernel(qkv_ref, z_ref, b_ref, a_ref, cw_ref, alog_ref, dtb_ref, ng_ref, o_ref,
                ext_s, state_s, q_s, k_s, v_s, beta_s, gc_s, p_s, qq_s, r_s, zz_s, oc_s, *, tt):
    nc = tt // CHUNK
    w3 = 3 * GDN_WIDTH
    t = pl.program_id(1)

    @pl.when(t == 0)
    def _():
        ext_s[0:8, :] = jnp.zeros((8, w3), F32)
        state_s[...] = jnp.zeros_like(state_s)

    raw = qkv_ref[0]
    ext_s[8:8 + tt, :] = raw
    cw = cw_ref[...]
    y = raw * cw[3:4, :]
    for i in range(3):
        y = y + ext_s[5 + i:5 + i + tt, :] * cw[i:i + 1, :]
    ext_s[0:8, :] = raw[tt - 8:tt, :]
    y = _silu(y)

    for h in range(GDN_HEADS):
        sl = slice(128 * h, 128 * h + 128)
        qh = y[:, 128 * h:128 * h + 128]
        kh = y[:, GDN_WIDTH + 128 * h:GDN_WIDTH + 128 * h + 128]
        vh = y[:, 2 * GDN_WIDTH + 128 * h:2 * GDN_WIDTH + 128 * h + 128]
        qh = qh * lax.rsqrt(jnp.sum(qh * qh, axis=-1, keepdims=True) + L2_EPS) * (GDN_HEAD_DIM ** -0.5)
        kh = kh * lax.rsqrt(jnp.sum(kh * kh, axis=-1, keepdims=True) + L2_EPS)
        q_s[:, sl] = qh
        k_s[:, sl] = kh
        v_s[:, sl] = vh

    beta_s[...] = _sigmoid(b_ref[0])
    g = -jnp.exp(alog_ref[...]) * _softplus(a_ref[0] + dtb_ref[...])
    gc_s[...] = _chunk_cumsum(g)

    eye64 = _eye(CHUNK)
    eye128 = _eye(128)
    eye128_bf = _eye(128, BF16)
    ii = _iota2((CHUNK, CHUNK), 0)
    jj = _iota2((CHUNK, CHUNK), 1)

    def chunk_body(ci, carry):
        probs = [(ci * GDN_CHUNKS_PER_STEP + u, h) for u in range(GDN_CHUNKS_PER_STEP)
                 for h in range(GDN_HEADS)]
        rows = [pl.ds(pl.multiple_of(c * CHUNK, CHUNK), CHUNK) for c, _ in probs]
        lanes = [slice(128 * h, 128 * h + 128) for _, h in probs]
        idx = range(len(probs))
        q = [q_s[rows[i], lanes[i]] for i in idx]
        k = [k_s[rows[i], lanes[i]] for i in idx]
        gcc = [gc_s[rows[i], lanes[i]] for i in idx]
        gl = [g[CHUNK - 1:CHUNK, :] for g in gcc]
        gcr = [_transpose_f32(g, eye128_bf)[0:CHUNK, :] for g in gcc]
        dec_incl = [jnp.exp(jnp.where(ii >= jj, gcc[i][:, 0:CHUNK] - gcr[i], -jnp.inf)) for i in idx]
        kb = [k[i] * beta_s[rows[i], lanes[i]] for i in idx]
        a_mat = [_dot_nt(kb[i], k[i]) for i in idx]
        attn = [_dot_nt(q[i], k[i]) * dec_incl[i] for i in idx]
        kdt = [_transpose_bf16(k[i] * jnp.exp(gl[i] - gcc[i]), eye128_bf) for i in idx]
        tinv = _nilpotent_inverse_many(
            [-a_mat[i] * jnp.where(ii > jj, dec_incl[i], 0.0) for i in idx], eye64, _dot)
        u = [_dot(tinv[i], v_s[rows[i], lanes[i]] * beta_s[rows[i], lanes[i]]) for i in idx]
        w = [_dot(tinv[i], kb[i] * jnp.exp(gcc[i])) for i in idx]
        for i, (c, h) in enumerate(probs):
            p_s[c, h] = jnp.exp(gl[i]) * eye128 - _dot(kdt[i], w[i])
        for i, (c, h) in enumerate(probs):
            qq_s[c, h] = _dot(kdt[i], u[i])
        for i, (c, h) in enumerate(probs):
            r_s[c, h] = q[i] * jnp.exp(gcc[i]) - _dot(attn[i], w[i])
        for i, (c, h) in enumerate(probs):
            zz_s[c, h] = _dot(attn[i], u[i])
        return carry

    lax.fori_loop(0, nc // GDN_CHUNKS_PER_STEP, chunk_body, 0)

    def scan_body(c, carry):
        rows = pl.ds(pl.multiple_of(c * CHUNK, CHUNK), CHUNK)
        s = [state_s[h] for h in range(GDN_HEADS)]
        s_new = [_dot3(p_s[c, h], s[h]) for h in range(GDN_HEADS)]
        o = [_dot(r_s[c, h], s[h]) for h in range(GDN_HEADS)]
        for h in range(GDN_HEADS):
            state_s[h] = s_new[h] + qq_s[c, h]
            oc_s[rows, 128 * h:128 * h + 128] = o[h] + zz_s[c, h]
        return carry

    lax.fori_loop(0, nc, scan_body, 0)

    o = oc_s[...]
    z = z_ref[0]
    ng = ng_ref[...]
    for h in range(GDN_HEADS):
        sl = slice(128 * h, 128 * h + 128)
        oh = o[:, sl]
        oh = oh * lax.rsqrt(jnp.mean(oh * oh, axis=-1, keepdims=True) + RMS_EPS) * ng[:, sl]
        o_ref[0, :, sl] = oh * _silu(z[:, sl])


def _gdn(proj_a, conv_w, a_log, dt_bias, norm_g, tt=512):
    b, t, _ = proj_a.shape
    nc = tt // CHUNK
    w3 = 3 * GDN_WIDTH
    rep = lambda p: jnp.repeat(p.astype(F32), 128).reshape(1, GDN_WIDTH)
    small = lambda shape: pl.BlockSpec(shape, lambda i, j: (0, 0))
    return pl.pallas_call(
        functools.partial(_gdn_kernel, tt=tt),
        grid=(b, t // tt),
        in_specs=[
            pl.BlockSpec((1, tt, w3), lambda i, j: (i, j, 0)),
            pl.BlockSpec((1, tt, GDN_WIDTH), lambda i, j: (i, j, 3)),
            pl.BlockSpec((1, tt, GDN_WIDTH), lambda i, j: (i, j, 4)),
            pl.BlockSpec((1, tt, GDN_WIDTH), lambda i, j: (i, j, 5)),
            small((4, w3)), small((1, GDN_WIDTH)), small((1, GDN_WIDTH)), small((1, GDN_WIDTH)),
        ],
        out_specs=pl.BlockSpec((1, tt, GDN_WIDTH), lambda i, j: (i, j, 0)),
        out_shape=jax.ShapeDtypeStruct((b, t, GDN_WIDTH), F32),
        scratch_shapes=[
            pltpu.VMEM((tt + 8, w3), F32),
            pltpu.VMEM((GDN_HEADS, 128, 128), F32),
            pltpu.VMEM((tt, GDN_WIDTH), F32), pltpu.VMEM((tt, GDN_WIDTH), F32),
            pltpu.VMEM((tt, GDN_WIDTH), F32), pltpu.VMEM((tt, GDN_WIDTH), F32),
            pltpu.VMEM((tt, GDN_WIDTH), F32),
            pltpu.VMEM((nc, GDN_HEADS, 128, 128), F32), pltpu.VMEM((nc, GDN_HEADS, 128, 128), F32),
            pltpu.VMEM((nc, GDN_HEADS, CHUNK, 128), F32), pltpu.VMEM((nc, GDN_HEADS, CHUNK, 128), F32),
            pltpu.VMEM((tt, GDN_WIDTH), F32),
        ],
        compiler_params=pltpu.CompilerParams(
            dimension_semantics=("arbitrary", "arbitrary"), vmem_limit_bytes=VMEM_LIMIT_BYTES),
        name="gdn",
    )(proj_a, proj_a, proj_a, proj_a, conv_w.astype(F32), rep(a_log), rep(dt_bias),
      jnp.tile(norm_g.astype(F32), GDN_HEADS).reshape(1, GDN_WIDTH))


def _rwkv_kernel(h_ref, mu_ref, w0_ref, a0_ref, kk_ref, ka_ref, rk_ref, lng_ref, lnb_ref,
                 wa_ref, g2_ref, o_ref,
                 ext_s, state_s, r_s, kn_s, k2_s, v_s, a_s, lw_s, lc_s, p_s, qq_s, rh_s, yc_s, y_s,
                 *, tt):
    nc = tt // CHUNK
    npair = RWKV_WIDTH // 128
    t = pl.program_id(1)

    @pl.when(t == 0)
    def _():
        ext_s[0:8, :] = jnp.zeros((8, RWKV_IN), F32)
        state_s[...] = jnp.zeros_like(state_s)

    raw = h_ref[0]
    ext_s[8:8 + tt, :] = raw
    prev = ext_s[7:7 + tt, :]
    ext_s[0:8, :] = raw[tt - 8:tt, :]
    hl = raw + (prev - raw) * mu_ref[...]
    r = hl[:, 0:512]
    k = hl[:, 512:1024]
    v = hl[:, 1024:1536]
    xwa = hl[:, 1536:1664]
    xg = hl[:, 1664:1792]
    lane128 = _iota2((1, 128), 1)
    xwa = jnp.where(lane128 < 64, jnp.tanh(xwa), xwa)
    lora = _dot3(xwa, wa_ref[...])
    w_log = -_softplus(-(w0_ref[...] + lora[:, 0:512])) - 0.5
    lw = -jnp.exp(w_log)
    a = _sigmoid(a0_ref[...] + lora[:, 512:1024])
    gate = _dot3(_sigmoid(xg), g2_ref[...])
    kk = k * kk_ref[...]
    k2 = k * (1.0 + (a - 1.0) * ka_ref[...])
    kn = kk * lax.rsqrt(_head_sum(kk * kk) + L2_EPS)
    r_s[...] = r
    kn_s[...] = kn
    k2_s[...] = k2
    v_s[...] = v
    a_s[...] = a
    lw_s[...] = lw
    lc_s[...] = _chunk_cumsum(lw)

    eye64 = _eye(CHUNK)
    eye128 = _eye(128)
    eye128_bf = _eye(128, BF16)
    ii = _iota2((CHUNK, CHUNK), 0)
    jj = _iota2((CHUNK, CHUNK), 1)
    strict = ii > jj
    incl = ii >= jj
    lane = _iota2((CHUNK, 128), 1)
    bi = _iota2((128, 128), 0)
    bj = _iota2((128, 128), 1)
    blockdiag = (bi < 64) == (bj < 64)

    def chunk_body(c, carry):
        rows = pl.ds(pl.multiple_of(c * CHUNK, CHUNK), CHUNK)
        pairs = range(npair)
        sls = [slice(128 * p, 128 * p + 128) for p in pairs]
        lcc = [lc_s[rows, sl] for sl in sls]
        vc = [v_s[rows, sl] for sl in sls]
        lcl = [x[CHUNK - 1:CHUNK, :] for x in lcc]
        ginv = [jnp.exp(-x) for x in lcc]
        gend = [jnp.exp(lcl[p] - lcc[p]) for p in pairs]
        kna = [kn_s[rows, sls[p]] * a_s[rows, sls[p]] for p in pairs]
        at = [-kn_s[rows, sls[p]] * jnp.exp(lcc[p] - lw_s[rows, sls[p]]) for p in pairs]
        bt = [kna[p] * ginv[p] for p in pairs]
        kt = [k2_s[rows, sls[p]] * ginv[p] for p in pairs]
        rt = [r_s[rows, sls[p]] * jnp.exp(lcc[p]) for p in pairs]
        probs = [(p, j) for p in pairs for j in range(2)]
        masks = [(lane < 64) if j == 0 else (lane >= 64) for _, j in probs]
        at_m = [jnp.where(masks[i], at[p], 0.0) for i, (p, _) in enumerate(probs)]
        rt_m = [jnp.where(masks[i], rt[p], 0.0) for i, (p, _) in enumerate(probs)]
        ar = [jnp.concatenate([at_m[i], rt_m[i]], axis=0) for i in range(len(probs))]
        xb = [_dot_nt(ar[i], bt[p]) for i, (p, _) in enumerate(probs)]
        xk = [_dot_nt(ar[i], kt[p]) for i, (p, _) in enumerate(probs)]
        a_ab = [jnp.where(strict, x[0:CHUNK], 0.0) for x in xb]
        a_rb = [jnp.where(incl, x[CHUNK:2 * CHUNK], 0.0) for x in xb]
        a_ak = [jnp.where(strict, x[0:CHUNK], 0.0) for x in xk]
        a_rk = [jnp.where(incl, x[CHUNK:2 * CHUNK], 0.0) for x in xk]
        akv = [_dot(a_ak[i], vc[p]) for i, (p, _) in enumerate(probs)]
        arkv = [_dot(a_rk[i], vc[p]) for i, (p, _) in enumerate(probs)]
        bbt = [_transpose_bf16(kna[p] * gend[p], eye128_bf) for p in pairs]
        kbt = [_transpose_bf16(k2_s[rows, sls[p]] * gend[p], eye128_bf) for p in pairs]
        tinv = _nilpotent_inverse_many(a_ab, eye64, _dot)
        au = [_dot(tinv[i], jnp.concatenate([at_m[i], akv[i]], axis=1)) for i in range(len(probs))]
        ry = [_dot(a_rb[i], au[i]) for i in range(len(probs))]
        kv = [_dot(kbt[p], vc[p]) for p in pairs]
        pq = []
        for p in pairs:
            i0, i1 = 2 * p, 2 * p + 1
            ahat = au[i0][:, 0:128] + au[i1][:, 0:128]
            uv = jnp.where(lane < 64, au[i0][:, 128:256], au[i1][:, 128:256])
            rh_s[c, p] = rt_m[i0] + rt_m[i1] + ry[i0][:, 0:128] + ry[i1][:, 0:128]
            yc_s[c, p] = jnp.where(lane < 64, ry[i0][:, 128:256] + arkv[i0], ry[i1][:, 128:256] + arkv[i1])
            pq.append(_dot(bbt[p], jnp.concatenate([ahat, uv], axis=1)))
        for p in pairs:
            p_s[c, p] = jnp.where(blockdiag, pq[p][:, 0:128], 0.0) + eye128 * jnp.exp(lcl[p])
            qq_s[c, p] = jnp.where(blockdiag, pq[p][:, 128:256] + kv[p], 0.0)
        return carry

    lax.fori_loop(0, nc, chunk_body, 0)

    def scan_body(c, carry):
        rows = pl.ds(pl.multiple_of(c * CHUNK, CHUNK), CHUNK)
        s = [state_s[p] for p in range(npair)]
        s_new = [_dot3(p_s[c, p], s[p]) for p in range(npair)]
        y = [_dot(rh_s[c, p], s[p]) for p in range(npair)]
        for p in range(npair):
            state_s[p] = s_new[p] + qq_s[c, p]
            y_s[rows, 128 * p:128 * p + 128] = y[p] + yc_s[c, p]
        return carry

    lax.fori_loop(0, nc, scan_body, 0)

    y = y_s[...]
    mean = _head_sum(y) * (1.0 / RWKV_HEAD_DIM)
    yc = y - mean
    var = _head_sum(yc * yc) * (1.0 / RWKV_HEAD_DIM)
    yn = yc * lax.rsqrt(var + GN_EPS) * lng_ref[...] + lnb_ref[...]
    bonus = _head_sum(r * k2 * rk_ref[...]) * v
    o_ref[0] = (yn + bonus) * gate


def _dot3_nt(a, b):
    ah, al = _split2(a)
    bh, bl = _split2(b)
    dn = (((1,), (1,)), ((), ()))
    return (lax.dot_general(ah, bh, dn, preferred_element_type=F32)
            + lax.dot_general(ah, bl, dn, preferred_element_type=F32)
            + lax.dot_general(al, bh, dn, preferred_element_type=F32))


def _rwkv(h, mu, w0, w2, a0, a2, g2, k_k, k_a, r_k, ln_g, ln_b, tt=512):
    b, t, _ = h.shape
    nc = tt // CHUNK
    npair = RWKV_WIDTH // 128
    row = lambda p: p.astype(F32).reshape(1, -1)
    wa = jnp.zeros((128, 2 * RWKV_WIDTH), F32)
    wa = wa.at[0:64, 0:RWKV_WIDTH].set(w2.astype(F32)).at[64:128, RWKV_WIDTH:].set(a2.astype(F32))
    small = lambda shape: pl.BlockSpec(shape, lambda i, j: (0, 0))
    vec = small((1, RWKV_WIDTH))
    return pl.pallas_call(
        functools.partial(_rwkv_kernel, tt=tt),
        grid=(b, t // tt),
        in_specs=[pl.BlockSpec((1, tt, RWKV_IN), lambda i, j: (i, j, 0)), small((1, RWKV_IN)),
                  vec, vec, vec, vec, vec, vec, vec,
                  small((128, 2 * RWKV_WIDTH)), small((128, RWKV_WIDTH))],
        out_specs=pl.BlockSpec((1, tt, RWKV_WIDTH), lambda i, j: (i, j, 0)),
        out_shape=jax.ShapeDtypeStruct((b, t, RWKV_WIDTH), F32),
        scratch_shapes=[
            pltpu.VMEM((tt + 8, RWKV_IN), F32),
            pltpu.VMEM((npair, 128, 128), F32),
        ] + [pltpu.VMEM((tt, RWKV_WIDTH), F32)] * 7 + [
            pltpu.VMEM((nc, npair, 128, 128), F32), pltpu.VMEM((nc, npair, 128, 128), F32),
            pltpu.VMEM((nc, npair, CHUNK, 128), F32), pltpu.VMEM((nc, npair, CHUNK, 128), F32),
            pltpu.VMEM((tt, RWKV_WIDTH), F32),
        ],
        compiler_params=pltpu.CompilerParams(
            dimension_semantics=("arbitrary", "arbitrary"), vmem_limit_bytes=VMEM_LIMIT_BYTES),
        name="rwkv7",
    )(h, row(mu), row(w0), row(a0), row(k_k), row(k_a), row(r_k), row(ln_g), row(ln_b),
      wa, g2.astype(F32))


def _sb_kernel(q_ref, k_ref, v_ref, o_ref, acc_s, aux_s, *, bq, bk):
    qi = pl.program_id(2)
    nsub = bq // bk
    q = q_ref[0] * (SB_HEAD_DIM ** -0.5 * LOG2E)
    lane = _iota2((1, 128), 1)
    qm = [jnp.where(lane < 64, q, 0.0).astype(BF16), jnp.where(lane >= 64, q, 0.0).astype(BF16)]
    ti = _iota2((2 * bk, bk), 0)
    tj = _iota2((2 * bk, bk), 1)
    cum_mat = (((ti < bk) & (ti > tj)) | (ti == bk)).astype(BF16)
    acc_s[...] = jnp.zeros_like(acc_s)
    aux_s[...] = jnp.zeros_like(aux_s)

    def load_kv(kb):
        start = pl.multiple_of(kb * bk, bk)
        return (start, k_ref[0, pl.ds(start, bk), :].astype(BF16), v_ref[0, pl.ds(start, bk), :].astype(BF16))

    def stage_scores(item):
        (start, k, v), ra, nr, j, masked = item
        z = lax.dot_general(qm[j][ra:ra + nr], k, (((1,), (1,)), ((), ())), preferred_element_type=F32)
        return z

    def stage_cumsum(item, z):
        (start, k, v), ra, nr, j, masked = item
        rows = slice(ra, ra + nr)
        lsig = jnp.minimum(z, 0.0) - jnp.log(1.0 + jnp.exp2(-jnp.abs(z))) * LOG2E
        l1 = lsig - z
        msk = None
        if masked:
            msk = (start + _iota2((nr, bk), 1)) < (qi * bq + ra + _iota2((nr, bk), 0))
            l1 = jnp.where(msk, l1, 0.0)
        aux = aux_s[j, rows, :]
        l1_hi = l1.astype(BF16)
        aux_hi = aux.astype(BF16)
        hi = jnp.concatenate([l1_hi, aux_hi], axis=1)
        lo = jnp.concatenate([(l1 - l1_hi.astype(F32)).astype(BF16),
                              (aux - aux_hi.astype(F32)).astype(BF16)], axis=1)
        cr = (jnp.dot(hi, cum_mat, preferred_element_type=F32)
              + jnp.dot(lo, cum_mat, preferred_element_type=F32))
        aux_s[j, rows, :] = cr + l1
        return lsig, cr, msk

    def stage_values(item, state):
        (start, k, v), ra, nr, j, masked = item
        lsig, cr, msk = state
        att = jnp.exp2(lsig + cr)
        if masked:
            att = jnp.where(msk, att, 0.0)
        acc_s[j, ra:ra + nr, :] += jnp.dot(att.astype(BF16), v, preferred_element_type=F32)

    def run_items(items):
        n = len(items)
        zs = {}
        states = {}
        for s in range(n + 2):
            if s < n:
                zs[s] = stage_scores(items[s])
            if 0 <= s - 1 < n:
                states[s - 1] = stage_cumsum(items[s - 1], zs.pop(s - 1))
            if 0 <= s - 2 < n:
                stage_values(items[s - 2], states.pop(s - 2))

    def max_carry():
        r = jnp.maximum(aux_s[0], aux_s[1])
        r = jnp.max(jnp.where(lane == 0, r, -jnp.inf), axis=0, keepdims=True)
        return jnp.max(r, axis=1, keepdims=True)[0, 0]

    items = []
    for d in range(nsub - 1, -1, -1):
        kv = load_kv(qi * nsub + d)
        ra = d * bk
        while ra < bq:
            nr = SB_ROW_TILE if (bq - ra) % (2 * SB_ROW_TILE) else 2 * SB_ROW_TILE
            items += [(kv, ra, nr, 0, True), (kv, ra, nr, 1, True)]
            ra += nr
    run_items(items)

    def cond(c):
        kb, rmax = c
        return (kb >= 0) & (rmax > SB_LOG2_CUTOFF)

    def body(c):
        kb, _ = c
        items = []
        for kv in (load_kv(kb), load_kv(kb - 1)):
            for ra in range(0, bq, 2 * SB_ROW_TILE):
                items += [(kv, ra, 2 * SB_ROW_TILE, 0, False), (kv, ra, 2 * SB_ROW_TILE, 1, False)]
        run_items(items)
        return kb - 2, max_carry()

    lax.while_loop(cond, body, (qi * nsub - 1, max_carry()))
    o_ref[0] = jnp.where(lane < 64, acc_s[0], acc_s[1])


def _stick_breaking(h_sb, bq=512, bk=128):
    b, t, _ = h_sb.shape
    npair = SB_WIDTH // 128
    return pl.pallas_call(
        functools.partial(_sb_kernel, bq=bq, bk=bk),
        grid=(b, npair, t // bq),
        in_specs=[
            pl.BlockSpec((1, bq, 128), lambda i, p, j: (i, j, p)),
            pl.BlockSpec((1, t, 128), lambda i, p, j: (i, 0, npair + p)),
            pl.BlockSpec((1, t, 128), lambda i, p, j: (i, 0, 2 * npair + p)),
        ],
        out_specs=pl.BlockSpec((1, bq, 128), lambda i, p, j: (i, j, p)),
        out_shape=jax.ShapeDtypeStruct((b, t, SB_WIDTH), F32),
        scratch_shapes=[pltpu.VMEM((2, bq, 128), F32), pltpu.VMEM((2, bq, bk), F32)],
        compiler_params=pltpu.CompilerParams(
            dimension_semantics=("arbitrary", "arbitrary", "arbitrary"),
            vmem_limit_bytes=VMEM_LIMIT_BYTES),
        name="stickbreak",
    )(h_sb, h_sb, h_sb)


def _merge_kernel(x_ref, oa_ref, ob_ref, oc_ref, gt_ref, wa_ref, wb_ref, wc_ref, wo_ref, o_ref):
    d = D_MODEL
    g = _sigmoid(gt_ref[...])
    m = (g[:, 0:d] * _dot(oa_ref[...], wa_ref[...])
         + g[:, d:2 * d] * _dot(ob_ref[...], wb_ref[...])
         + g[:, 2 * d:3 * d] * _dot(oc_ref[...], wc_ref[...]))
    o_ref[...] = x_ref[...] + _dot(m, wo_ref[...])


def _merge(x2d, oa, ob, oc, gates, wa, wb, wc, wo, tm=512):
    n, d = x2d.shape
    rowspec = lambda w: pl.BlockSpec((tm, w), lambda i: (i, 0))
    full = lambda w: pl.BlockSpec(w.shape, lambda i: (0, 0))
    return pl.pallas_call(
        _merge_kernel,
        grid=(n // tm,),
        in_specs=[rowspec(d), rowspec(512), rowspec(512), rowspec(512), rowspec(3 * d),
                  full(wa), full(wb), full(wc), full(wo)],
        out_specs=rowspec(d),
        out_shape=jax.ShapeDtypeStruct((n, d), F32),
        compiler_params=pltpu.CompilerParams(
            dimension_semantics=("arbitrary",), vmem_limit_bytes=VMEM_LIMIT_BYTES),
        name="merge",
    )(x2d, oa, ob, oc, gates, wa, wb, wc, wo)


def _router_kernel(x_ref, g_ref, rw_ref, comb_ref, idx_ref, cnt_ref, run_s):
    @pl.when(pl.program_id(0) == 0)
    def _():
        run_s[...] = jnp.zeros_like(run_s)

    hn = _rms(x_ref[...], g_ref[...])
    logits = _dot3(hn, rw_ref[...])
    tm = logits.shape[0]
    lane = _iota2(logits.shape, 1)
    m1 = jnp.max(logits, axis=-1, keepdims=True)
    i1 = jnp.min(jnp.where(logits == m1, lane, N_EXPERTS), axis=-1, keepdims=True)
    sel1 = lane == i1
    rest = jnp.where(sel1, -jnp.inf, logits)
    m2 = jnp.max(rest, axis=-1, keepdims=True)
    i2 = jnp.min(jnp.where(rest == m2, lane, N_EXPERTS), axis=-1, keepdims=True)
    sel2 = lane == i2
    e2 = jnp.exp(m2 - m1)
    w1 = 1.0 / (1.0 + e2)
    comb_ref[...] = jnp.where(sel1, w1, 0.0) + jnp.where(sel2, e2 * w1, 0.0)

    member = (sel1 | sel2).astype(BF16)
    before = (_iota2((tm, tm), 0) > _iota2((tm, tm), 1)).astype(BF16)
    rank = jnp.dot(before, member, preferred_element_type=F32) + run_s[...]
    r1 = jnp.sum(jnp.where(sel1, rank, 0.0), axis=-1, keepdims=True).astype(jnp.int32)
    r2 = jnp.sum(jnp.where(sel2, rank, 0.0), axis=-1, keepdims=True).astype(jnp.int32)
    run_s[...] += jnp.sum(member.astype(F32), axis=0, keepdims=True)
    cnt_ref[...] = run_s[...]
    lane128 = _iota2((tm, 128), 1)
    idx_ref[...] = jnp.where(lane128 == 0, i1, jnp.where(lane128 == 1, i2,
                             jnp.where(lane128 == 2, r1, jnp.where(lane128 == 3, r2, 0))))


def _router(x2d, g, rw, tm=512):
    n, d = x2d.shape
    return pl.pallas_call(
        _router_kernel,
        grid=(n // tm,),
        in_specs=[pl.BlockSpec((tm, d), lambda i: (i, 0)), pl.BlockSpec((1, d), lambda i: (0, 0)),
                  pl.BlockSpec(rw.shape, lambda i: (0, 0))],
        out_specs=[pl.BlockSpec((tm, N_EXPERTS), lambda i: (i, 0)),
                   pl.BlockSpec((tm, 128), lambda i: (i, 0)),
                   pl.BlockSpec((1, N_EXPERTS), lambda i: (0, 0))],
        out_shape=[jax.ShapeDtypeStruct((n, N_EXPERTS), F32),
                   jax.ShapeDtypeStruct((n, 128), jnp.int32),
                   jax.ShapeDtypeStruct((1, N_EXPERTS), F32)],
        scratch_shapes=[pltpu.VMEM((1, N_EXPERTS), F32)],
        compiler_params=pltpu.CompilerParams(dimension_semantics=("arbitrary",)),
        name="router",
    )(x2d, g.reshape(1, d), rw.astype(F32))


def _row_copy(src_ref, src_row, dst_ref, dst_row, sem):
    return pltpu.make_async_copy(src_ref.at[pl.ds(src_row, 1), :], dst_ref.at[pl.ds(dst_row, 1), :], sem)


def _dispatch_kernel(pos1_ref, pos2_ref, x_ref, g_ref, xs_in_ref, xs_ref, hn_s, sem):
    del xs_in_ref
    tm = hn_s.shape[0]
    hn_s[...] = _rms(x_ref[...], g_ref[...])

    def issue(t, carry):
        _row_copy(hn_s, t, xs_ref, pos1_ref[t], sem).start()
        _row_copy(hn_s, t, xs_ref, pos2_ref[t], sem).start()
        return carry

    lax.fori_loop(0, tm, issue, 0, unroll=8)

    def drain(t, carry):
        _row_copy(hn_s, 0, xs_ref, 0, sem).wait()
        _row_copy(hn_s, 0, xs_ref, 0, sem).wait()
        return carry

    lax.fori_loop(0, tm, drain, 0, unroll=8)


def _dispatch(x2d, g, pos1, pos2, n_rows, tm=256):
    n, d = x2d.shape
    smem = lambda: pl.BlockSpec((tm,), lambda i: (i,), memory_space=pltpu.SMEM)
    return pl.pallas_call(
        _dispatch_kernel,
        grid=(n // tm,),
        in_specs=[smem(), smem(), pl.BlockSpec((tm, d), lambda i: (i, 0)),
                  pl.BlockSpec((1, d), lambda i: (0, 0)), pl.BlockSpec(memory_space=pl.ANY)],
        out_specs=pl.BlockSpec(memory_space=pl.ANY),
        out_shape=jax.ShapeDtypeStruct((n_rows, d), F32),
        scratch_shapes=[pltpu.VMEM((tm, d), F32), pltpu.SemaphoreType.DMA(())],
        input_output_aliases={4: 0},
        compiler_params=pltpu.CompilerParams(dimension_semantics=("arbitrary",)),
        name="moe_dispatch",
    )(pos1, pos2, x2d, g.reshape(1, d), jnp.zeros((n_rows, d), F32))


def _grouped_ffn_kernel(te_ref, nt_ref, xs_ref, wg_ref, wu_ref, wd_ref, o_ref, xb_s, acc_s):
    i = pl.program_id(0)
    f = pl.program_id(1)

    @pl.when(f == 0)
    def _():
        xb_s[...] = xs_ref[...].astype(BF16)
        acc_s[...] = jnp.zeros_like(acc_s)

    @pl.when(i < nt_ref[0])
    def _():
        xb = xb_s[...]
        act = _silu(jnp.dot(xb, wg_ref[...], preferred_element_type=F32)) * jnp.dot(
            xb, wu_ref[...], preferred_element_type=F32)
        acc_s[...] += jnp.dot(act.astype(BF16), wd_ref[...], preferred_element_type=F32)

    @pl.when(f == pl.num_programs(1) - 1)
    def _():
        o_ref[...] = acc_s[...]


def _grouped_ffn(xs, tile_expert, n_tiles, wg, wu, wd, tm, tf=896):
    n_rows, d = xs.shape
    dff = wg.shape[2]
    nf = dff // tf
    fsel = lambda i, f, te, nt: jnp.where(i < nt[0], f, nf - 1)
    return pl.pallas_call(
        _grouped_ffn_kernel,
        grid_spec=pltpu.PrefetchScalarGridSpec(
            num_scalar_prefetch=2,
            grid=(n_rows // tm, nf),
            in_specs=[pl.BlockSpec((tm, d), lambda i, f, te, nt: (i, 0)),
                      pl.BlockSpec((None, d, tf), lambda i, f, te, nt: (te[i], 0, fsel(i, f, te, nt))),
                      pl.BlockSpec((None, d, tf), lambda i, f, te, nt: (te[i], 0, fsel(i, f, te, nt))),
                      pl.BlockSpec((None, tf, d), lambda i, f, te, nt: (te[i], fsel(i, f, te, nt), 0))],
            out_specs=pl.BlockSpec((tm, d), lambda i, f, te, nt: (i, 0)),
            scratch_shapes=[pltpu.VMEM((tm, d), BF16), pltpu.VMEM((tm, d), F32)]),
        out_shape=jax.ShapeDtypeStruct((n_rows, d), F32),
        compiler_params=pltpu.CompilerParams(
            dimension_semantics=("arbitrary", "arbitrary"), vmem_limit_bytes=VMEM_LIMIT_BYTES),
        name="moe_grouped_ffn",
    )(tile_expert, n_tiles, xs, wg, wu, wd)


def _combine_kernel(*refs, final):
    if final:
        pos1_ref, pos2_ref, x_ref, comb_ref, fg_ref, ys_ref, o_ref, y1_s, y2_s, sem = refs
    else:
        pos1_ref, pos2_ref, x_ref, comb_ref, ys_ref, o_ref, y1_s, y2_s, sem = refs
    tm = y1_s.shape[0]

    def issue(t, carry):
        _row_copy(ys_ref, pos1_ref[t], y1_s, t, sem).start()
        _row_copy(ys_ref, pos2_ref[t], y2_s, t, sem).start()
        return carry

    lax.fori_loop(0, tm, issue, 0, unroll=8)

    def drain(t, carry):
        _row_copy(ys_ref, 0, y1_s, 0, sem).wait()
        _row_copy(ys_ref, 0, y2_s, 0, sem).wait()
        return carry

    lax.fori_loop(0, tm, drain, 0, unroll=8)
    comb = comb_ref[...]
    w1 = jnp.max(comb, axis=-1, keepdims=True)
    w2 = jnp.sum(comb, axis=-1, keepdims=True) - w1
    out = x_ref[...] + w1 * y1_s[...] + w2 * y2_s[...]
    if final:
        out = _rms(out, fg_ref[...])
    o_ref[...] = out


def _combine(x2d, comb, pos1, pos2, ys, final_g=None, tm=256):
    n, d = x2d.shape
    final = final_g is not None
    smem = lambda: pl.BlockSpec((tm,), lambda i: (i,), memory_space=pltpu.SMEM)
    in_specs = [smem(), smem(), pl.BlockSpec((tm, d), lambda i: (i, 0)),
                pl.BlockSpec((tm, N_EXPERTS), lambda i: (i, 0))]
    args = [pos1, pos2, x2d, comb]
    if final:
        in_specs.append(pl.BlockSpec((1, d), lambda i: (0, 0)))
        args.append(final_g.reshape(1, d))
    in_specs.append(pl.BlockSpec(memory_space=pl.ANY))
    args.append(ys)
    return pl.pallas_call(
        functools.partial(_combine_kernel, final=final),
        grid=(n // tm,),
        in_specs=in_specs,
        out_specs=pl.BlockSpec((tm, d), lambda i: (i, 0)),
        out_shape=jax.ShapeDtypeStruct((n, d), F32),
        scratch_shapes=[pltpu.VMEM((tm, d), F32), pltpu.VMEM((tm, d), F32), pltpu.SemaphoreType.DMA(())],
        compiler_params=pltpu.CompilerParams(dimension_semantics=("arbitrary",)),
        name="moe_combine",
    )(*args)


def _moe(x2d, g, rw, wg, wu, wd, final_g=None, tm=MOE_ROW_TILE):
    n, d = x2d.shape
    comb, idx, cnt = _router(x2d, g, rw)
    counts = cnt[0].astype(jnp.int32)
    padded = ((counts + tm - 1) // tm) * tm
    ends = jnp.cumsum(padded)
    starts = ends - padded
    pos1 = starts[idx[:, 0]] + idx[:, 2]
    pos2 = starts[idx[:, 1]] + idx[:, 3]
    n_rows = 2 * n + N_EXPERTS * tm
    n_tiles = (ends[-1] // tm).reshape(1)
    tile_expert = jnp.minimum(
        jnp.searchsorted(ends // tm, jnp.arange(n_rows // tm, dtype=jnp.int32), side="right"),
        N_EXPERTS - 1).astype(jnp.int32)
    xs = _dispatch(x2d, g, pos1, pos2, n_rows)
    ys = _grouped_ffn(xs, tile_expert, n_tiles, wg, wu, wd, tm)
    return _combine(x2d, comb, pos1, pos2, ys, final_g=final_g)


def _ffn_kernel(*refs, final):
    if final:
        x_ref, g_ref, wg_ref, wu_ref, wd_ref, fg_ref, o_ref, hn_s, acc_s = refs
    else:
        x_ref, g_ref, wg_ref, wu_ref, wd_ref, o_ref, hn_s, acc_s = refs
    f = pl.program_id(1)

    @pl.when(f == 0)
    def _():
        hn_s[...] = _rms(x_ref[...], g_ref[...]).astype(BF16)
        acc_s[...] = jnp.zeros_like(acc_s)

    hn = hn_s[...]
    act = _silu(jnp.dot(hn, wg_ref[...], preferred_element_type=F32)) * jnp.dot(
        hn, wu_ref[...], preferred_element_type=F32)
    acc_s[...] += jnp.dot(act.astype(BF16), wd_ref[...], preferred_element_type=F32)

    @pl.when(f == pl.num_programs(1) - 1)
    def _():
        out = x_ref[...] + acc_s[...]
        if final:
            out = _rms(out, fg_ref[...])
        o_ref[...] = out


def _ffn(x2d, g, wg, wu, wd, final_g=None, tm=1024, tf=256):
    n, d = x2d.shape
    final = final_g is not None
    dff = wg.shape[1]
    xmap = lambda i, f: (i, 0)
    cmap = lambda i, f: (0, 0)
    in_specs = [pl.BlockSpec((tm, d), xmap), pl.BlockSpec((1, d), cmap),
                pl.BlockSpec((d, tf), lambda i, f: (0, f)),
                pl.BlockSpec((d, tf), lambda i, f: (0, f)),
                pl.BlockSpec((tf, d), lambda i, f: (f, 0))]
    args = [x2d, g.reshape(1, d), wg, wu, wd]
    if final:
        in_specs.append(pl.BlockSpec((1, d), cmap))
        args.append(final_g.reshape(1, d))
    return pl.pallas_call(
        functools.partial(_ffn_kernel, final=final),
        grid=(n // tm, dff // tf),
        in_specs=in_specs,
        out_specs=pl.BlockSpec((tm, d), xmap),
        out_shape=jax.ShapeDtypeStruct((n, d), F32),
        scratch_shapes=[pltpu.VMEM((tm, d), BF16), pltpu.VMEM((tm, d), F32)],
        compiler_params=pltpu.CompilerParams(
            dimension_semantics=("arbitrary", "arbitrary"), vmem_limit_bytes=VMEM_LIMIT_BYTES),
        name="dense_ffn",
    )(*args)


def kernel(x, norm_mix_g, w_in, gdn_conv_w, gdn_a_log, gdn_dt_bias, gdn_norm_g, rwkv_mu, rwkv_w0, rwkv_w2, rwkv_a0, rwkv_a2, rwkv_g2, rwkv_k_k, rwkv_k_a, rwkv_r_k, rwkv_ln_g, rwkv_ln_b, w_branch_gdn, w_branch_rwkv, w_branch_sb, w_out, norm_ffn_g, ffn_w_gate, ffn_w_up, ffn_w_down, router_w, moe_w_gate, moe_w_up, moe_w_down, final_norm_g):
    b, t, d = x.shape
    n = b * t
    depth = w_in.shape[0]
    x2 = x.reshape(n, d).astype(F32)
    for layer in range(depth):
        w = w_in[layer]
        w_gdn = jnp.concatenate(
            [w[:, 0:2048], jnp.repeat(w[:, 2048:2052], 128, axis=1), jnp.repeat(w[:, 2052:2056], 128, axis=1)],
            axis=1).astype(BF16)
        w_rwkv = w[:, 2056:3848].astype(BF16)
        w_sb = w[:, 3848:5384].astype(BF16)
        w_gates = w[:, 5384:8456].astype(BF16)
        g_mix = norm_mix_g[layer].astype(F32)
        p_gdn, p_rwkv = _normproj(x2, g_mix, [w_gdn, w_rwkv])
        p_sb, p_gates = _normproj(x2, g_mix, [w_sb, w_gates])
        o_a = _gdn(p_gdn.reshape(b, t, -1), gdn_conv_w[layer], gdn_a_log[layer], gdn_dt_bias[layer],
                   gdn_norm_g[layer])
        o_b = _rwkv(p_rwkv.reshape(b, t, -1), rwkv_mu[layer], rwkv_w0[layer], rwkv_w2[layer],
                    rwkv_a0[layer], rwkv_a2[layer], rwkv_g2[layer], rwkv_k_k[layer], rwkv_k_a[layer],
                    rwkv_r_k[layer].reshape(-1), rwkv_ln_g[layer], rwkv_ln_b[layer])
        o_c = _stick_breaking(p_sb.reshape(b, t, -1))
        x2 = _merge(x2, o_a.reshape(n, -1), o_b.reshape(n, -1), o_c.reshape(n, -1), p_gates,
                    w_branch_gdn[layer].astype(BF16), w_branch_rwkv[layer].astype(BF16),
                    w_branch_sb[layer].astype(BF16), w_out[layer].astype(BF16))
        g_ffn = norm_ffn_g[layer].astype(F32)
        final_g = final_norm_g.astype(F32) if layer == depth - 1 else None
        i = layer // 2
        if layer % 2 == 0:
            x2 = _ffn(x2, g_ffn, ffn_w_gate[i].astype(BF16), ffn_w_up[i].astype(BF16),
                      ffn_w_down[i].astype(BF16), final_g=final_g, tf=256)
        else:
            x2 = _moe(x2, g_ffn, router_w[i], moe_w_gate[i].astype(BF16), moe_w_up[i].astype(BF16),
                      moe_w_down[i].astype(BF16), final_g=final_g)
    return x2.reshape(b, t, d)
```

```python
import functools

import jax
import jax.numpy as jnp
from jax import lax
from jax.experimental import pallas as pl
from jax.experimental.pallas import tpu as pltpu

F32 = jnp.float32
BF16 = jnp.bfloat16

RMS_EPS = 1e-6
GN_EPS = 64e-5
L2_EPS = 1e-6

D_MODEL = 1024
GDN_HEADS = 4
GDN_HEAD_DIM = 128
GDN_WIDTH = 512
RWKV_WIDTH = 512
RWKV_HEAD_DIM = 64
RWKV_IN = 1792
SB_WIDTH = 512
SB_HEAD_DIM = 64
N_EXPERTS = 8
CHUNK = 64
LOG2E = 1.4426950408889634
SB_LOG2_CUTOFF = -160.0
SB_ROW_TILE = 128
GDN_CHUNKS_PER_STEP = 8
RWKV_CHUNKS_PER_STEP = 4
MOE_ROW_TILE = 512
VMEM_LIMIT_BYTES = 56 * 1024 * 1024


def _dot(a, b):
    return jnp.dot(a.astype(BF16), b.astype(BF16), preferred_element_type=F32)


def _dot_nt(a, b):
    return lax.dot_general(a.astype(BF16), b.astype(BF16), (((1,), (1,)), ((), ())),
                           preferred_element_type=F32)


def _split2(a):
    hi = a.astype(BF16)
    lo = (a - hi.astype(F32)).astype(BF16)
    return hi, lo


def _split3(a):
    hi = a.astype(BF16)
    r = a - hi.astype(F32)
    mid = r.astype(BF16)
    lo = (r - mid.astype(F32)).astype(BF16)
    return hi, mid, lo


def _dot3(a, b):
    ah, al = _split2(a)
    bh, bl = _split2(b)
    return (jnp.dot(ah, bh, preferred_element_type=F32)
            + jnp.dot(ah, bl, preferred_element_type=F32)
            + jnp.dot(al, bh, preferred_element_type=F32))


def _dot_exact_lhs(m, x, parts=3):
    xs = _split3(x) if parts == 3 else _split2(x)
    out = jnp.dot(m, xs[0], preferred_element_type=F32)
    for p in xs[1:]:
        out = out + jnp.dot(m, p, preferred_element_type=F32)
    return out


def _dot_exact_rhs(x, m, parts=2):
    xs = _split3(x) if parts == 3 else _split2(x)
    out = jnp.dot(xs[0], m, preferred_element_type=F32)
    for p in xs[1:]:
        out = out + jnp.dot(p, m, preferred_element_type=F32)
    return out


def _iota2(shape, dim):
    return lax.broadcasted_iota(jnp.int32, shape, dim)


def _eye(n, dtype=F32):
    return (_iota2((n, n), 0) == _iota2((n, n), 1)).astype(dtype)


def _softplus(x):
    return jnp.maximum(x, 0.0) + jnp.log(1.0 + jnp.exp(-jnp.abs(x)))


def _sigmoid(x):
    return 1.0 / (1.0 + jnp.exp(-x))


def _silu(x):
    return x * _sigmoid(x)


def _rms(x, g):
    return x * lax.rsqrt(jnp.mean(x * x, axis=-1, keepdims=True) + RMS_EPS) * g


def _nilpotent_inverse(n, eye, dot):
    t = eye + n
    x = n
    for _ in range(5):
        x = dot(x, x)
        t = t + dot(t, x)
    return t


def _nilpotent_inverse_many(ns, eye, dot):
    ts = [eye + n for n in ns]
    xs = list(ns)
    for _ in range(5):
        xs = [dot(x, x) for x in xs]
        ts = [t + dot(t, x) for t, x in zip(ts, xs)]
    return ts


def _chunk_cumsum(x):
    i = _iota2((128, 128), 0)
    j = _iota2((128, 128), 1)
    m = ((j <= i) & ((i >> 6) == (j >> 6))).astype(BF16)
    return jnp.concatenate(
        [_dot_exact_lhs(m, x[r:r + 128]) for r in range(0, x.shape[0], 128)], axis=0)


def _head_sum(x):
    i = _iota2((128, 128), 0)
    j = _iota2((128, 128), 1)
    m = ((i >> 6) == (j >> 6)).astype(BF16)
    return jnp.concatenate(
        [_dot_exact_rhs(x[:, c:c + 128], m) for c in range(0, x.shape[1], 128)], axis=1)


def _normproj_kernel(x_ref, g_ref, *refs, n_out):
    w_refs = refs[:n_out]
    o_refs = refs[n_out:]
    hn = _rms(x_ref[...], g_ref[...]).astype(BF16)
    for w_ref, o_ref in zip(w_refs, o_refs):
        o_ref[...] = jnp.dot(hn, w_ref[...], preferred_element_type=F32)


def _normproj(x2d, g, weights, tm=256):
    n, d = x2d.shape
    n_out = len(weights)
    in_specs = [pl.BlockSpec((tm, d), lambda i: (i, 0)), pl.BlockSpec((1, d), lambda i: (0, 0))]
    in_specs += [pl.BlockSpec(w.shape, lambda i: (0, 0)) for w in weights]
    out_specs = [pl.BlockSpec((tm, w.shape[1]), lambda i: (i, 0)) for w in weights]
    out_shape = [jax.ShapeDtypeStruct((n, w.shape[1]), F32) for w in weights]
    return pl.pallas_call(
        functools.partial(_normproj_kernel, n_out=n_out),
        grid=(n // tm,),
        in_specs=in_specs,
        out_specs=out_specs,
        out_shape=out_shape,
        compiler_params=pltpu.CompilerParams(
            dimension_semantics=("arbitrary",), vmem_limit_bytes=VMEM_LIMIT_BYTES),
        name="normproj",
    )(x2d, g.reshape(1, d), *weights)


def _gdn_kernel(qkv_ref, z_ref, b_ref, a_ref, cw_ref, alog_ref, dtb_ref, ng_ref, o_ref,
                ext_s, state_s, q_s, k_s, v_s, beta_s, gc_s, p_s, qq_s, r_s, zz_s, oc_s, *, tt):
    nc = tt // CHUNK
    w3 = 3 * GDN_WIDTH
    t = pl.program_id(1)

    @pl.when(t == 0)
    def _():
        ext_s[0:8, :] = jnp.zeros((8, w3), F32)
        state_s[...] = jnp.zeros_like(state_s)

    raw = qkv_ref[0]
    ext_s[8:8 + tt, :] = raw
    cw = cw_ref[...]
    y = raw * cw[3:4, :]
    for i in range(3):
        y = y + ext_s[5 + i:5 + i + tt, :] * cw[i:i + 1, :]
    ext_s[0:8, :] = raw[tt - 8:tt, :]
    y = _silu(y)

    for h in range(GDN_HEADS):
        sl = slice(128 * h, 128 * h + 128)
        qh = y[:, 128 * h:128 * h + 128]
        kh = y[:, GDN_WIDTH + 128 * h:GDN_WIDTH + 128 * h + 128]
        vh = y[:, 2 * GDN_WIDTH + 128 * h:2 * GDN_WIDTH + 128 * h + 128]
        qh = qh * lax.rsqrt(jnp.sum(qh * qh, axis=-1, keepdims=True) + L2_EPS) * (GDN_HEAD_DIM ** -0.5)
        kh = kh * lax.rsqrt(jnp.sum(kh * kh, axis=-1, keepdims=True) + L2_EPS)
        q_s[:, sl] = qh
        k_s[:, sl] = kh
        v_s[:, sl] = vh

    beta_s[...] = _sigmoid(b_ref[0])
    g = -jnp.exp(alog_ref[...]) * _softplus(a_ref[0] + dtb_ref[...])
    gc_s[...] = _chunk_cumsum(g)

    eye64 = _eye(CHUNK)
    eye128 = _eye(128)
    ii = _iota2((CHUNK, CHUNK), 0)
    jj = _iota2((CHUNK, CHUNK), 1)

    def chunk_body(ci, carry):
        probs = [(ci * GDN_CHUNKS_PER_STEP + u, h) for u in range(GDN_CHUNKS_PER_STEP)
                 for h in range(GDN_HEADS)]
        rows = [pl.ds(pl.multiple_of(c * CHUNK, CHUNK), CHUNK) for c, _ in probs]
        lanes = [slice(128 * h, 128 * h + 128) for _, h in probs]
        idx = range(len(probs))
        q = [q_s[rows[i], lanes[i]] for i in idx]
        k = [k_s[rows[i], lanes[i]] for i in idx]
        gcc = [gc_s[rows[i], lanes[i]] for i in idx]
        gl = [g[CHUNK - 1:CHUNK, :] for g in gcc]
        gcr = [g.T[0:CHUNK, :] for g in gcc]
        dec_incl = [jnp.exp(jnp.where(ii >= jj, gcc[i][:, 0:CHUNK] - gcr[i], -jnp.inf)) for i in idx]
        kb = [k[i] * beta_s[rows[i], lanes[i]] for i in idx]
        a_mat = [_dot_nt(kb[i], k[i]) for i in idx]
        attn = [_dot_nt(q[i], k[i]) * dec_incl[i] for i in idx]
        kdt = [(k[i] * jnp.exp(gl[i] - gcc[i])).T for i in idx]
        tinv = _nilpotent_inverse_many(
            [-a_mat[i] * jnp.where(ii > jj, dec_incl[i], 0.0) for i in idx], eye64, _dot)
        u = [_dot(tinv[i], v_s[rows[i], lanes[i]] * beta_s[rows[i], lanes[i]]) for i in idx]
        w = [_dot(tinv[i], kb[i] * jnp.exp(gcc[i])) for i in idx]
        for i, (c, h) in enumerate(probs):
            p_s[c, h] = jnp.exp(gl[i]) * eye128 - _dot(kdt[i], w[i])
        for i, (c, h) in enumerate(probs):
            qq_s[c, h] = _dot(kdt[i], u[i])
        for i, (c, h) in enumerate(probs):
            r_s[c, h] = q[i] * jnp.exp(gcc[i]) - _dot(attn[i], w[i])
        for i, (c, h) in enumerate(probs):
            zz_s[c, h] = _dot(attn[i], u[i])
        return carry

    lax.fori_loop(0, nc // GDN_CHUNKS_PER_STEP, chunk_body, 0)

    def scan_body(c, carry):
        rows = pl.ds(pl.multiple_of(c * CHUNK, CHUNK), CHUNK)
        s = [state_s[h] for h in range(GDN_HEADS)]
        s_new = [_dot3(p_s[c, h], s[h]) for h in range(GDN_HEADS)]
        o = [_dot(r_s[c, h], s[h]) for h in range(GDN_HEADS)]
        for h in range(GDN_HEADS):
            state_s[h] = s_new[h] + qq_s[c, h]
            oc_s[rows, 128 * h:128 * h + 128] = o[h] + zz_s[c, h]
        return carry

    lax.fori_loop(0, nc, scan_body, 0)

    o = oc_s[...]
    z = z_ref[0]
    ng = ng_ref[...]
    for h in range(GDN_HEADS):
        sl = slice(128 * h, 128 * h + 128)
        oh = o[:, sl]
        oh = oh * lax.rsqrt(jnp.mean(oh * oh, axis=-1, keepdims=True) + RMS_EPS) * ng[:, sl]
        o_ref[0, :, sl] = oh * _silu(z[:, sl])


def _gdn(proj_a, conv_w, a_log, dt_bias, norm_g, tt=512):
    b, t, _ = proj_a.shape
    nc = tt // CHUNK
    w3 = 3 * GDN_WIDTH
    rep = lambda p: jnp.repeat(p.astype(F32), 128).reshape(1, GDN_WIDTH)
    small = lambda shape: pl.BlockSpec(shape, lambda i, j: (0, 0))
    return pl.pallas_call(
        functools.partial(_gdn_kernel, tt=tt),
        grid=(b, t // tt),
        in_specs=[
            pl.BlockSpec((1, tt, w3), lambda i, j: (i, j, 0)),
            pl.BlockSpec((1, tt, GDN_WIDTH), lambda i, j: (i, j, 3)),
            pl.BlockSpec((1, tt, GDN_WIDTH), lambda i, j: (i, j, 4)),
            pl.BlockSpec((1, tt, GDN_WIDTH), lambda i, j: (i, j, 5)),
            small((4, w3)), small((1, GDN_WIDTH)), small((1, GDN_WIDTH)), small((1, GDN_WIDTH)),
        ],
        out_specs=pl.BlockSpec((1, tt, GDN_WIDTH), lambda i, j: (i, j, 0)),
        out_shape=jax.ShapeDtypeStruct((b, t, GDN_WIDTH), F32),
        scratch_shapes=[
            pltpu.VMEM((tt + 8, w3), F32),
            pltpu.VMEM((GDN_HEADS, 128, 128), F32),
            pltpu.VMEM((tt, GDN_WIDTH), F32), pltpu.VMEM((tt, GDN_WIDTH), F32),
            pltpu.VMEM((tt, GDN_WIDTH), F32), pltpu.VMEM((tt, GDN_WIDTH), F32),
            pltpu.VMEM((tt, GDN_WIDTH), F32),
            pltpu.VMEM((nc, GDN_HEADS, 128, 128), F32), pltpu.VMEM((nc, GDN_HEADS, 128, 128), F32),
            pltpu.VMEM((nc, GDN_HEADS, CHUNK, 128), F32), pltpu.VMEM((nc, GDN_HEADS, CHUNK, 128), F32),
            pltpu.VMEM((tt, GDN_WIDTH), F32),
        ],
        compiler_params=pltpu.CompilerParams(
            dimension_semantics=("arbitrary", "arbitrary"), vmem_limit_bytes=VMEM_LIMIT_BYTES),
        name="gdn",
    )(proj_a, proj_a, proj_a, proj_a, conv_w.astype(F32), rep(a_log), rep(dt_bias),
      jnp.tile(norm_g.astype(F32), GDN_HEADS).reshape(1, GDN_WIDTH))


def _rwkv_kernel(h_ref, mu_ref, w0_ref, a0_ref, kk_ref, ka_ref, rk_ref, lng_ref, lnb_ref,
                 wa_ref, g2_ref, o_ref,
                 ext_s, state_s, r_s, kn_s, k2_s, v_s, a_s, lw_s, lc_s, p_s, qq_s, rh_s, yc_s, y_s,
                 *, tt):
    nc = tt // CHUNK
    npair = RWKV_WIDTH // 128
    t = pl.program_id(1)

    @pl.when(t == 0)
    def _():
        ext_s[0:8, :] = jnp.zeros((8, RWKV_IN), F32)
        state_s[...] = jnp.zeros_like(state_s)

    raw = h_ref[0]
    ext_s[8:8 + tt, :] = raw
    prev = ext_s[7:7 + tt, :]
    ext_s[0:8, :] = raw[tt - 8:tt, :]
    hl = raw + (prev - raw) * mu_ref[...]
    r = hl[:, 0:512]
    k = hl[:, 512:1024]
    v = hl[:, 1024:1536]
    xwa = hl[:, 1536:1664]
    xg = hl[:, 1664:1792]
    lane128 = _iota2((1, 128), 1)
    xwa = jnp.where(lane128 < 64, jnp.tanh(xwa), xwa)
    lora = _dot3(xwa, wa_ref[...])
    w_log = -_softplus(-(w0_ref[...] + lora[:, 0:512])) - 0.5
    lw = -jnp.exp(w_log)
    a = _sigmoid(a0_ref[...] + lora[:, 512:1024])
    gate = _dot3(_sigmoid(xg), g2_ref[...])
    kk = k * kk_ref[...]
    k2 = k * (1.0 + (a - 1.0) * ka_ref[...])
    kn = kk * lax.rsqrt(_head_sum(kk * kk) + L2_EPS)
    r_s[...] = r
    kn_s[...] = kn
    k2_s[...] = k2
    v_s[...] = v
    a_s[...] = a
    lw_s[...] = lw
    lc_s[...] = _chunk_cumsum(lw)

    eye64 = _eye(CHUNK)
    eye128 = _eye(128)
    ii = _iota2((CHUNK, CHUNK), 0)
    jj = _iota2((CHUNK, CHUNK), 1)
    strict = ii > jj
    incl = ii >= jj
    lane = _iota2((CHUNK, 128), 1)
    bi = _iota2((128, 128), 0)
    bj = _iota2((128, 128), 1)
    blockdiag = (bi < 64) == (bj < 64)

    def chunk_body(ci, carry):
        units = [(ci * RWKV_CHUNKS_PER_STEP + u, p) for u in range(RWKV_CHUNKS_PER_STEP) for p in range(npair)]
        nu = range(len(units))
        rows = [pl.ds(pl.multiple_of(c * CHUNK, CHUNK), CHUNK) for c, _ in units]
        sls = [slice(128 * p, 128 * p + 128) for _, p in units]
        lcc = [lc_s[rows[u], sls[u]] for u in nu]
        vc = [v_s[rows[u], sls[u]] for u in nu]
        lcl = [x[CHUNK - 1:CHUNK, :] for x in lcc]
        ginv = [jnp.exp(-x) for x in lcc]
        gend = [jnp.exp(lcl[u] - lcc[u]) for u in nu]
        kna = [kn_s[rows[u], sls[u]] * a_s[rows[u], sls[u]] for u in nu]
        at = [-kn_s[rows[u], sls[u]] * jnp.exp(lcc[u] - lw_s[rows[u], sls[u]]) for u in nu]
        bt = [kna[u] * ginv[u] for u in nu]
        kt = [k2_s[rows[u], sls[u]] * ginv[u] for u in nu]
        rt = [r_s[rows[u], sls[u]] * jnp.exp(lcc[u]) for u in nu]
        probs = [(u, j) for u in nu for j in range(2)]
        npr = range(len(probs))
        masks = [(lane < 64) if j == 0 else (lane >= 64) for _, j in probs]
        at_m = [jnp.where(masks[i], at[u], 0.0) for i, (u, _) in enumerate(probs)]
        rt_m = [jnp.where(masks[i], rt[u], 0.0) for i, (u, _) in enumerate(probs)]
        ar = [jnp.concatenate([at_m[i], rt_m[i]], axis=0) for i in npr]
        xb = [_dot_nt(ar[i], bt[u]) for i, (u, _) in enumerate(probs)]
        xk = [_dot_nt(ar[i], kt[u]) for i, (u, _) in enumerate(probs)]
        a_ab = [jnp.where(strict, x[0:CHUNK], 0.0) for x in xb]
        a_rb = [jnp.where(incl, x[CHUNK:2 * CHUNK], 0.0) for x in xb]
        a_ak = [jnp.where(strict, x[0:CHUNK], 0.0) for x in xk]
        a_rk = [jnp.where(incl, x[CHUNK:2 * CHUNK], 0.0) for x in xk]
        akv = [_dot(a_ak[i], vc[u]) for i, (u, _) in enumerate(probs)]
        arkv = [_dot(a_rk[i], vc[u]) for i, (u, _) in enumerate(probs)]
        bbt = [(kna[u] * gend[u]).T for u in nu]
        kbt = [(k2_s[rows[u], sls[u]] * gend[u]).T for u in nu]
        tinv = _nilpotent_inverse_many(a_ab, eye64, _dot)
        au = [_dot(tinv[i], jnp.concatenate([at_m[i], akv[i]], axis=1)) for i in npr]
        ry = [_dot(a_rb[i], au[i]) for i in npr]
        kv = [_dot(kbt[u], vc[u]) for u in nu]
        pq = []
        for u, (c, p) in enumerate(units):
            i0, i1 = 2 * u, 2 * u + 1
            ahat = au[i0][:, 0:128] + au[i1][:, 0:128]
            uv = jnp.where(lane < 64, au[i0][:, 128:256], au[i1][:, 128:256])
            rh_s[c, p] = rt_m[i0] + rt_m[i1] + ry[i0][:, 0:128] + ry[i1][:, 0:128]
            yc_s[c, p] = jnp.where(lane < 64, ry[i0][:, 128:256] + arkv[i0], ry[i1][:, 128:256] + arkv[i1])
            pq.append(_dot(bbt[u], jnp.concatenate([ahat, uv], axis=1)))
        for u, (c, p) in enumerate(units):
            p_s[c, p] = jnp.where(blockdiag, pq[u][:, 0:128], 0.0) + eye128 * jnp.exp(lcl[u])
            qq_s[c, p] = jnp.where(blockdiag, pq[u][:, 128:256] + kv[u], 0.0)
        return carry

    lax.fori_loop(0, nc // RWKV_CHUNKS_PER_STEP, chunk_body, 0)

    def scan_body(c, carry):
        rows = pl.ds(pl.multiple_of(c * CHUNK, CHUNK), CHUNK)
        s = [state_s[p] for p in range(npair)]
        s_new = [_dot3(p_s[c, p], s[p]) for p in range(npair)]
        y = [_dot(rh_s[c, p], s[p]) for p in range(npair)]
        for p in range(npair):
            state_s[p] = s_new[p] + qq_s[c, p]
            y_s[rows, 128 * p:128 * p + 128] = y[p] + yc_s[c, p]
        return carry

    lax.fori_loop(0, nc, scan_body, 0)

    y = y_s[...]
    mean = _head_sum(y) * (1.0 / RWKV_HEAD_DIM)
    yc = y - mean
    var = _head_sum(yc * yc) * (1.0 / RWKV_HEAD_DIM)
    yn = yc * lax.rsqrt(var + GN_EPS) * lng_ref[...] + lnb_ref[...]
    bonus = _head_sum(r * k2 * rk_ref[...]) * v
    o_ref[0] = (yn + bonus) * gate


def _dot3_nt(a, b):
    ah, al = _split2(a)
    bh, bl = _split2(b)
    dn = (((1,), (1,)), ((), ()))
    return (lax.dot_general(ah, bh, dn, preferred_element_type=F32)
            + lax.dot_general(ah, bl, dn, preferred_element_type=F32)
            + lax.dot_general(al, bh, dn, preferred_element_type=F32))


def _rwkv(h, mu, w0, w2, a0, a2, g2, k_k, k_a, r_k, ln_g, ln_b, tt=512):
    b, t, _ = h.shape
    nc = tt // CHUNK
    npair = RWKV_WIDTH // 128
    row = lambda p: p.astype(F32).reshape(1, -1)
    wa = jnp.zeros((128, 2 * RWKV_WIDTH), F32)
    wa = wa.at[0:64, 0:RWKV_WIDTH].set(w2.astype(F32)).at[64:128, RWKV_WIDTH:].set(a2.astype(F32))
    small = lambda shape: pl.BlockSpec(shape, lambda i, j: (0, 0))
    vec = small((1, RWKV_WIDTH))
    return pl.pallas_call(
        functools.partial(_rwkv_kernel, tt=tt),
        grid=(b, t // tt),
        in_specs=[pl.BlockSpec((1, tt, RWKV_IN), lambda i, j: (i, j, 0)), small((1, RWKV_IN)),
                  vec, vec, vec, vec, vec, vec, vec,
                  small((128, 2 * RWKV_WIDTH)), small((128, RWKV_WIDTH))],
        out_specs=pl.BlockSpec((1, tt, RWKV_WIDTH), lambda i, j: (i, j, 0)),
        out_shape=jax.ShapeDtypeStruct((b, t, RWKV_WIDTH), F32),
        scratch_shapes=[
            pltpu.VMEM((tt + 8, RWKV_IN), F32),
            pltpu.VMEM((npair, 128, 128), F32),
        ] + [pltpu.VMEM((tt, RWKV_WIDTH), F32)] * 7 + [
            pltpu.VMEM((nc, npair, 128, 128), F32), pltpu.VMEM((nc, npair, 128, 128), F32),
            pltpu.VMEM((nc, npair, CHUNK, 128), F32), pltpu.VMEM((nc, npair, CHUNK, 128), F32),
            pltpu.VMEM((tt, RWKV_WIDTH), F32),
        ],
        compiler_params=pltpu.CompilerParams(
            dimension_semantics=("arbitrary", "arbitrary"), vmem_limit_bytes=VMEM_LIMIT_BYTES),
        name="rwkv7",
    )(h, row(mu), row(w0), row(a0), row(k_k), row(k_a), row(r_k), row(ln_g), row(ln_b),
      wa, g2.astype(F32))


def _sb_kernel(q_ref, k_ref, v_ref, o_ref, acc_s, aux_s, *, bq, bk):
    qi = pl.program_id(2)
    nsub = bq // bk
    q = q_ref[0] * (SB_HEAD_DIM ** -0.5 * LOG2E)
    lane = _iota2((1, 128), 1)
    qm = [jnp.where(lane < 64, q, 0.0).astype(BF16), jnp.where(lane >= 64, q, 0.0).astype(BF16)]
    ti = _iota2((2 * bk, bk), 0)
    tj = _iota2((2 * bk, bk), 1)
    cum_mat = ((ti & (bk - 1)) > tj).astype(BF16)
    acc_s[...] = jnp.zeros_like(acc_s)
    aux_s[...] = jnp.zeros_like(aux_s)

    def load_kv(kb):
        start = pl.multiple_of(kb * bk, bk)
        return (start, k_ref[0, pl.ds(start, bk), :].astype(BF16), v_ref[0, pl.ds(start, bk), :].astype(BF16))

    def stage_scores(item):
        (start, k, v), ra, nr, j, masked = item
        z = lax.dot_general(qm[j][ra:ra + nr], k, (((1,), (1,)), ((), ())), preferred_element_type=F32)
        return z

    def stage_cumsum(item, z):
        (start, k, v), ra, nr, j, masked = item
        rows = slice(ra, ra + nr)
        neg_abs = lax.bitcast_convert_type(
            lax.bitcast_convert_type(z, jnp.uint32) | jnp.uint32(0x80000000), F32)
        lsig = jnp.minimum(z, 0.0) - jnp.log(1.0 + jnp.exp2(neg_abs)) * LOG2E
        l1 = lsig - z
        msk = None
        if masked:
            msk = (start + _iota2((nr, bk), 1)) < (qi * bq + ra + _iota2((nr, bk), 0))
            l1 = jnp.where(msk, l1, 0.0)
        l1_hi = l1.astype(BF16)
        l1_lo = (l1 - l1_hi.astype(F32)).astype(BF16)
        cr = (jnp.dot(jnp.concatenate([l1_hi, l1_lo], axis=1), cum_mat, preferred_element_type=F32)
              + aux_s[j, rows, 0:1])
        aux_s[j, rows, :] = cr + l1
        return lsig, cr, msk

    def stage_values(item, state):
        (start, k, v), ra, nr, j, masked = item
        lsig, cr, msk = state
        att = jnp.exp2(lsig + cr)
        if masked:
            att = jnp.where(msk, att, 0.0)
        acc_s[j, ra:ra + nr, :] += jnp.dot(att.astype(BF16), v, preferred_element_type=F32)

    def run_items(items):
        n = len(items)
        zs = {}
        states = {}
        for s in range(n + 2):
            if s < n:
                zs[s] = stage_scores(items[s])
            if 0 <= s - 1 < n:
                states[s - 1] = stage_cumsum(items[s - 1], zs.pop(s - 1))
            if 0 <= s - 2 < n:
                stage_values(items[s - 2], states.pop(s - 2))

    def max_carry():
        r = jnp.maximum(aux_s[0], aux_s[1])
        r = jnp.max(jnp.where(lane == 0, r, -jnp.inf), axis=0, keepdims=True)
        return jnp.max(r, axis=1, keepdims=True)[0, 0]

    items = []
    for d in range(nsub - 1, -1, -1):
        kv = load_kv(qi * nsub + d)
        ra = d * bk
        while ra < bq:
            nr = SB_ROW_TILE if (bq - ra) % (2 * SB_ROW_TILE) else 2 * SB_ROW_TILE
            items += [(kv, ra, nr, 0, True), (kv, ra, nr, 1, True)]
            ra += nr
    run_items(items)

    def cond(c):
        kb, rmax = c
        return (kb >= 0) & (rmax > SB_LOG2_CUTOFF)

    def body(c):
        kb, _ = c
        items = []
        for kv in (load_kv(kb), load_kv(kb - 1)):
            for ra in range(0, bq, 2 * SB_ROW_TILE):
                items += [(kv, ra, 2 * SB_ROW_TILE, 0, False), (kv, ra, 2 * SB_ROW_TILE, 1, False)]
        run_items(items)
        return kb - 2, max_carry()

    lax.while_loop(cond, body, (qi * nsub - 1, max_carry()))
    o_ref[0] = jnp.where(lane < 64, acc_s[0], acc_s[1])


def _stick_breaking(h_sb, bq=512, bk=128):
    b, t, _ = h_sb.shape
    npair = SB_WIDTH // 128
    return pl.pallas_call(
        functools.partial(_sb_kernel, bq=bq, bk=bk),
        grid=(b, npair, t // bq),
        in_specs=[
            pl.BlockSpec((1, bq, 128), lambda i, p, j: (i, j, p)),
            pl.BlockSpec((1, t, 128), lambda i, p, j: (i, 0, npair + p)),
            pl.BlockSpec((1, t, 128), lambda i, p, j: (i, 0, 2 * npair + p)),
        ],
        out_specs=pl.BlockSpec((1, bq, 128), lambda i, p, j: (i, j, p)),
        out_shape=jax.ShapeDtypeStruct((b, t, SB_WIDTH), F32),
        scratch_shapes=[pltpu.VMEM((2, bq, 128), F32), pltpu.VMEM((2, bq, bk), F32)],
        compiler_params=pltpu.CompilerParams(
            dimension_semantics=("arbitrary", "arbitrary", "arbitrary"),
            vmem_limit_bytes=VMEM_LIMIT_BYTES),
        name="stickbreak",
    )(h_sb, h_sb, h_sb)


def _merge_kernel(x_ref, oa_ref, ob_ref, oc_ref, gt_ref, wa_ref, wb_ref, wc_ref, wo_ref, o_ref):
    d = D_MODEL
    g = _sigmoid(gt_ref[...])
    m = (g[:, 0:d] * _dot(oa_ref[...], wa_ref[...])
         + g[:, d:2 * d] * _dot(ob_ref[...], wb_ref[...])
         + g[:, 2 * d:3 * d] * _dot(oc_ref[...], wc_ref[...]))
    o_ref[...] = x_ref[...] + _dot(m, wo_ref[...])


def _merge(x2d, oa, ob, oc, gates, wa, wb, wc, wo, tm=512):
    n, d = x2d.shape
    rowspec = lambda w: pl.BlockSpec((tm, w), lambda i: (i, 0))
    full = lambda w: pl.BlockSpec(w.shape, lambda i: (0, 0))
    return pl.pallas_call(
        _merge_kernel,
        grid=(n // tm,),
        in_specs=[rowspec(d), rowspec(512), rowspec(512), rowspec(512), rowspec(3 * d),
                  full(wa), full(wb), full(wc), full(wo)],
        out_specs=rowspec(d),
        out_shape=jax.ShapeDtypeStruct((n, d), F32),
        compiler_params=pltpu.CompilerParams(
            dimension_semantics=("arbitrary",), vmem_limit_bytes=VMEM_LIMIT_BYTES),
        name="merge",
    )(x2d, oa, ob, oc, gates, wa, wb, wc, wo)


def _router_kernel(x_ref, g_ref, rw_ref, comb_ref, idx_ref, cnt_ref, run_s):
    @pl.when(pl.program_id(0) == 0)
    def _():
        run_s[...] = jnp.zeros_like(run_s)

    hn = _rms(x_ref[...], g_ref[...])
    logits = _dot3(hn, rw_ref[...])
    tm = logits.shape[0]
    lane = _iota2(logits.shape, 1)
    m1 = jnp.max(logits, axis=-1, keepdims=True)
    i1 = jnp.min(jnp.where(logits == m1, lane, N_EXPERTS), axis=-1, keepdims=True)
    sel1 = lane == i1
    rest = jnp.where(sel1, -jnp.inf, logits)
    m2 = jnp.max(rest, axis=-1, keepdims=True)
    i2 = jnp.min(jnp.where(rest == m2, lane, N_EXPERTS), axis=-1, keepdims=True)
    sel2 = lane == i2
    e2 = jnp.exp(m2 - m1)
    w1 = 1.0 / (1.0 + e2)
    comb_ref[...] = jnp.where(sel1, w1, 0.0) + jnp.where(sel2, e2 * w1, 0.0)

    member = (sel1 | sel2).astype(BF16)
    before = (_iota2((tm, tm), 0) > _iota2((tm, tm), 1)).astype(BF16)
    rank = jnp.dot(before, member, preferred_element_type=F32) + run_s[...]
    r1 = jnp.sum(jnp.where(sel1, rank, 0.0), axis=-1, keepdims=True).astype(jnp.int32)
    r2 = jnp.sum(jnp.where(sel2, rank, 0.0), axis=-1, keepdims=True).astype(jnp.int32)
    run_s[...] += jnp.sum(member.astype(F32), axis=0, keepdims=True)
    cnt_ref[...] = run_s[...]
    lane128 = _iota2((tm, 128), 1)
    idx_ref[...] = jnp.where(lane128 == 0, i1, jnp.where(lane128 == 1, i2,
                             jnp.where(lane128 == 2, r1, jnp.where(lane128 == 3, r2, 0))))


def _router(x2d, g, rw, tm=512):
    n, d = x2d.shape
    return pl.pallas_call(
        _router_kernel,
        grid=(n // tm,),
        in_specs=[pl.BlockSpec((tm, d), lambda i: (i, 0)), pl.BlockSpec((1, d), lambda i: (0, 0)),
                  pl.BlockSpec(rw.shape, lambda i: (0, 0))],
        out_specs=[pl.BlockSpec((tm, N_EXPERTS), lambda i: (i, 0)),
                   pl.BlockSpec((tm, 128), lambda i: (i, 0)),
                   pl.BlockSpec((1, N_EXPERTS), lambda i: (0, 0))],
        out_shape=[jax.ShapeDtypeStruct((n, N_EXPERTS), F32),
                   jax.ShapeDtypeStruct((n, 128), jnp.int32),
                   jax.ShapeDtypeStruct((1, N_EXPERTS), F32)],
        scratch_shapes=[pltpu.VMEM((1, N_EXPERTS), F32)],
        compiler_params=pltpu.CompilerParams(dimension_semantics=("arbitrary",)),
        name="router",
    )(x2d, g.reshape(1, d), rw.astype(F32))


def _row_copy(src_ref, src_row, dst_ref, dst_row, sem):
    return pltpu.make_async_copy(src_ref.at[pl.ds(src_row, 1), :], dst_ref.at[pl.ds(dst_row, 1), :], sem)


def _drain_rows(n_copies, src_ref, dst_ref, sem):
    def body(t, carry):
        _row_copy(src_ref, 0, dst_ref, 0, sem).wait()
        return carry

    lax.fori_loop(0, n_copies, body, 0, unroll=8)


def _dispatch_kernel(pos1_ref, pos2_ref, x_ref, g_ref, xs_in_ref, xs_ref, hn_s, sems):
    del xs_in_ref
    tm = hn_s.shape[1]
    i = pl.program_id(0)
    slot = i % 2
    hn_s[slot] = _rms(x_ref[...], g_ref[...])
    src = hn_s.at[slot]
    sem = sems.at[slot]

    def issue(t, carry):
        _row_copy(src, t, xs_ref, pos1_ref[t], sem).start()
        _row_copy(src, t, xs_ref, pos2_ref[t], sem).start()
        return carry

    lax.fori_loop(0, tm, issue, 0, unroll=8)

    @pl.when(i > 0)
    def _():
        _drain_rows(2 * tm, hn_s.at[1 - slot], xs_ref, sems.at[1 - slot])

    @pl.when(i == pl.num_programs(0) - 1)
    def _():
        _drain_rows(2 * tm, src, xs_ref, sem)


def _dispatch(x2d, g, pos1, pos2, n_rows, tm=256):
    n, d = x2d.shape
    smem = lambda: pl.BlockSpec((tm,), lambda i: (i,), memory_space=pltpu.SMEM)
    return pl.pallas_call(
        _dispatch_kernel,
        grid=(n // tm,),
        in_specs=[smem(), smem(), pl.BlockSpec((tm, d), lambda i: (i, 0)),
                  pl.BlockSpec((1, d), lambda i: (0, 0)), pl.BlockSpec(memory_space=pl.ANY)],
        out_specs=pl.BlockSpec(memory_space=pl.ANY),
        out_shape=jax.ShapeDtypeStruct((n_rows, d), F32),
        scratch_shapes=[pltpu.VMEM((2, tm, d), F32), pltpu.SemaphoreType.DMA((2,))],
        input_output_aliases={4: 0},
        compiler_params=pltpu.CompilerParams(dimension_semantics=("arbitrary",)),
        name="moe_dispatch",
    )(pos1, pos2, x2d, g.reshape(1, d), jnp.zeros((n_rows, d), F32))


def _grouped_ffn_kernel(te_ref, nt_ref, xs_ref, wg_ref, wu_ref, wd_ref, o_ref, xb_s, acc_s):
    i = pl.program_id(0)
    f = pl.program_id(1)

    @pl.when(f == 0)
    def _():
        xb_s[...] = xs_ref[...].astype(BF16)
        acc_s[...] = jnp.zeros_like(acc_s)

    @pl.when(i < nt_ref[0])
    def _():
        xb = xb_s[...]
        act = _silu(jnp.dot(xb, wg_ref[...], preferred_element_type=F32)) * jnp.dot(
            xb, wu_ref[...], preferred_element_type=F32)
        acc_s[...] += jnp.dot(act.astype(BF16), wd_ref[...], preferred_element_type=F32)

    @pl.when(f == pl.num_programs(1) - 1)
    def _():
        o_ref[...] = acc_s[...]


def _grouped_ffn(xs, tile_expert, n_tiles, wg, wu, wd, tm, tf=1792):
    n_rows, d = xs.shape
    dff = wg.shape[2]
    nf = dff // tf
    fsel = lambda i, f, te, nt: jnp.where(i < nt[0], f, nf - 1)
    return pl.pallas_call(
        _grouped_ffn_kernel,
        grid_spec=pltpu.PrefetchScalarGridSpec(
            num_scalar_prefetch=2,
            grid=(n_rows // tm, nf),
            in_specs=[pl.BlockSpec((tm, d), lambda i, f, te, nt: (i, 0)),
                      pl.BlockSpec((None, d, tf), lambda i, f, te, nt: (te[i], 0, fsel(i, f, te, nt))),
                      pl.BlockSpec((None, d, tf), lambda i, f, te, nt: (te[i], 0, fsel(i, f, te, nt))),
                      pl.BlockSpec((None, tf, d), lambda i, f, te, nt: (te[i], fsel(i, f, te, nt), 0))],
            out_specs=pl.BlockSpec((tm, d), lambda i, f, te, nt: (i, 0)),
            scratch_shapes=[pltpu.VMEM((tm, d), BF16), pltpu.VMEM((tm, d), F32)]),
        out_shape=jax.ShapeDtypeStruct((n_rows, d), F32),
        compiler_params=pltpu.CompilerParams(
            dimension_semantics=("arbitrary", "arbitrary"), vmem_limit_bytes=VMEM_LIMIT_BYTES),
        name="moe_grouped_ffn",
    )(tile_expert, n_tiles, xs, wg, wu, wd)


def _combine_kernel(*refs, final):
    if final:
        pos1_ref, pos2_ref, npos1_ref, npos2_ref, x_ref, comb_ref, fg_ref, ys_ref, o_ref, y1_s, y2_s, sems = refs
    else:
        pos1_ref, pos2_ref, npos1_ref, npos2_ref, x_ref, comb_ref, ys_ref, o_ref, y1_s, y2_s, sems = refs
    tm = y1_s.shape[1]
    i = pl.program_id(0)
    slot = i % 2

    def gather(p1_ref, p2_ref, sl):
        def issue(t, carry):
            _row_copy(ys_ref, p1_ref[t], y1_s.at[sl], t, sems.at[sl]).start()
            _row_copy(ys_ref, p2_ref[t], y2_s.at[sl], t, sems.at[sl]).start()
            return carry

        lax.fori_loop(0, tm, issue, 0, unroll=8)

    @pl.when(i == 0)
    def _():
        gather(pos1_ref, pos2_ref, 0)

    @pl.when(i < pl.num_programs(0) - 1)
    def _():
        gather(npos1_ref, npos2_ref, 1 - slot)

    _drain_rows(2 * tm, ys_ref, y1_s.at[slot], sems.at[slot])
    comb = comb_ref[...]
    w1 = jnp.max(comb, axis=-1, keepdims=True)
    w2 = jnp.sum(comb, axis=-1, keepdims=True) - w1
    out = x_ref[...] + w1 * y1_s[slot] + w2 * y2_s[slot]
    if final:
        out = _rms(out, fg_ref[...])
    o_ref[...] = out


def _combine(x2d, comb, pos1, pos2, ys, final_g=None, tm=256):
    n, d = x2d.shape
    final = final_g is not None
    nsteps = n // tm
    smem = lambda: pl.BlockSpec((tm,), lambda i: (i,), memory_space=pltpu.SMEM)
    smem_next = lambda: pl.BlockSpec((tm,), lambda i: (jnp.minimum(i + 1, nsteps - 1),),
                                     memory_space=pltpu.SMEM)
    in_specs = [smem(), smem(), smem_next(), smem_next(), pl.BlockSpec((tm, d), lambda i: (i, 0)),
                pl.BlockSpec((tm, N_EXPERTS), lambda i: (i, 0))]
    args = [pos1, pos2, pos1, pos2, x2d, comb]
    if final:
        in_specs.append(pl.BlockSpec((1, d), lambda i: (0, 0)))
        args.append(final_g.reshape(1, d))
    in_specs.append(pl.BlockSpec(memory_space=pl.ANY))
    args.append(ys)
    return pl.pallas_call(
        functools.partial(_combine_kernel, final=final),
        grid=(nsteps,),
        in_specs=in_specs,
        out_specs=pl.BlockSpec((tm, d), lambda i: (i, 0)),
        out_shape=jax.ShapeDtypeStruct((n, d), F32),
        scratch_shapes=[pltpu.VMEM((2, tm, d), F32), pltpu.VMEM((2, tm, d), F32),
                        pltpu.SemaphoreType.DMA((2,))],
        compiler_params=pltpu.CompilerParams(dimension_semantics=("arbitrary",)),
        name="moe_combine",
    )(*args)


def _moe(x2d, g, rw, wg, wu, wd, final_g=None, tm=MOE_ROW_TILE):
    n, d = x2d.shape
    comb, idx, cnt = _router(x2d, g, rw)
    counts = cnt[0].astype(jnp.int32)
    padded = ((counts + tm - 1) // tm) * tm
    ends = jnp.cumsum(padded)
    starts = ends - padded
    pos1 = starts[idx[:, 0]] + idx[:, 2]
    pos2 = starts[idx[:, 1]] + idx[:, 3]
    n_rows = 2 * n + N_EXPERTS * tm
    n_tiles = (ends[-1] // tm).reshape(1)
    tile_expert = jnp.minimum(
        jnp.searchsorted(ends // tm, jnp.arange(n_rows // tm, dtype=jnp.int32), side="right"),
        N_EXPERTS - 1).astype(jnp.int32)
    xs = _dispatch(x2d, g, pos1, pos2, n_rows)
    ys = _grouped_ffn(xs, tile_expert, n_tiles, wg, wu, wd, tm)
    return _combine(x2d, comb, pos1, pos2, ys, final_g=final_g)


def _ffn_kernel(*refs, final):
    if final:
        x_ref, g_ref, wg_ref, wu_ref, wd_ref, fg_ref, o_ref, hn_s, acc_s = refs
    else:
        x_ref, g_ref, wg_ref, wu_ref, wd_ref, o_ref, hn_s, acc_s = refs
    f = pl.program_id(1)

    @pl.when(f == 0)
    def _():
        hn_s[...] = _rms(x_ref[...], g_ref[...]).astype(BF16)
        acc_s[...] = jnp.zeros_like(acc_s)

    hn = hn_s[...]
    act = _silu(jnp.dot(hn, wg_ref[...], preferred_element_type=F32)) * jnp.dot(
        hn, wu_ref[...], preferred_element_type=F32)
    acc_s[...] += jnp.dot(act.astype(BF16), wd_ref[...], preferred_element_type=F32)

    @pl.when(f == pl.num_programs(1) - 1)
    def _():
        out = x_ref[...] + acc_s[...]
        if final:
            out = _rms(out, fg_ref[...])
        o_ref[...] = out


def _ffn(x2d, g, wg, wu, wd, final_g=None, tm=1024, tf=256):
    n, d = x2d.shape
    final = final_g is not None
    dff = wg.shape[1]
    xmap = lambda i, f: (i, 0)
    cmap = lambda i, f: (0, 0)
    in_specs = [pl.BlockSpec((tm, d), xmap), pl.BlockSpec((1, d), cmap),
                pl.BlockSpec((d, tf), lambda i, f: (0, f)),
                pl.BlockSpec((d, tf), lambda i, f: (0, f)),
                pl.BlockSpec((tf, d), lambda i, f: (f, 0))]
    args = [x2d, g.reshape(1, d), wg, wu, wd]
    if final:
        in_specs.append(pl.BlockSpec((1, d), cmap))
        args.append(final_g.reshape(1, d))
    return pl.pallas_call(
        functools.partial(_ffn_kernel, final=final),
        grid=(n // tm, dff // tf),
        in_specs=in_specs,
        out_specs=pl.BlockSpec((tm, d), xmap),
        out_shape=jax.ShapeDtypeStruct((n, d), F32),
        scratch_shapes=[pltpu.VMEM((tm, d), BF16), pltpu.VMEM((tm, d), F32)],
        compiler_params=pltpu.CompilerParams(
            dimension_semantics=("arbitrary", "arbitrary"), vmem_limit_bytes=VMEM_LIMIT_BYTES),
        name="dense_ffn",
    )(*args)


def kernel(x, norm_mix_g, w_in, gdn_conv_w, gdn_a_log, gdn_dt_bias, gdn_norm_g, rwkv_mu, rwkv_w0, rwkv_w2, rwkv_a0, rwkv_a2, rwkv_g2, rwkv_k_k, rwkv_k_a, rwkv_r_k, rwkv_ln_g, rwkv_ln_b, w_branch_gdn, w_branch_rwkv, w_branch_sb, w_out, norm_ffn_g, ffn_w_gate, ffn_w_up, ffn_w_down, router_w, moe_w_gate, moe_w_up, moe_w_down, final_norm_g):
    b, t, d = x.shape
    n = b * t
    depth = w_in.shape[0]
    x2 = x.reshape(n, d).astype(F32)
    for layer in range(depth):
        w = w_in[layer]
        w_gdn = jnp.concatenate(
            [w[:, 0:2048], jnp.repeat(w[:, 2048:2052], 128, axis=1), jnp.repeat(w[:, 2052:2056], 128, axis=1)],
            axis=1).astype(BF16)
        w_rwkv = w[:, 2056:3848].astype(BF16)
        w_sb = w[:, 3848:5384].astype(BF16)
        w_gates = w[:, 5384:8456].astype(BF16)
        g_mix = norm_mix_g[layer].astype(F32)
        p_gdn, p_rwkv = _normproj(x2, g_mix, [w_gdn, w_rwkv])
        p_sb, p_gates = _normproj(x2, g_mix, [w_sb, w_gates])
        o_a = _gdn(p_gdn.reshape(b, t, -1), gdn_conv_w[layer], gdn_a_log[layer], gdn_dt_bias[layer],
                   gdn_norm_g[layer])
        o_b = _rwkv(p_rwkv.reshape(b, t, -1), rwkv_mu[layer], rwkv_w0[layer], rwkv_w2[layer],
                    rwkv_a0[layer], rwkv_a2[layer], rwkv_g2[layer], rwkv_k_k[layer], rwkv_k_a[layer],
                    rwkv_r_k[layer].reshape(-1), rwkv_ln_g[layer], rwkv_ln_b[layer])
        o_c = _stick_breaking(p_sb.reshape(b, t, -1))
        x2 = _merge(x2, o_a.reshape(n, -1), o_b.reshape(n, -1), o_c.reshape(n, -1), p_gates,
                    w_branch_gdn[layer].astype(BF16), w_branch_rwkv[layer].astype(BF16),
                    w_branch_sb[layer].astype(BF16), w_out[layer].astype(BF16))
        g_ffn = norm_ffn_g[layer].astype(F32)
        final_g = final_norm_g.astype(F32) if layer == depth - 1 else None
        i = layer // 2
        if layer % 2 == 0:
            x2 = _ffn(x2, g_ffn, ffn_w_gate[i].astype(BF16), ffn_w_up[i].astype(BF16),
                      ffn_w_down[i].astype(BF16), final_g=final_g, tf=256)
        else:
            x2 = _moe(x2, g_ffn, router_w[i], moe_w_gate[i].astype(BF16), moe_w_up[i].astype(BF16),
                      moe_w_down[i].astype(BF16), final_g=final_g)
    return x2.reshape(b, t, d)
```

```python
import functools

import jax
import jax.numpy as jnp
from jax import lax
from jax.experimental import pallas as pl
from jax.experimental.pallas import tpu as pltpu

F32 = jnp.float32
BF16 = jnp.bfloat16

RMS_EPS = 1e-6
GN_EPS = 64e-5
L2_EPS = 1e-6

D_MODEL = 1024
GDN_HEADS = 4
GDN_HEAD_DIM = 128
GDN_WIDTH = 512
RWKV_WIDTH = 512
RWKV_HEAD_DIM = 64
RWKV_IN = 1792
SB_WIDTH = 512
SB_HEAD_DIM = 64
N_EXPERTS = 8
CHUNK = 64
LOG2E = 1.4426950408889634
SB_LOG2_CUTOFF = -160.0
SB_ROW_TILE = 128
GDN_CHUNKS_PER_STEP = 8
RWKV_CHUNKS_PER_STEP = 4
MOE_ROW_TILE = 512
VMEM_LIMIT_BYTES = 56 * 1024 * 1024


def _dot(a, b):
    return jnp.dot(a.astype(BF16), b.astype(BF16), preferred_element_type=F32)


def _dot_nt(a, b):
    return lax.dot_general(a.astype(BF16), b.astype(BF16), (((1,), (1,)), ((), ())),
                           preferred_element_type=F32)


def _split2(a):
    hi = a.astype(BF16)
    lo = (a - hi.astype(F32)).astype(BF16)
    return hi, lo


def _split3(a):
    hi = a.astype(BF16)
    r = a - hi.astype(F32)
    mid = r.astype(BF16)
    lo = (r - mid.astype(F32)).astype(BF16)
    return hi, mid, lo


def _dot3(a, b):
    ah, al = _split2(a)
    bh, bl = _split2(b)
    return (jnp.dot(ah, bh, preferred_element_type=F32)
            + jnp.dot(ah, bl, preferred_element_type=F32)
            + jnp.dot(al, bh, preferred_element_type=F32))


def _dot_exact_lhs(m, x, parts=3):
    xs = _split3(x) if parts == 3 else _split2(x)
    out = jnp.dot(m, xs[0], preferred_element_type=F32)
    for p in xs[1:]:
        out = out + jnp.dot(m, p, preferred_element_type=F32)
    return out


def _dot_exact_rhs(x, m, parts=2):
    xs = _split3(x) if parts == 3 else _split2(x)
    out = jnp.dot(xs[0], m, preferred_element_type=F32)
    for p in xs[1:]:
        out = out + jnp.dot(p, m, preferred_element_type=F32)
    return out


def _iota2(shape, dim):
    return lax.broadcasted_iota(jnp.int32, shape, dim)


def _eye(n, dtype=F32):
    return (_iota2((n, n), 0) == _iota2((n, n), 1)).astype(dtype)


def _softplus(x):
    return jnp.maximum(x, 0.0) + jnp.log(1.0 + jnp.exp(-jnp.abs(x)))


def _sigmoid(x):
    return 1.0 / (1.0 + jnp.exp(-x))


def _silu(x):
    return x * _sigmoid(x)


def _rms(x, g):
    return x * lax.rsqrt(jnp.mean(x * x, axis=-1, keepdims=True) + RMS_EPS) * g


def _nilpotent_inverse(n, eye, dot):
    t = eye + n
    x = n
    for _ in range(5):
        x = dot(x, x)
        t = t + dot(t, x)
    return t


def _nilpotent_inverse_many(ns, eye, dot):
    ts = [eye + n for n in ns]
    xs = list(ns)
    for _ in range(5):
        xs = [dot(x, x) for x in xs]
        ts = [t + dot(t, x) for t, x in zip(ts, xs)]
    return ts


def _chunk_cumsum(x):
    i = _iota2((128, 128), 0)
    j = _iota2((128, 128), 1)
    m = ((j <= i) & ((i >> 6) == (j >> 6))).astype(BF16)
    return jnp.concatenate(
        [_dot_exact_lhs(m, x[r:r + 128]) for r in range(0, x.shape[0], 128)], axis=0)


def _head_sum(x):
    i = _iota2((128, 128), 0)
    j = _iota2((128, 128), 1)
    m = ((i >> 6) == (j >> 6)).astype(BF16)
    return jnp.concatenate(
        [_dot_exact_rhs(x[:, c:c + 128], m) for c in range(0, x.shape[1], 128)], axis=1)


def _normproj_kernel(x_ref, g_ref, *refs, n_out):
    w_refs = refs[:n_out]
    o_refs = refs[n_out:]
    hn = _rms(x_ref[...], g_ref[...]).astype(BF16)
    for w_ref, o_ref in zip(w_refs, o_refs):
        o_ref[...] = jnp.dot(hn, w_ref[...], preferred_element_type=F32).astype(o_ref.dtype)


def _normproj(x2d, g, weights, out_dtypes, tm=256):
    n, d = x2d.shape
    n_out = len(weights)
    in_specs = [pl.BlockSpec((tm, d), lambda i: (i, 0)), pl.BlockSpec((1, d), lambda i: (0, 0))]
    in_specs += [pl.BlockSpec(w.shape, lambda i: (0, 0)) for w in weights]
    out_specs = [pl.BlockSpec((tm, w.shape[1]), lambda i: (i, 0)) for w in weights]
    out_shape = [jax.ShapeDtypeStruct((n, w.shape[1]), dt) for w, dt in zip(weights, out_dtypes)]
    return pl.pallas_call(
        functools.partial(_normproj_kernel, n_out=n_out),
        grid=(n // tm,),
        in_specs=in_specs,
        out_specs=out_specs,
        out_shape=out_shape,
        compiler_params=pltpu.CompilerParams(
            dimension_semantics=("arbitrary",), vmem_limit_bytes=VMEM_LIMIT_BYTES),
        name="normproj",
    )(x2d, g.reshape(1, d), *weights)


def _gdn_kernel(qkv_ref, z_ref, ba_ref, cw_ref, alog_ref, dtb_ref, ng_ref, o_ref,
                ext_s, state_s, q_s, k_s, v_s, beta_s, gc_s, p_s, qq_s, r_s, zz_s, oc_s, *, tt):
    nc = tt // CHUNK
    w3 = 3 * GDN_WIDTH
    t = pl.program_id(1)

    @pl.when(t == 0)
    def _():
        ext_s[0:8, :] = jnp.zeros((8, w3), F32)
        state_s[...] = jnp.zeros_like(state_s)

    raw = qkv_ref[0]
    ext_s[8:8 + tt, :] = raw
    cw = cw_ref[...]
    y = raw * cw[3:4, :]
    for i in range(3):
        y = y + ext_s[5 + i:5 + i + tt, :] * cw[i:i + 1, :]
    ext_s[0:8, :] = raw[tt - 8:tt, :]
    y = _silu(y)

    for h in range(GDN_HEADS):
        sl = slice(128 * h, 128 * h + 128)
        qh = y[:, 128 * h:128 * h + 128]
        kh = y[:, GDN_WIDTH + 128 * h:GDN_WIDTH + 128 * h + 128]
        vh = y[:, 2 * GDN_WIDTH + 128 * h:2 * GDN_WIDTH + 128 * h + 128]
        qh = qh * lax.rsqrt(jnp.sum(qh * qh, axis=-1, keepdims=True) + L2_EPS) * (GDN_HEAD_DIM ** -0.5)
        kh = kh * lax.rsqrt(jnp.sum(kh * kh, axis=-1, keepdims=True) + L2_EPS)
        q_s[:, sl] = qh
        k_s[:, sl] = kh
        v_s[:, sl] = vh

    ec = _iota2((128, 2 * GDN_WIDTH), 0)
    el = _iota2((128, 2 * GDN_WIDTH), 1)
    ba = _dot_exact_rhs(ba_ref[0], (ec == (el >> 7)).astype(BF16), parts=3)
    beta_s[...] = _sigmoid(ba[:, 0:GDN_WIDTH])
    g = -jnp.exp(alog_ref[...]) * _softplus(ba[:, GDN_WIDTH:2 * GDN_WIDTH] + dtb_ref[...])
    gc_s[...] = _chunk_cumsum(g)

    eye64 = _eye(CHUNK)
    eye128 = _eye(128)
    ii = _iota2((CHUNK, CHUNK), 0)
    jj = _iota2((CHUNK, CHUNK), 1)

    def chunk_body(ci, carry):
        probs = [(ci * GDN_CHUNKS_PER_STEP + u, h) for u in range(GDN_CHUNKS_PER_STEP)
                 for h in range(GDN_HEADS)]
        rows = [pl.ds(pl.multiple_of(c * CHUNK, CHUNK), CHUNK) for c, _ in probs]
        lanes = [slice(128 * h, 128 * h + 128) for _, h in probs]
        idx = range(len(probs))
        q = [q_s[rows[i], lanes[i]] for i in idx]
        k = [k_s[rows[i], lanes[i]] for i in idx]
        gcc = [gc_s[rows[i], lanes[i]] for i in idx]
        gl = [g[CHUNK - 1:CHUNK, :] for g in gcc]
        gcr = [g.T[0:CHUNK, :] for g in gcc]
        dec_incl = [jnp.exp(jnp.where(ii >= jj, gcc[i][:, 0:CHUNK] - gcr[i], -jnp.inf)) for i in idx]
        kb = [k[i] * beta_s[rows[i], lanes[i]] for i in idx]
        a_mat = [_dot_nt(kb[i], k[i]) for i in idx]
        attn = [_dot_nt(q[i], k[i]) * dec_incl[i] for i in idx]
        kdt = [(k[i] * jnp.exp(gl[i] - gcc[i])).T for i in idx]
        tinv = _nilpotent_inverse_many(
            [-a_mat[i] * jnp.where(ii > jj, dec_incl[i], 0.0) for i in idx], eye64, _dot)
        u = [_dot(tinv[i], v_s[rows[i], lanes[i]] * beta_s[rows[i], lanes[i]]) for i in idx]
        w = [_dot(tinv[i], kb[i] * jnp.exp(gcc[i])) for i in idx]
        for i, (c, h) in enumerate(probs):
            p_s[c, h] = jnp.exp(gl[i]) * eye128 - _dot(kdt[i], w[i])
        for i, (c, h) in enumerate(probs):
            qq_s[c, h] = _dot(kdt[i], u[i])
        for i, (c, h) in enumerate(probs):
            r_s[c, h] = q[i] * jnp.exp(gcc[i]) - _dot(attn[i], w[i])
        for i, (c, h) in enumerate(probs):
            zz_s[c, h] = _dot(attn[i], u[i])
        return carry

    lax.fori_loop(0, nc // GDN_CHUNKS_PER_STEP, chunk_body, 0)

    def scan_body(c, carry):
        rows = pl.ds(pl.multiple_of(c * CHUNK, CHUNK), CHUNK)
        s = [state_s[h] for h in range(GDN_HEADS)]
        s_new = [_dot3(p_s[c, h], s[h]) for h in range(GDN_HEADS)]
        o = [_dot(r_s[c, h], s[h]) for h in range(GDN_HEADS)]
        for h in range(GDN_HEADS):
            state_s[h] = s_new[h] + qq_s[c, h]
            oc_s[rows, 128 * h:128 * h + 128] = o[h] + zz_s[c, h]
        return carry

    lax.fori_loop(0, nc, scan_body, 0)

    o = oc_s[...]
    z = z_ref[0]
    ng = ng_ref[...]
    for h in range(GDN_HEADS):
        sl = slice(128 * h, 128 * h + 128)
        oh = o[:, sl]
        oh = oh * lax.rsqrt(jnp.mean(oh * oh, axis=-1, keepdims=True) + RMS_EPS) * ng[:, sl]
        o_ref[0, :, sl] = (oh * _silu(z[:, sl])).astype(o_ref.dtype)


def _gdn(proj_a, proj_ba, conv_w, a_log, dt_bias, norm_g, tt=512):
    b, t, _ = proj_a.shape
    nc = tt // CHUNK
    w3 = 3 * GDN_WIDTH
    rep = lambda p: jnp.repeat(p.astype(F32), 128).reshape(1, GDN_WIDTH)
    small = lambda shape: pl.BlockSpec(shape, lambda i, j: (0, 0))
    return pl.pallas_call(
        functools.partial(_gdn_kernel, tt=tt),
        grid=(b, t // tt),
        in_specs=[
            pl.BlockSpec((1, tt, w3), lambda i, j: (i, j, 0)),
            pl.BlockSpec((1, tt, GDN_WIDTH), lambda i, j: (i, j, 3)),
            pl.BlockSpec((1, tt, 128), lambda i, j: (i, j, 0)),
            small((4, w3)), small((1, GDN_WIDTH)), small((1, GDN_WIDTH)), small((1, GDN_WIDTH)),
        ],
        out_specs=pl.BlockSpec((1, tt, GDN_WIDTH), lambda i, j: (i, j, 0)),
        out_shape=jax.ShapeDtypeStruct((b, t, GDN_WIDTH), BF16),
        scratch_shapes=[
            pltpu.VMEM((tt + 8, w3), F32),
            pltpu.VMEM((GDN_HEADS, 128, 128), F32),
            pltpu.VMEM((tt, GDN_WIDTH), F32), pltpu.VMEM((tt, GDN_WIDTH), F32),
            pltpu.VMEM((tt, GDN_WIDTH), F32), pltpu.VMEM((tt, GDN_WIDTH), F32),
            pltpu.VMEM((tt, GDN_WIDTH), F32),
            pltpu.VMEM((nc, GDN_HEADS, 128, 128), F32), pltpu.VMEM((nc, GDN_HEADS, 128, 128), F32),
            pltpu.VMEM((nc, GDN_HEADS, CHUNK, 128), F32), pltpu.VMEM((nc, GDN_HEADS, CHUNK, 128), F32),
            pltpu.VMEM((tt, GDN_WIDTH), F32),
        ],
        compiler_params=pltpu.CompilerParams(
            dimension_semantics=("arbitrary", "arbitrary"), vmem_limit_bytes=VMEM_LIMIT_BYTES),
        name="gdn",
    )(proj_a, proj_a, proj_ba, conv_w.astype(F32), rep(a_log), rep(dt_bias),
      jnp.tile(norm_g.astype(F32), GDN_HEADS).reshape(1, GDN_WIDTH))


def _rwkv_kernel(h_ref, mu_ref, w0_ref, a0_ref, kk_ref, ka_ref, rk_ref, lng_ref, lnb_ref,
                 wa_ref, g2_ref, o_ref,
                 ext_s, state_s, r_s, kn_s, k2_s, v_s, a_s, lw_s, lc_s, p_s, qq_s, rh_s, yc_s, y_s,
                 *, tt):
    nc = tt // CHUNK
    npair = RWKV_WIDTH // 128
    t = pl.program_id(1)

    @pl.when(t == 0)
    def _():
        ext_s[0:8, :] = jnp.zeros((8, RWKV_IN), F32)
        state_s[...] = jnp.zeros_like(state_s)

    raw = h_ref[0]
    ext_s[8:8 + tt, :] = raw
    prev = ext_s[7:7 + tt, :]
    ext_s[0:8, :] = raw[tt - 8:tt, :]
    hl = raw + (prev - raw) * mu_ref[...]
    r = hl[:, 0:512]
    k = hl[:, 512:1024]
    v = hl[:, 1024:1536]
    xwa = hl[:, 1536:1664]
    xg = hl[:, 1664:1792]
    lane128 = _iota2((1, 128), 1)
    xwa = jnp.where(lane128 < 64, jnp.tanh(xwa), xwa)
    lora = _dot3(xwa, wa_ref[...])
    w_log = -_softplus(-(w0_ref[...] + lora[:, 0:512])) - 0.5
    lw = -jnp.exp(w_log)
    a = _sigmoid(a0_ref[...] + lora[:, 512:1024])
    gate = _dot3(_sigmoid(xg), g2_ref[...])
    kk = k * kk_ref[...]
    k2 = k * (1.0 + (a - 1.0) * ka_ref[...])
    kn = kk * lax.rsqrt(_head_sum(kk * kk) + L2_EPS)
    r_s[...] = r
    kn_s[...] = kn
    k2_s[...] = k2
    v_s[...] = v
    a_s[...] = a
    lw_s[...] = lw
    lc_s[...] = _chunk_cumsum(lw)

    eye64 = _eye(CHUNK)
    eye128 = _eye(128)
    ii = _iota2((CHUNK, CHUNK), 0)
    jj = _iota2((CHUNK, CHUNK), 1)
    strict = ii > jj
    incl = ii >= jj
    lane = _iota2((CHUNK, 128), 1)
    bi = _iota2((128, 128), 0)
    bj = _iota2((128, 128), 1)
    blockdiag = (bi < 64) == (bj < 64)

    def chunk_body(ci, carry):
        units = [(ci * RWKV_CHUNKS_PER_STEP + u, p) for u in range(RWKV_CHUNKS_PER_STEP) for p in range(npair)]
        nu = range(len(units))
        rows = [pl.ds(pl.multiple_of(c * CHUNK, CHUNK), CHUNK) for c, _ in units]
        sls = [slice(128 * p, 128 * p + 128) for _, p in units]
        lcc = [lc_s[rows[u], sls[u]] for u in nu]
        vc = [v_s[rows[u], sls[u]] for u in nu]
        lcl = [x[CHUNK - 1:CHUNK, :] for x in lcc]
        ginv = [jnp.exp(-x) for x in lcc]
        gend = [jnp.exp(lcl[u] - lcc[u]) for u in nu]
        kna = [kn_s[rows[u], sls[u]] * a_s[rows[u], sls[u]] for u in nu]
        at = [-kn_s[rows[u], sls[u]] * jnp.exp(lcc[u] - lw_s[rows[u], sls[u]]) for u in nu]
        bt = [kna[u] * ginv[u] for u in nu]
        kt = [k2_s[rows[u], sls[u]] * ginv[u] for u in nu]
        rt = [r_s[rows[u], sls[u]] * jnp.exp(lcc[u]) for u in nu]
        probs = [(u, j) for u in nu for j in range(2)]
        npr = range(len(probs))
        masks = [(lane < 64) if j == 0 else (lane >= 64) for _, j in probs]
        at_m = [jnp.where(masks[i], at[u], 0.0) for i, (u, _) in enumerate(probs)]
        rt_m = [jnp.where(masks[i], rt[u], 0.0) for i, (u, _) in enumerate(probs)]
        ar = [jnp.concatenate([at_m[i], rt_m[i]], axis=0) for i in npr]
        xb = [_dot_nt(ar[i], bt[u]) for i, (u, _) in enumerate(probs)]
        xk = [_dot_nt(ar[i], kt[u]) for i, (u, _) in enumerate(probs)]
        a_ab = [jnp.where(strict, x[0:CHUNK], 0.0) for x in xb]
        a_rb = [jnp.where(incl, x[CHUNK:2 * CHUNK], 0.0) for x in xb]
        a_ak = [jnp.where(strict, x[0:CHUNK], 0.0) for x in xk]
        a_rk = [jnp.where(incl, x[CHUNK:2 * CHUNK], 0.0) for x in xk]
        akv = [_dot(a_ak[i], vc[u]) for i, (u, _) in enumerate(probs)]
        arkv = [_dot(a_rk[i], vc[u]) for i, (u, _) in enumerate(probs)]
        bbt = [(kna[u] * gend[u]).T for u in nu]
        kbt = [(k2_s[rows[u], sls[u]] * gend[u]).T for u in nu]
        tinv = _nilpotent_inverse_many(a_ab, eye64, _dot)
        au = [_dot(tinv[i], jnp.concatenate([at_m[i], akv[i]], axis=1)) for i in npr]
        ry = [_dot(a_rb[i], au[i]) for i in npr]
        kv = [_dot(kbt[u], vc[u]) for u in nu]
        pq = []
        for u, (c, p) in enumerate(units):
            i0, i1 = 2 * u, 2 * u + 1
            ahat = au[i0][:, 0:128] + au[i1][:, 0:128]
            uv = jnp.where(lane < 64, au[i0][:, 128:256], au[i1][:, 128:256])
            rh_s[c, p] = rt_m[i0] + rt_m[i1] + ry[i0][:, 0:128] + ry[i1][:, 0:128]
            yc_s[c, p] = jnp.where(lane < 64, ry[i0][:, 128:256] + arkv[i0], ry[i1][:, 128:256] + arkv[i1])
            pq.append(_dot(bbt[u], jnp.concatenate([ahat, uv], axis=1)))
        for u, (c, p) in enumerate(units):
            p_s[c, p] = jnp.where(blockdiag, pq[u][:, 0:128], 0.0) + eye128 * jnp.exp(lcl[u])
            qq_s[c, p] = jnp.where(blockdiag, pq[u][:, 128:256] + kv[u], 0.0)
        return carry

    lax.fori_loop(0, nc // RWKV_CHUNKS_PER_STEP, chunk_body, 0)

    def scan_body(c, carry):
        rows = pl.ds(pl.multiple_of(c * CHUNK, CHUNK), CHUNK)
        s = [state_s[p] for p in range(npair)]
        s_new = [_dot3(p_s[c, p], s[p]) for p in range(npair)]
        y = [_dot(rh_s[c, p], s[p]) for p in range(npair)]
        for p in range(npair):
            state_s[p] = s_new[p] + qq_s[c, p]
            y_s[rows, 128 * p:128 * p + 128] = y[p] + yc_s[c, p]
        return carry

    lax.fori_loop(0, nc, scan_body, 0)

    y = y_s[...]
    mean = _head_sum(y) * (1.0 / RWKV_HEAD_DIM)
    yc = y - mean
    var = _head_sum(yc * yc) * (1.0 / RWKV_HEAD_DIM)
    yn = yc * lax.rsqrt(var + GN_EPS) * lng_ref[...] + lnb_ref[...]
    bonus = _head_sum(r * k2 * rk_ref[...]) * v
    o_ref[0] = ((yn + bonus) * gate).astype(o_ref.dtype)


def _dot3_nt(a, b):
    ah, al = _split2(a)
    bh, bl = _split2(b)
    dn = (((1,), (1,)), ((), ()))
    return (lax.dot_general(ah, bh, dn, preferred_element_type=F32)
            + lax.dot_general(ah, bl, dn, preferred_element_type=F32)
            + lax.dot_general(al, bh, dn, preferred_element_type=F32))


def _rwkv(h, mu, w0, w2, a0, a2, g2, k_k, k_a, r_k, ln_g, ln_b, tt=512):
    b, t, _ = h.shape
    nc = tt // CHUNK
    npair = RWKV_WIDTH // 128
    row = lambda p: p.astype(F32).reshape(1, -1)
    wa = jnp.zeros((128, 2 * RWKV_WIDTH), F32)
    wa = wa.at[0:64, 0:RWKV_WIDTH].set(w2.astype(F32)).at[64:128, RWKV_WIDTH:].set(a2.astype(F32))
    small = lambda shape: pl.BlockSpec(shape, lambda i, j: (0, 0))
    vec = small((1, RWKV_WIDTH))
    return pl.pallas_call(
        functools.partial(_rwkv_kernel, tt=tt),
        grid=(b, t // tt),
        in_specs=[pl.BlockSpec((1, tt, RWKV_IN), lambda i, j: (i, j, 0)), small((1, RWKV_IN)),
                  vec, vec, vec, vec, vec, vec, vec,
                  small((128, 2 * RWKV_WIDTH)), small((128, RWKV_WIDTH))],
        out_specs=pl.BlockSpec((1, tt, RWKV_WIDTH), lambda i, j: (i, j, 0)),
        out_shape=jax.ShapeDtypeStruct((b, t, RWKV_WIDTH), BF16),
        scratch_shapes=[
            pltpu.VMEM((tt + 8, RWKV_IN), F32),
            pltpu.VMEM((npair, 128, 128), F32),
        ] + [pltpu.VMEM((tt, RWKV_WIDTH), F32)] * 7 + [
            pltpu.VMEM((nc, npair, 128, 128), F32), pltpu.VMEM((nc, npair, 128, 128), F32),
            pltpu.VMEM((nc, npair, CHUNK, 128), F32), pltpu.VMEM((nc, npair, CHUNK, 128), F32),
            pltpu.VMEM((tt, RWKV_WIDTH), F32),
        ],
        compiler_params=pltpu.CompilerParams(
            dimension_semantics=("arbitrary", "arbitrary"), vmem_limit_bytes=VMEM_LIMIT_BYTES),
        name="rwkv7",
    )(h, row(mu), row(w0), row(a0), row(k_k), row(k_a), row(r_k), row(ln_g), row(ln_b),
      wa, g2.astype(F32))


def _sb_kernel(q_ref, k_ref, v_ref, o_ref, acc_s, aux_s, *, bq, bk):
    qi = pl.program_id(2)
    nsub = bq // bk
    q = q_ref[0] * (SB_HEAD_DIM ** -0.5 * LOG2E)
    lane = _iota2((1, 128), 1)
    qm = [jnp.where(lane < 64, q, 0.0).astype(BF16), jnp.where(lane >= 64, q, 0.0).astype(BF16)]
    ti = _iota2((2 * bk, bk), 0)
    tj = _iota2((2 * bk, bk), 1)
    cum_mat = ((ti & (bk - 1)) > tj).astype(BF16)
    acc_s[...] = jnp.zeros_like(acc_s)
    aux_s[...] = jnp.zeros_like(aux_s)

    def load_kv(kb):
        start = pl.multiple_of(kb * bk, bk)
        return (start, k_ref[0, pl.ds(start, bk), :].astype(BF16), v_ref[0, pl.ds(start, bk), :].astype(BF16))

    def stage_scores(item):
        (start, k, v), ra, nr, j, masked = item
        z = lax.dot_general(qm[j][ra:ra + nr], k, (((1,), (1,)), ((), ())), preferred_element_type=F32)
        return z

    def stage_cumsum(item, z):
        (start, k, v), ra, nr, j, masked = item
        rows = slice(ra, ra + nr)
        neg_abs = lax.bitcast_convert_type(
            lax.bitcast_convert_type(z, jnp.uint32) | jnp.uint32(0x80000000), F32)
        lsig = jnp.minimum(z, 0.0) - jnp.log(1.0 + jnp.exp2(neg_abs)) * LOG2E
        l1 = lsig - z
        msk = None
        if masked:
            msk = (start + _iota2((nr, bk), 1)) < (qi * bq + ra + _iota2((nr, bk), 0))
            l1 = jnp.where(msk, l1, 0.0)
        l1_hi = l1.astype(BF16)
        l1_lo = (l1 - l1_hi.astype(F32)).astype(BF16)
        cr = (jnp.dot(jnp.concatenate([l1_hi, l1_lo], axis=1), cum_mat, preferred_element_type=F32)
              + aux_s[j, rows, 0:1])
        aux_s[j, rows, :] = cr + l1
        return lsig, cr, msk

    def stage_values(item, state):
        (start, k, v), ra, nr, j, masked = item
        lsig, cr, msk = state
        att = jnp.exp2(lsig + cr)
        if masked:
            att = jnp.where(msk, att, 0.0)
        acc_s[j, ra:ra + nr, :] += jnp.dot(att.astype(BF16), v, preferred_element_type=F32)

    def run_items(items):
        n = len(items)
        zs = {}
        states = {}
        for s in range(n + 2):
            if s < n:
                zs[s] = stage_scores(items[s])
            if 0 <= s - 1 < n:
                states[s - 1] = stage_cumsum(items[s - 1], zs.pop(s - 1))
            if 0 <= s - 2 < n:
                stage_values(items[s - 2], states.pop(s - 2))

    def max_carry():
        r = jnp.maximum(aux_s[0], aux_s[1])
        r = jnp.max(jnp.where(lane == 0, r, -jnp.inf), axis=0, keepdims=True)
        return jnp.max(r, axis=1, keepdims=True)[0, 0]

    items = []
    for d in range(nsub - 1, -1, -1):
        kv = load_kv(qi * nsub + d)
        ra = d * bk
        while ra < bq:
            nr = SB_ROW_TILE if (bq - ra) % (2 * SB_ROW_TILE) else 2 * SB_ROW_TILE
            items += [(kv, ra, nr, 0, True), (kv, ra, nr, 1, True)]
            ra += nr
    run_items(items)

    def cond(c):
        kb, rmax = c
        return (kb >= 0) & (rmax > SB_LOG2_CUTOFF)

    def body(c):
        kb, _ = c
        items = []
        for kv in (load_kv(kb), load_kv(kb - 1)):
            for ra in range(0, bq, 2 * SB_ROW_TILE):
                items += [(kv, ra, 2 * SB_ROW_TILE, 0, False), (kv, ra, 2 * SB_ROW_TILE, 1, False)]
        run_items(items)
        return kb - 2, max_carry()

    lax.while_loop(cond, body, (qi * nsub - 1, max_carry()))
    o_ref[0] = jnp.where(lane < 64, acc_s[0], acc_s[1]).astype(o_ref.dtype)


def _stick_breaking(h_sb, bq=512, bk=128):
    b, t, _ = h_sb.shape
    npair = SB_WIDTH // 128
    return pl.pallas_call(
        functools.partial(_sb_kernel, bq=bq, bk=bk),
        grid=(b, npair, t // bq),
        in_specs=[
            pl.BlockSpec((1, bq, 128), lambda i, p, j: (i, j, p)),
            pl.BlockSpec((1, t, 128), lambda i, p, j: (i, 0, npair + p)),
            pl.BlockSpec((1, t, 128), lambda i, p, j: (i, 0, 2 * npair + p)),
        ],
        out_specs=pl.BlockSpec((1, bq, 128), lambda i, p, j: (i, j, p)),
        out_shape=jax.ShapeDtypeStruct((b, t, SB_WIDTH), BF16),
        scratch_shapes=[pltpu.VMEM((2, bq, 128), F32), pltpu.VMEM((2, bq, bk), F32)],
        compiler_params=pltpu.CompilerParams(
            dimension_semantics=("arbitrary", "arbitrary", "arbitrary"),
            vmem_limit_bytes=VMEM_LIMIT_BYTES),
        name="stickbreak",
    )(h_sb, h_sb, h_sb)


def _merge_kernel(x_ref, oa_ref, ob_ref, oc_ref, gt_ref, wa_ref, wb_ref, wc_ref, wo_ref, o_ref):
    d = D_MODEL
    g = _sigmoid(gt_ref[...].astype(F32))
    m = (g[:, 0:d] * _dot(oa_ref[...], wa_ref[...])
         + g[:, d:2 * d] * _dot(ob_ref[...], wb_ref[...])
         + g[:, 2 * d:3 * d] * _dot(oc_ref[...], wc_ref[...]))
    o_ref[...] = x_ref[...] + _dot(m, wo_ref[...])


def _merge(x2d, oa, ob, oc, gates, wa, wb, wc, wo, tm=512):
    n, d = x2d.shape
    rowspec = lambda w: pl.BlockSpec((tm, w), lambda i: (i, 0))
    full = lambda w: pl.BlockSpec(w.shape, lambda i: (0, 0))
    return pl.pallas_call(
        _merge_kernel,
        grid=(n // tm,),
        in_specs=[rowspec(d), rowspec(512), rowspec(512), rowspec(512), rowspec(3 * d),
                  full(wa), full(wb), full(wc), full(wo)],
        out_specs=rowspec(d),
        out_shape=jax.ShapeDtypeStruct((n, d), F32),
        compiler_params=pltpu.CompilerParams(
            dimension_semantics=("arbitrary",), vmem_limit_bytes=VMEM_LIMIT_BYTES),
        name="merge",
    )(x2d, oa, ob, oc, gates, wa, wb, wc, wo)


def _router_kernel(x_ref, g_ref, rw_ref, comb_ref, idx_ref, cnt_ref, run_s):
    @pl.when(pl.program_id(0) == 0)
    def _():
        run_s[...] = jnp.zeros_like(run_s)

    hn = _rms(x_ref[...], g_ref[...])
    logits = _dot3(hn, rw_ref[...])
    tm = logits.shape[0]
    lane = _iota2(logits.shape, 1)
    m1 = jnp.max(logits, axis=-1, keepdims=True)
    i1 = jnp.min(jnp.where(logits == m1, lane, N_EXPERTS), axis=-1, keepdims=True)
    sel1 = lane == i1
    rest = jnp.where(sel1, -jnp.inf, logits)
    m2 = jnp.max(rest, axis=-1, keepdims=True)
    i2 = jnp.min(jnp.where(rest == m2, lane, N_EXPERTS), axis=-1, keepdims=True)
    sel2 = lane == i2
    e2 = jnp.exp(m2 - m1)
    w1 = 1.0 / (1.0 + e2)
    comb_ref[...] = jnp.where(sel1, w1, 0.0) + jnp.where(sel2, e2 * w1, 0.0)

    member = (sel1 | sel2).astype(BF16)
    before = (_iota2((tm, tm), 0) > _iota2((tm, tm), 1)).astype(BF16)
    rank = jnp.dot(before, member, preferred_element_type=F32) + run_s[...]
    r1 = jnp.sum(jnp.where(sel1, rank, 0.0), axis=-1, keepdims=True).astype(jnp.int32)
    r2 = jnp.sum(jnp.where(sel2, rank, 0.0), axis=-1, keepdims=True).astype(jnp.int32)
    run_s[...] += jnp.sum(member.astype(F32), axis=0, keepdims=True)
    cnt_ref[...] = run_s[...]
    lane128 = _iota2((tm, 128), 1)
    idx_ref[...] = jnp.where(lane128 == 0, i1, jnp.where(lane128 == 1, i2,
                             jnp.where(lane128 == 2, r1, jnp.where(lane128 == 3, r2, 0))))


def _router(x2d, g, rw, tm=512):
    n, d = x2d.shape
    return pl.pallas_call(
        _router_kernel,
        grid=(n // tm,),
        in_specs=[pl.BlockSpec((tm, d), lambda i: (i, 0)), pl.BlockSpec((1, d), lambda i: (0, 0)),
                  pl.BlockSpec(rw.shape, lambda i: (0, 0))],
        out_specs=[pl.BlockSpec((tm, N_EXPERTS), lambda i: (i, 0)),
                   pl.BlockSpec((tm, 128), lambda i: (i, 0)),
                   pl.BlockSpec((1, N_EXPERTS), lambda i: (0, 0))],
        out_shape=[jax.ShapeDtypeStruct((n, N_EXPERTS), F32),
                   jax.ShapeDtypeStruct((n, 128), jnp.int32),
                   jax.ShapeDtypeStruct((1, N_EXPERTS), F32)],
        scratch_shapes=[pltpu.VMEM((1, N_EXPERTS), F32)],
        compiler_params=pltpu.CompilerParams(dimension_semantics=("arbitrary",)),
        name="router",
    )(x2d, g.reshape(1, d), rw.astype(F32))


def _row_copy(src_ref, src_row, dst_ref, dst_row, sem):
    return pltpu.make_async_copy(src_ref.at[pl.ds(src_row, 1), :], dst_ref.at[pl.ds(dst_row, 1), :], sem)


def _drain_rows(n_copies, src_ref, dst_ref, sem):
    def body(t, carry):
        _row_copy(src_ref, 0, dst_ref, 0, sem).wait()
        return carry

    lax.fori_loop(0, n_copies, body, 0, unroll=8)


def _dispatch_kernel(pos1_ref, pos2_ref, x_ref, g_ref, xs_in_ref, xs_ref, hn_s, sems):
    del xs_in_ref
    tm = hn_s.shape[1]
    i = pl.program_id(0)
    slot = i % 2
    hn_s[slot] = _rms(x_ref[...], g_ref[...])
    src = hn_s.at[slot]
    sem = sems.at[slot]

    def issue(t, carry):
        _row_copy(src, t, xs_ref, pos1_ref[t], sem).start()
        _row_copy(src, t, xs_ref, pos2_ref[t], sem).start()
        return carry

    lax.fori_loop(0, tm, issue, 0, unroll=8)

    @pl.when(i > 0)
    def _():
        _drain_rows(2 * tm, hn_s.at[1 - slot], xs_ref, sems.at[1 - slot])

    @pl.when(i == pl.num_programs(0) - 1)
    def _():
        _drain_rows(2 * tm, src, xs_ref, sem)


def _dispatch(x2d, g, pos1, pos2, n_rows, tm=256):
    n, d = x2d.shape
    smem = lambda: pl.BlockSpec((tm,), lambda i: (i,), memory_space=pltpu.SMEM)
    return pl.pallas_call(
        _dispatch_kernel,
        grid=(n // tm,),
        in_specs=[smem(), smem(), pl.BlockSpec((tm, d), lambda i: (i, 0)),
                  pl.BlockSpec((1, d), lambda i: (0, 0)), pl.BlockSpec(memory_space=pl.ANY)],
        out_specs=pl.BlockSpec(memory_space=pl.ANY),
        out_shape=jax.ShapeDtypeStruct((n_rows, d), F32),
        scratch_shapes=[pltpu.VMEM((2, tm, d), F32), pltpu.SemaphoreType.DMA((2,))],
        input_output_aliases={4: 0},
        compiler_params=pltpu.CompilerParams(dimension_semantics=("arbitrary",)),
        name="moe_dispatch",
    )(pos1, pos2, x2d, g.reshape(1, d), jnp.zeros((n_rows, d), F32))


def _grouped_ffn_kernel(te_ref, nt_ref, xs_ref, wg_ref, wu_ref, wd_ref, o_ref, xb_s, acc_s):
    i = pl.program_id(0)
    f = pl.program_id(1)

    @pl.when(f == 0)
    def _():
        xb_s[...] = xs_ref[...].astype(BF16)
        acc_s[...] = jnp.zeros_like(acc_s)

    @pl.when(i < nt_ref[0])
    def _():
        xb = xb_s[...]
        act = _silu(jnp.dot(xb, wg_ref[...], preferred_element_type=F32)) * jnp.dot(
            xb, wu_ref[...], preferred_element_type=F32)
        acc_s[...] += jnp.dot(act.astype(BF16), wd_ref[...], preferred_element_type=F32)

    @pl.when(f == pl.num_programs(1) - 1)
    def _():
        o_ref[...] = acc_s[...]


def _grouped_ffn(xs, tile_expert, n_tiles, wg, wu, wd, tm, tf=1792):
    n_rows, d = xs.shape
    dff = wg.shape[2]
    nf = dff // tf
    fsel = lambda i, f, te, nt: jnp.where(i < nt[0], f, nf - 1)
    return pl.pallas_call(
        _grouped_ffn_kernel,
        grid_spec=pltpu.PrefetchScalarGridSpec(
            num_scalar_prefetch=2,
            grid=(n_rows // tm, nf),
            in_specs=[pl.BlockSpec((tm, d), lambda i, f, te, nt: (i, 0)),
                      pl.BlockSpec((None, d, tf), lambda i, f, te, nt: (te[i], 0, fsel(i, f, te, nt))),
                      pl.BlockSpec((None, d, tf), lambda i, f, te, nt: (te[i], 0, fsel(i, f, te, nt))),
                      pl.BlockSpec((None, tf, d), lambda i, f, te, nt: (te[i], fsel(i, f, te, nt), 0))],
            out_specs=pl.BlockSpec((tm, d), lambda i, f, te, nt: (i, 0)),
            scratch_shapes=[pltpu.VMEM((tm, d), BF16), pltpu.VMEM((tm, d), F32)]),
        out_shape=jax.ShapeDtypeStruct((n_rows, d), F32),
        compiler_params=pltpu.CompilerParams(
            dimension_semantics=("arbitrary", "arbitrary"), vmem_limit_bytes=VMEM_LIMIT_BYTES),
        name="moe_grouped_ffn",
    )(tile_expert, n_tiles, xs, wg, wu, wd)


def _combine_kernel(*refs, final):
    if final:
        pos1_ref, pos2_ref, npos1_ref, npos2_ref, x_ref, comb_ref, fg_ref, ys_ref, o_ref, y1_s, y2_s, sems = refs
    else:
        pos1_ref, pos2_ref, npos1_ref, npos2_ref, x_ref, comb_ref, ys_ref, o_ref, y1_s, y2_s, sems = refs
    tm = y1_s.shape[1]
    i = pl.program_id(0)
    slot = i % 2

    def gather(p1_ref, p2_ref, sl):
        def issue(t, carry):
            _row_copy(ys_ref, p1_ref[t], y1_s.at[sl], t, sems.at[sl]).start()
            _row_copy(ys_ref, p2_ref[t], y2_s.at[sl], t, sems.at[sl]).start()
            return carry

        lax.fori_loop(0, tm, issue, 0, unroll=8)

    @pl.when(i == 0)
    def _():
        gather(pos1_ref, pos2_ref, 0)

    @pl.when(i < pl.num_programs(0) - 1)
    def _():
        gather(npos1_ref, npos2_ref, 1 - slot)

    _drain_rows(2 * tm, ys_ref, y1_s.at[slot], sems.at[slot])
    comb = comb_ref[...]
    w1 = jnp.max(comb, axis=-1, keepdims=True)
    w2 = jnp.sum(comb, axis=-1, keepdims=True) - w1
    out = x_ref[...] + w1 * y1_s[slot] + w2 * y2_s[slot]
    if final:
        out = _rms(out, fg_ref[...])
    o_ref[...] = out


def _combine(x2d, comb, pos1, pos2, ys, final_g=None, tm=256):
    n, d = x2d.shape
    final = final_g is not None
    nsteps = n // tm
    smem = lambda: pl.BlockSpec((tm,), lambda i: (i,), memory_space=pltpu.SMEM)
    smem_next = lambda: pl.BlockSpec((tm,), lambda i: (jnp.minimum(i + 1, nsteps - 1),),
                                     memory_space=pltpu.SMEM)
    in_specs = [smem(), smem(), smem_next(), smem_next(), pl.BlockSpec((tm, d), lambda i: (i, 0)),
                pl.BlockSpec((tm, N_EXPERTS), lambda i: (i, 0))]
    args = [pos1, pos2, pos1, pos2, x2d, comb]
    if final:
        in_specs.append(pl.BlockSpec((1, d), lambda i: (0, 0)))
        args.append(final_g.reshape(1, d))
    in_specs.append(pl.BlockSpec(memory_space=pl.ANY))
    args.append(ys)
    return pl.pallas_call(
        functools.partial(_combine_kernel, final=final),
        grid=(nsteps,),
        in_specs=in_specs,
        out_specs=pl.BlockSpec((tm, d), lambda i: (i, 0)),
        out_shape=jax.ShapeDtypeStruct((n, d), F32),
        scratch_shapes=[pltpu.VMEM((2, tm, d), F32), pltpu.VMEM((2, tm, d), F32),
                        pltpu.SemaphoreType.DMA((2,))],
        compiler_params=pltpu.CompilerParams(dimension_semantics=("arbitrary",)),
        name="moe_combine",
    )(*args)


def _moe(x2d, g, rw, wg, wu, wd, final_g=None, tm=MOE_ROW_TILE):
    n, d = x2d.shape
    comb, idx, cnt = _router(x2d, g, rw)
    counts = cnt[0].astype(jnp.int32)
    padded = ((counts + tm - 1) // tm) * tm
    ends = jnp.cumsum(padded)
    starts = ends - padded
    pos1 = starts[idx[:, 0]] + idx[:, 2]
    pos2 = starts[idx[:, 1]] + idx[:, 3]
    n_rows = 2 * n + N_EXPERTS * tm
    n_tiles = (ends[-1] // tm).reshape(1)
    tile_expert = jnp.minimum(
        jnp.searchsorted(ends // tm, jnp.arange(n_rows // tm, dtype=jnp.int32), side="right"),
        N_EXPERTS - 1).astype(jnp.int32)
    xs = _dispatch(x2d, g, pos1, pos2, n_rows)
    ys = _grouped_ffn(xs, tile_expert, n_tiles, wg, wu, wd, tm)
    return _combine(x2d, comb, pos1, pos2, ys, final_g=final_g)


def _ffn_kernel(*refs, final):
    if final:
        x_ref, g_ref, wg_ref, wu_ref, wd_ref, fg_ref, o_ref = refs
    else:
        x_ref, g_ref, wg_ref, wu_ref, wd_ref, o_ref = refs
    x = x_ref[...]
    hn = _rms(x, g_ref[...]).astype(BF16)
    act = _silu(jnp.dot(hn, wg_ref[...], preferred_element_type=F32)) * jnp.dot(
        hn, wu_ref[...], preferred_element_type=F32)
    out = x + jnp.dot(act.astype(BF16), wd_ref[...], preferred_element_type=F32)
    if final:
        out = _rms(out, fg_ref[...])
    o_ref[...] = out


def _ffn(x2d, g, wg, wu, wd, final_g=None, tm=512):
    n, d = x2d.shape
    final = final_g is not None
    xmap = lambda i: (i, 0)
    cmap = lambda i: (0, 0)
    resident = lambda w: pl.BlockSpec(w.shape, cmap, pipeline_mode=pl.Buffered(1))
    in_specs = [pl.BlockSpec((tm, d), xmap), pl.BlockSpec((1, d), cmap), resident(wg), resident(wu), resident(wd)]
    args = [x2d, g.reshape(1, d), wg, wu, wd]
    if final:
        in_specs.append(pl.BlockSpec((1, d), cmap))
        args.append(final_g.reshape(1, d))
    return pl.pallas_call(
        functools.partial(_ffn_kernel, final=final),
        grid=(n // tm,),
        in_specs=in_specs,
        out_specs=pl.BlockSpec((tm, d), xmap),
        out_shape=jax.ShapeDtypeStruct((n, d), F32),
        compiler_params=pltpu.CompilerParams(
            dimension_semantics=("arbitrary",), vmem_limit_bytes=VMEM_LIMIT_BYTES),
        name="dense_ffn",
    )(*args)


def kernel(x, norm_mix_g, w_in, gdn_conv_w, gdn_a_log, gdn_dt_bias, gdn_norm_g, rwkv_mu, rwkv_w0, rwkv_w2, rwkv_a0, rwkv_a2, rwkv_g2, rwkv_k_k, rwkv_k_a, rwkv_r_k, rwkv_ln_g, rwkv_ln_b, w_branch_gdn, w_branch_rwkv, w_branch_sb, w_out, norm_ffn_g, ffn_w_gate, ffn_w_up, ffn_w_down, router_w, moe_w_gate, moe_w_up, moe_w_down, final_norm_g):
    b, t, d = x.shape
    n = b * t
    depth = w_in.shape[0]
    x2 = x.reshape(n, d).astype(F32)
    for layer in range(depth):
        w = w_in[layer]
        w_gdn = w[:, 0:2048].astype(BF16)
        w_ba = jnp.pad(w[:, 2048:2056], ((0, 0), (0, 120))).astype(BF16)
        w_rwkv = w[:, 2056:3848].astype(BF16)
        w_sb = w[:, 3848:5384].astype(BF16)
        w_gates = w[:, 5384:8456].astype(BF16)
        g_mix = norm_mix_g[layer].astype(F32)
        p_gdn, p_ba, p_rwkv = _normproj(x2, g_mix, [w_gdn, w_ba, w_rwkv], [F32, F32, F32])
        p_sb, p_gates = _normproj(x2, g_mix, [w_sb, w_gates], [F32, BF16])
        o_a = _gdn(p_gdn.reshape(b, t, -1), p_ba.reshape(b, t, -1), gdn_conv_w[layer], gdn_a_log[layer], gdn_dt_bias[layer],
                   gdn_norm_g[layer])
        o_b = _rwkv(p_rwkv.reshape(b, t, -1), rwkv_mu[layer], rwkv_w0[layer], rwkv_w2[layer],
                    rwkv_a0[layer], rwkv_a2[layer], rwkv_g2[layer], rwkv_k_k[layer], rwkv_k_a[layer],
                    rwkv_r_k[layer].reshape(-1), rwkv_ln_g[layer], rwkv_ln_b[layer])
        o_c = _stick_breaking(p_sb.reshape(b, t, -1))
        x2 = _merge(x2, o_a.reshape(n, -1), o_b.reshape(n, -1), o_c.reshape(n, -1), p_gates,
                    w_branch_gdn[layer].astype(BF16), w_branch_rwkv[layer].astype(BF16),
                    w_branch_sb[layer].astype(BF16), w_out[layer].astype(BF16))
        g_ffn = norm_ffn_g[layer].astype(F32)
        final_g = final_norm_g.astype(F32) if layer == depth - 1 else None
        i = layer // 2
        if layer % 2 == 0:
            x2 = _ffn(x2, g_ffn, ffn_w_gate[i].astype(BF16), ffn_w_up[i].astype(BF16),
                      ffn_w_down[i].astype(BF16), final_g=final_g)
        else:
            x2 = _moe(x2, g_ffn, router_w[i], moe_w_gate[i].astype(BF16), moe_w_up[i].astype(BF16),
                      moe_w_down[i].astype(BF16), final_g=final_g)
    return x2.reshape(b, t, d)
```

```python
import functools

import jax
import jax.numpy as jnp
from jax import lax
from jax.experimental import pallas as pl
from jax.experimental.pallas import tpu as pltpu

F32 = jnp.float32
BF16 = jnp.bfloat16

RMS_EPS = 1e-6
GN_EPS = 64e-5
L2_EPS = 1e-6

D_MODEL = 1024
GDN_HEADS = 4
GDN_HEAD_DIM = 128
GDN_WIDTH = 512
RWKV_WIDTH = 512
RWKV_HEAD_DIM = 64
RWKV_IN = 1792
SB_WIDTH = 512
SB_HEAD_DIM = 64
N_EXPERTS = 8
CHUNK = 64
LOG2E = 1.4426950408889634
SB_LOG2_CUTOFF = -160.0
SB_ROW_TILE = 128
GDN_CHUNKS_PER_STEP = 8
RWKV_CHUNKS_PER_STEP = 4
MOE_ROW_TILE = 512
VMEM_LIMIT_BYTES = 56 * 1024 * 1024


def _dot(a, b):
    return jnp.dot(a.astype(BF16), b.astype(BF16), preferred_element_type=F32)


def _dot_nt(a, b):
    return lax.dot_general(a.astype(BF16), b.astype(BF16), (((1,), (1,)), ((), ())),
                           preferred_element_type=F32)


def _split2(a):
    hi = a.astype(BF16)
    lo = (a - hi.astype(F32)).astype(BF16)
    return hi, lo


def _split3(a):
    hi = a.astype(BF16)
    r = a - hi.astype(F32)
    mid = r.astype(BF16)
    lo = (r - mid.astype(F32)).astype(BF16)
    return hi, mid, lo


def _dot3(a, b):
    ah, al = _split2(a)
    bh, bl = _split2(b)
    return (jnp.dot(ah, bh, preferred_element_type=F32)
            + jnp.dot(ah, bl, preferred_element_type=F32)
            + jnp.dot(al, bh, preferred_element_type=F32))


def _dot_exact_lhs(m, x, parts=3):
    xs = _split3(x) if parts == 3 else _split2(x)
    out = jnp.dot(m, xs[0], preferred_element_type=F32)
    for p in xs[1:]:
        out = out + jnp.dot(m, p, preferred_element_type=F32)
    return out


def _dot_exact_rhs(x, m, parts=2):
    xs = _split3(x) if parts == 3 else _split2(x)
    out = jnp.dot(xs[0], m, preferred_element_type=F32)
    for p in xs[1:]:
        out = out + jnp.dot(p, m, preferred_element_type=F32)
    return out


def _iota2(shape, dim):
    return lax.broadcasted_iota(jnp.int32, shape, dim)


def _eye(n, dtype=F32):
    return (_iota2((n, n), 0) == _iota2((n, n), 1)).astype(dtype)


def _softplus(x):
    return jnp.maximum(x, 0.0) + jnp.log(1.0 + jnp.exp(-jnp.abs(x)))


def _sigmoid(x):
    return 1.0 / (1.0 + jnp.exp(-x))


def _silu(x):
    return x * _sigmoid(x)


def _rms(x, g):
    return x * lax.rsqrt(jnp.mean(x * x, axis=-1, keepdims=True) + RMS_EPS) * g


def _nilpotent_inverse(n, eye, dot):
    t = eye + n
    x = n
    for _ in range(5):
        x = dot(x, x)
        t = t + dot(t, x)
    return t


def _nilpotent_inverse_many(ns, eye, dot):
    ts = [eye + n for n in ns]
    xs = list(ns)
    for _ in range(5):
        xs = [dot(x, x) for x in xs]
        ts = [t + dot(t, x) for t, x in zip(ts, xs)]
    return ts


def _chunk_cumsum(x):
    i = _iota2((128, 128), 0)
    j = _iota2((128, 128), 1)
    m = ((j <= i) & ((i >> 6) == (j >> 6))).astype(BF16)
    return jnp.concatenate(
        [_dot_exact_lhs(m, x[r:r + 128]) for r in range(0, x.shape[0], 128)], axis=0)


def _head_sum(x):
    i = _iota2((128, 128), 0)
    j = _iota2((128, 128), 1)
    m = ((i >> 6) == (j >> 6)).astype(BF16)
    return jnp.concatenate(
        [_dot_exact_rhs(x[:, c:c + 128], m) for c in range(0, x.shape[1], 128)], axis=1)


def _normproj_kernel(x_ref, g_ref, *refs, n_out):
    w_refs = refs[:n_out]
    o_refs = refs[n_out:]
    hn = _rms(x_ref[...], g_ref[...]).astype(BF16)
    for w_ref, o_ref in zip(w_refs, o_refs):
        o_ref[...] = jnp.dot(hn, w_ref[...], preferred_element_type=F32).astype(o_ref.dtype)


def _normproj(x2d, g, weights, out_dtypes, tm=512):
    n, d = x2d.shape
    n_out = len(weights)
    in_specs = [pl.BlockSpec((tm, d), lambda i: (i, 0)), pl.BlockSpec((1, d), lambda i: (0, 0))]
    in_specs += [pl.BlockSpec(w.shape, lambda i: (0, 0), pipeline_mode=pl.Buffered(1)) for w in weights]
    out_specs = [pl.BlockSpec((tm, w.shape[1]), lambda i: (i, 0)) for w in weights]
    out_shape = [jax.ShapeDtypeStruct((n, w.shape[1]), dt) for w, dt in zip(weights, out_dtypes)]
    return pl.pallas_call(
        functools.partial(_normproj_kernel, n_out=n_out),
        grid=(n // tm,),
        in_specs=in_specs,
        out_specs=out_specs,
        out_shape=out_shape,
        compiler_params=pltpu.CompilerParams(
            dimension_semantics=("arbitrary",), vmem_limit_bytes=VMEM_LIMIT_BYTES),
        name="normproj",
    )(x2d, g.reshape(1, d), *weights)


def _gdn_kernel(qkv_ref, z_ref, ba_ref, cw_ref, alog_ref, dtb_ref, ng_ref, o_ref,
                ext_s, state_s, q_s, k_s, v_s, beta_s, gc_s, p_s, qq_s, r_s, zz_s, oc_s, *, tt):
    nc = tt // CHUNK
    w3 = 3 * GDN_WIDTH
    t = pl.program_id(1)

    @pl.when(t == 0)
    def _():
        ext_s[0:8, :] = jnp.zeros((8, w3), F32)
        state_s[...] = jnp.zeros_like(state_s)

    raw = qkv_ref[0]
    ext_s[8:8 + tt, :] = raw
    cw = cw_ref[...]
    y = raw * cw[3:4, :]
    for i in range(3):
        y = y + ext_s[5 + i:5 + i + tt, :] * cw[i:i + 1, :]
    ext_s[0:8, :] = raw[tt - 8:tt, :]
    y = _silu(y)

    for h in range(GDN_HEADS):
        sl = slice(128 * h, 128 * h + 128)
        qh = y[:, 128 * h:128 * h + 128]
        kh = y[:, GDN_WIDTH + 128 * h:GDN_WIDTH + 128 * h + 128]
        vh = y[:, 2 * GDN_WIDTH + 128 * h:2 * GDN_WIDTH + 128 * h + 128]
        qh = qh * lax.rsqrt(jnp.sum(qh * qh, axis=-1, keepdims=True) + L2_EPS) * (GDN_HEAD_DIM ** -0.5)
        kh = kh * lax.rsqrt(jnp.sum(kh * kh, axis=-1, keepdims=True) + L2_EPS)
        q_s[:, sl] = qh
        k_s[:, sl] = kh
        v_s[:, sl] = vh

    ec = _iota2((128, 2 * GDN_WIDTH), 0)
    el = _iota2((128, 2 * GDN_WIDTH), 1)
    ba = _dot_exact_rhs(ba_ref[0], (ec == (el >> 7)).astype(BF16), parts=3)
    beta_s[...] = _sigmoid(ba[:, 0:GDN_WIDTH])
    g = -jnp.exp(alog_ref[...]) * _softplus(ba[:, GDN_WIDTH:2 * GDN_WIDTH] + dtb_ref[...])
    gc_s[...] = _chunk_cumsum(g)

    eye64 = _eye(CHUNK)
    eye128 = _eye(128)
    ii = _iota2((CHUNK, CHUNK), 0)
    jj = _iota2((CHUNK, CHUNK), 1)

    def chunk_body(ci, carry):
        probs = [(ci * GDN_CHUNKS_PER_STEP + u, h) for u in range(GDN_CHUNKS_PER_STEP)
                 for h in range(GDN_HEADS)]
        rows = [pl.ds(pl.multiple_of(c * CHUNK, CHUNK), CHUNK) for c, _ in probs]
        lanes = [slice(128 * h, 128 * h + 128) for _, h in probs]
        idx = range(len(probs))
        q = [q_s[rows[i], lanes[i]] for i in idx]
        k = [k_s[rows[i], lanes[i]] for i in idx]
        gcc = [gc_s[rows[i], lanes[i]] for i in idx]
        gl = [g[CHUNK - 1:CHUNK, :] for g in gcc]
        gcr = [g.T[0:CHUNK, :] for g in gcc]
        dec_incl = [jnp.exp(jnp.where(ii >= jj, gcc[i][:, 0:CHUNK] - gcr[i], -jnp.inf)) for i in idx]
        kb = [k[i] * beta_s[rows[i], lanes[i]] for i in idx]
        a_mat = [_dot_nt(kb[i], k[i]) for i in idx]
        attn = [_dot_nt(q[i], k[i]) * dec_incl[i] for i in idx]
        kdt = [(k[i] * jnp.exp(gl[i] - gcc[i])).T for i in idx]
        tinv = _nilpotent_inverse_many(
            [-a_mat[i] * jnp.where(ii > jj, dec_incl[i], 0.0) for i in idx], eye64, _dot)
        u = [_dot(tinv[i], v_s[rows[i], lanes[i]] * beta_s[rows[i], lanes[i]]) for i in idx]
        w = [_dot(tinv[i], kb[i] * jnp.exp(gcc[i])) for i in idx]
        for i, (c, h) in enumerate(probs):
            p_s[c, h] = jnp.exp(gl[i]) * eye128 - _dot(kdt[i], w[i])
        for i, (c, h) in enumerate(probs):
            qq_s[c, h] = _dot(kdt[i], u[i])
        for i, (c, h) in enumerate(probs):
            r_s[c, h] = q[i] * jnp.exp(gcc[i]) - _dot(attn[i], w[i])
        for i, (c, h) in enumerate(probs):
            zz_s[c, h] = _dot(attn[i], u[i])
        return carry

    lax.fori_loop(0, nc // GDN_CHUNKS_PER_STEP, chunk_body, 0)

    def scan_body(c, carry):
        rows = pl.ds(pl.multiple_of(c * CHUNK, CHUNK), CHUNK)
        s = [state_s[h] for h in range(GDN_HEADS)]
        s_new = [_dot3(p_s[c, h], s[h]) for h in range(GDN_HEADS)]
        o = [_dot(r_s[c, h], s[h]) for h in range(GDN_HEADS)]
        for h in range(GDN_HEADS):
            state_s[h] = s_new[h] + qq_s[c, h]
            oc_s[rows, 128 * h:128 * h + 128] = o[h] + zz_s[c, h]
        return carry

    lax.fori_loop(0, nc, scan_body, 0)

    o = oc_s[...]
    z = z_ref[0]
    ng = ng_ref[...]
    for h in range(GDN_HEADS):
        sl = slice(128 * h, 128 * h + 128)
        oh = o[:, sl]
        oh = oh * lax.rsqrt(jnp.mean(oh * oh, axis=-1, keepdims=True) + RMS_EPS) * ng[:, sl]
        o_ref[0, :, sl] = (oh * _silu(z[:, sl])).astype(o_ref.dtype)


def _gdn(proj_a, proj_ba, conv_w, a_log, dt_bias, norm_g, tt=512):
    b, t, _ = proj_a.shape
    nc = tt // CHUNK
    w3 = 3 * GDN_WIDTH
    rep = lambda p: jnp.repeat(p.astype(F32), 128).reshape(1, GDN_WIDTH)
    small = lambda shape: pl.BlockSpec(shape, lambda i, j: (0, 0))
    return pl.pallas_call(
        functools.partial(_gdn_kernel, tt=tt),
        grid=(b, t // tt),
        in_specs=[
            pl.BlockSpec((1, tt, w3), lambda i, j: (i, j, 0)),
            pl.BlockSpec((1, tt, GDN_WIDTH), lambda i, j: (i, j, 3)),
            pl.BlockSpec((1, tt, 128), lambda i, j: (i, j, 0)),
            small((4, w3)), small((1, GDN_WIDTH)), small((1, GDN_WIDTH)), small((1, GDN_WIDTH)),
        ],
        out_specs=pl.BlockSpec((1, tt, GDN_WIDTH), lambda i, j: (i, j, 0)),
        out_shape=jax.ShapeDtypeStruct((b, t, GDN_WIDTH), BF16),
        scratch_shapes=[
            pltpu.VMEM((tt + 8, w3), F32),
            pltpu.VMEM((GDN_HEADS, 128, 128), F32),
            pltpu.VMEM((tt, GDN_WIDTH), F32), pltpu.VMEM((tt, GDN_WIDTH), F32),
            pltpu.VMEM((tt, GDN_WIDTH), F32), pltpu.VMEM((tt, GDN_WIDTH), F32),
            pltpu.VMEM((tt, GDN_WIDTH), F32),
            pltpu.VMEM((nc, GDN_HEADS, 128, 128), F32), pltpu.VMEM((nc, GDN_HEADS, 128, 128), F32),
            pltpu.VMEM((nc, GDN_HEADS, CHUNK, 128), F32), pltpu.VMEM((nc, GDN_HEADS, CHUNK, 128), F32),
            pltpu.VMEM((tt, GDN_WIDTH), F32),
        ],
        compiler_params=pltpu.CompilerParams(
            dimension_semantics=("arbitrary", "arbitrary"), vmem_limit_bytes=VMEM_LIMIT_BYTES),
        name="gdn",
    )(proj_a, proj_a, proj_ba, conv_w.astype(F32), rep(a_log), rep(dt_bias),
      jnp.tile(norm_g.astype(F32), GDN_HEADS).reshape(1, GDN_WIDTH))


def _rwkv_kernel(h_ref, mu_ref, w0_ref, a0_ref, kk_ref, ka_ref, rk_ref, lng_ref, lnb_ref,
                 wa_ref, g2_ref, o_ref,
                 ext_s, state_s, r_s, kn_s, k2_s, v_s, a_s, lw_s, lc_s, p_s, qq_s, rh_s, yc_s, y_s,
                 *, tt):
    nc = tt // CHUNK
    npair = RWKV_WIDTH // 128
    t = pl.program_id(1)

    @pl.when(t == 0)
    def _():
        ext_s[0:8, :] = jnp.zeros((8, RWKV_IN), F32)
        state_s[...] = jnp.zeros_like(state_s)

    raw = h_ref[0]
    ext_s[8:8 + tt, :] = raw
    prev = ext_s[7:7 + tt, :]
    ext_s[0:8, :] = raw[tt - 8:tt, :]
    hl = raw + (prev - raw) * mu_ref[...]
    r = hl[:, 0:512]
    k = hl[:, 512:1024]
    v = hl[:, 1024:1536]
    xwa = hl[:, 1536:1664]
    xg = hl[:, 1664:1792]
    lane128 = _iota2((1, 128), 1)
    xwa = jnp.where(lane128 < 64, jnp.tanh(xwa), xwa)
    lora = _dot3(xwa, wa_ref[...])
    w_log = -_softplus(-(w0_ref[...] + lora[:, 0:512])) - 0.5
    lw = -jnp.exp(w_log)
    a = _sigmoid(a0_ref[...] + lora[:, 512:1024])
    gate = _dot3(_sigmoid(xg), g2_ref[...])
    kk = k * kk_ref[...]
    k2 = k * (1.0 + (a - 1.0) * ka_ref[...])
    kn = kk * lax.rsqrt(_head_sum(kk * kk) + L2_EPS)
    r_s[...] = r
    kn_s[...] = kn
    k2_s[...] = k2
    v_s[...] = v
    a_s[...] = a
    lw_s[...] = lw
    lc_s[...] = _chunk_cumsum(lw)

    eye64 = _eye(CHUNK)
    eye128 = _eye(128)
    ii = _iota2((CHUNK, CHUNK), 0)
    jj = _iota2((CHUNK, CHUNK), 1)
    strict = ii > jj
    incl = ii >= jj
    lane = _iota2((CHUNK, 128), 1)
    bi = _iota2((128, 128), 0)
    bj = _iota2((128, 128), 1)
    blockdiag = (bi < 64) == (bj < 64)

    def chunk_body(ci, carry):
        units = [(ci * RWKV_CHUNKS_PER_STEP + u, p) for u in range(RWKV_CHUNKS_PER_STEP) for p in range(npair)]
        nu = range(len(units))
        rows = [pl.ds(pl.multiple_of(c * CHUNK, CHUNK), CHUNK) for c, _ in units]
        sls = [slice(128 * p, 128 * p + 128) for _, p in units]
        lcc = [lc_s[rows[u], sls[u]] for u in nu]
        vc = [v_s[rows[u], sls[u]] for u in nu]
        lcl = [x[CHUNK - 1:CHUNK, :] for x in lcc]
        ginv = [jnp.exp(-x) for x in lcc]
        gend = [jnp.exp(lcl[u] - lcc[u]) for u in nu]
        kna = [kn_s[rows[u], sls[u]] * a_s[rows[u], sls[u]] for u in nu]
        at = [-kn_s[rows[u], sls[u]] * jnp.exp(lcc[u] - lw_s[rows[u], sls[u]]) for u in nu]
        bt = [kna[u] * ginv[u] for u in nu]
        kt = [k2_s[rows[u], sls[u]] * ginv[u] for u in nu]
        rt = [r_s[rows[u], sls[u]] * jnp.exp(lcc[u]) for u in nu]
        probs = [(u, j) for u in nu for j in range(2)]
        npr = range(len(probs))
        masks = [(lane < 64) if j == 0 else (lane >= 64) for _, j in probs]
        at_m = [jnp.where(masks[i], at[u], 0.0) for i, (u, _) in enumerate(probs)]
        rt_m = [jnp.where(masks[i], rt[u], 0.0) for i, (u, _) in enumerate(probs)]
        ar = [jnp.concatenate([at_m[i], rt_m[i]], axis=0) for i in npr]
        xb = [_dot_nt(ar[i], bt[u]) for i, (u, _) in enumerate(probs)]
        xk = [_dot_nt(ar[i], kt[u]) for i, (u, _) in enumerate(probs)]
        a_ab = [jnp.where(strict, x[0:CHUNK], 0.0) for x in xb]
        a_rb = [jnp.where(incl, x[CHUNK:2 * CHUNK], 0.0) for x in xb]
        a_ak = [jnp.where(strict, x[0:CHUNK], 0.0) for x in xk]
        a_rk = [jnp.where(incl, x[CHUNK:2 * CHUNK], 0.0) for x in xk]
        akv = [_dot(a_ak[i], vc[u]) for i, (u, _) in enumerate(probs)]
        arkv = [_dot(a_rk[i], vc[u]) for i, (u, _) in enumerate(probs)]
        bbt = [(kna[u] * gend[u]).T for u in nu]
        kbt = [(k2_s[rows[u], sls[u]] * gend[u]).T for u in nu]
        tinv = _nilpotent_inverse_many(a_ab, eye64, _dot)
        au = [_dot(tinv[i], jnp.concatenate([at_m[i], akv[i]], axis=1)) for i in npr]
        ry = [_dot(a_rb[i], au[i]) for i in npr]
        kv = [_dot(kbt[u], vc[u]) for u in nu]
        pq = []
        for u, (c, p) in enumerate(units):
            i0, i1 = 2 * u, 2 * u + 1
            ahat = au[i0][:, 0:128] + au[i1][:, 0:128]
            uv = jnp.where(lane < 64, au[i0][:, 128:256], au[i1][:, 128:256])
            rh_s[c, p] = rt_m[i0] + rt_m[i1] + ry[i0][:, 0:128] + ry[i1][:, 0:128]
            yc_s[c, p] = jnp.where(lane < 64, ry[i0][:, 128:256] + arkv[i0], ry[i1][:, 128:256] + arkv[i1])
            pq.append(_dot(bbt[u], jnp.concatenate([ahat, uv], axis=1)))
        for u, (c, p) in enumerate(units):
            p_s[c, p] = jnp.where(blockdiag, pq[u][:, 0:128], 0.0) + eye128 * jnp.exp(lcl[u])
            qq_s[c, p] = jnp.where(blockdiag, pq[u][:, 128:256] + kv[u], 0.0)
        return carry

    lax.fori_loop(0, nc // RWKV_CHUNKS_PER_STEP, chunk_body, 0)

    def scan_body(c, carry):
        rows = pl.ds(pl.multiple_of(c * CHUNK, CHUNK), CHUNK)
        s = [state_s[p] for p in range(npair)]
        s_new = [_dot3(p_s[c, p], s[p]) for p in range(npair)]
        y = [_dot(rh_s[c, p], s[p]) for p in range(npair)]
        for p in range(npair):
            state_s[p] = s_new[p] + qq_s[c, p]
            y_s[rows, 128 * p:128 * p + 128] = y[p] + yc_s[c, p]
        return carry

    lax.fori_loop(0, nc, scan_body, 0)

    y = y_s[...]
    mean = _head_sum(y) * (1.0 / RWKV_HEAD_DIM)
    yc = y - mean
    var = _head_sum(yc * yc) * (1.0 / RWKV_HEAD_DIM)
    yn = yc * lax.rsqrt(var + GN_EPS) * lng_ref[...] + lnb_ref[...]
    bonus = _head_sum(r * k2 * rk_ref[...]) * v
    o_ref[0] = ((yn + bonus) * gate).astype(o_ref.dtype)


def _dot3_nt(a, b):
    ah, al = _split2(a)
    bh, bl = _split2(b)
    dn = (((1,), (1,)), ((), ()))
    return (lax.dot_general(ah, bh, dn, preferred_element_type=F32)
            + lax.dot_general(ah, bl, dn, preferred_element_type=F32)
            + lax.dot_general(al, bh, dn, preferred_element_type=F32))


def _rwkv(h, mu, w0, w2, a0, a2, g2, k_k, k_a, r_k, ln_g, ln_b, tt=512):
    b, t, _ = h.shape
    nc = tt // CHUNK
    npair = RWKV_WIDTH // 128
    row = lambda p: p.astype(F32).reshape(1, -1)
    wa = jnp.zeros((128, 2 * RWKV_WIDTH), F32)
    wa = wa.at[0:64, 0:RWKV_WIDTH].set(w2.astype(F32)).at[64:128, RWKV_WIDTH:].set(a2.astype(F32))
    small = lambda shape: pl.BlockSpec(shape, lambda i, j: (0, 0))
    vec = small((1, RWKV_WIDTH))
    return pl.pallas_call(
        functools.partial(_rwkv_kernel, tt=tt),
        grid=(b, t // tt),
        in_specs=[pl.BlockSpec((1, tt, RWKV_IN), lambda i, j: (i, j, 0)), small((1, RWKV_IN)),
                  vec, vec, vec, vec, vec, vec, vec,
                  small((128, 2 * RWKV_WIDTH)), small((128, RWKV_WIDTH))],
        out_specs=pl.BlockSpec((1, tt, RWKV_WIDTH), lambda i, j: (i, j, 0)),
        out_shape=jax.ShapeDtypeStruct((b, t, RWKV_WIDTH), BF16),
        scratch_shapes=[
            pltpu.VMEM((tt + 8, RWKV_IN), F32),
            pltpu.VMEM((npair, 128, 128), F32),
        ] + [pltpu.VMEM((tt, RWKV_WIDTH), F32)] * 7 + [
            pltpu.VMEM((nc, npair, 128, 128), F32), pltpu.VMEM((nc, npair, 128, 128), F32),
            pltpu.VMEM((nc, npair, CHUNK, 128), F32), pltpu.VMEM((nc, npair, CHUNK, 128), F32),
            pltpu.VMEM((tt, RWKV_WIDTH), F32),
        ],
        compiler_params=pltpu.CompilerParams(
            dimension_semantics=("arbitrary", "arbitrary"), vmem_limit_bytes=VMEM_LIMIT_BYTES),
        name="rwkv7",
    )(h, row(mu), row(w0), row(a0), row(k_k), row(k_a), row(r_k), row(ln_g), row(ln_b),
      wa, g2.astype(F32))


def _sb_kernel(q_ref, k_ref, v_ref, o_ref, acc_s, aux_s, *, bq, bk):
    qi = pl.program_id(2)
    nsub = bq // bk
    q = q_ref[0] * (SB_HEAD_DIM ** -0.5 * LOG2E)
    lane = _iota2((1, 128), 1)
    qm = [jnp.where(lane < 64, q, 0.0).astype(BF16), jnp.where(lane >= 64, q, 0.0).astype(BF16)]
    ti = _iota2((2 * bk, bk), 0)
    tj = _iota2((2 * bk, bk), 1)
    cum_mat = ((ti & (bk - 1)) > tj).astype(BF16)
    acc_s[...] = jnp.zeros_like(acc_s)
    aux_s[...] = jnp.zeros_like(aux_s)

    def load_kv(kb):
        start = pl.multiple_of(kb * bk, bk)
        return (start, k_ref[0, pl.ds(start, bk), :].astype(BF16), v_ref[0, pl.ds(start, bk), :].astype(BF16))

    def stage_scores(item):
        (start, k, v), ra, nr, j, masked = item
        z = lax.dot_general(qm[j][ra:ra + nr], k, (((1,), (1,)), ((), ())), preferred_element_type=F32)
        return z

    def stage_cumsum(item, z):
        (start, k, v), ra, nr, j, masked = item
        rows = slice(ra, ra + nr)
        neg_abs = lax.bitcast_convert_type(
            lax.bitcast_convert_type(z, jnp.uint32) | jnp.uint32(0x80000000), F32)
        lsig = jnp.minimum(z, 0.0) - jnp.log(1.0 + jnp.exp2(neg_abs)) * LOG2E
        l1 = lsig - z
        msk = None
        if masked:
            msk = (start + _iota2((nr, bk), 1)) < (qi * bq + ra + _iota2((nr, bk), 0))
            l1 = jnp.where(msk, l1, 0.0)
        l1_hi = l1.astype(BF16)
        l1_lo = (l1 - l1_hi.astype(F32)).astype(BF16)
        cr = (jnp.dot(jnp.concatenate([l1_hi, l1_lo], axis=1), cum_mat, preferred_element_type=F32)
              + aux_s[j, rows, 0:1])
        aux_s[j, rows, :] = cr + l1
        return lsig, cr, msk

    def stage_values(item, state):
        (start, k, v), ra, nr, j, masked = item
        lsig, cr, msk = state
        att = jnp.exp2(lsig + cr)
        if masked:
            att = jnp.where(msk, att, 0.0)
        acc_s[j, ra:ra + nr, :] += jnp.dot(att.astype(BF16), v, preferred_element_type=F32)

    def run_items(items):
        n = len(items)
        zs = {}
        states = {}
        for s in range(n + 2):
            if s < n:
                zs[s] = stage_scores(items[s])
            if 0 <= s - 1 < n:
                states[s - 1] = stage_cumsum(items[s - 1], zs.pop(s - 1))
            if 0 <= s - 2 < n:
                stage_values(items[s - 2], states.pop(s - 2))

    def max_carry(ra, nr):
        r = jnp.maximum(aux_s[0, ra:ra + nr, :], aux_s[1, ra:ra + nr, :])
        r = jnp.max(jnp.where(lane == 0, r, -jnp.inf), axis=0, keepdims=True)
        return jnp.max(r, axis=1, keepdims=True)[0, 0]

    items = []
    for d in range(nsub - 1, -1, -1):
        kv = load_kv(qi * nsub + d)
        ra = d * bk
        while ra < bq:
            nr = SB_ROW_TILE if (bq - ra) % (2 * SB_ROW_TILE) else 2 * SB_ROW_TILE
            items += [(kv, ra, nr, 0, True), (kv, ra, nr, 1, True)]
            ra += nr
    run_items(items)

    def cond(c):
        kb, rmax = c
        return (kb >= 0) & (rmax > SB_LOG2_CUTOFF)

    for ra in range(0, bq, 2 * SB_ROW_TILE):
        nr = 2 * SB_ROW_TILE

        def body(c, ra=ra, nr=nr):
            kb, _ = c
            items = []
            for kv in (load_kv(kb), load_kv(kb - 1)):
                items += [(kv, ra, nr, 0, False), (kv, ra, nr, 1, False)]
            run_items(items)
            return kb - 2, max_carry(ra, nr)

        lax.while_loop(cond, body, (qi * nsub - 1, max_carry(ra, nr)))
    o_ref[0] = jnp.where(lane < 64, acc_s[0], acc_s[1]).astype(o_ref.dtype)


def _stick_breaking(h_sb, bq=512, bk=128):
    b, t, _ = h_sb.shape
    npair = SB_WIDTH // 128
    return pl.pallas_call(
        functools.partial(_sb_kernel, bq=bq, bk=bk),
        grid=(b, npair, t // bq),
        in_specs=[
            pl.BlockSpec((1, bq, 128), lambda i, p, j: (i, j, p)),
            pl.BlockSpec((1, t, 128), lambda i, p, j: (i, 0, npair + p)),
            pl.BlockSpec((1, t, 128), lambda i, p, j: (i, 0, 2 * npair + p)),
        ],
        out_specs=pl.BlockSpec((1, bq, 128), lambda i, p, j: (i, j, p)),
        out_shape=jax.ShapeDtypeStruct((b, t, SB_WIDTH), BF16),
        scratch_shapes=[pltpu.VMEM((2, bq, 128), F32), pltpu.VMEM((2, bq, bk), F32)],
        compiler_params=pltpu.CompilerParams(
            dimension_semantics=("arbitrary", "arbitrary", "arbitrary"),
            vmem_limit_bytes=VMEM_LIMIT_BYTES),
        name="stickbreak",
    )(h_sb, h_sb, h_sb)


def _merge_kernel(x_ref, oa_ref, ob_ref, oc_ref, gt_ref, wa_ref, wb_ref, wc_ref, wo_ref, o_ref):
    d = D_MODEL
    g = _sigmoid(gt_ref[...].astype(F32))
    m = (g[:, 0:d] * _dot(oa_ref[...], wa_ref[...])
         + g[:, d:2 * d] * _dot(ob_ref[...], wb_ref[...])
         + g[:, 2 * d:3 * d] * _dot(oc_ref[...], wc_ref[...]))
    o_ref[...] = x_ref[...] + _dot(m, wo_ref[...])


def _merge(x2d, oa, ob, oc, gates, wa, wb, wc, wo, tm=512):
    n, d = x2d.shape
    rowspec = lambda w: pl.BlockSpec((tm, w), lambda i: (i, 0))
    full = lambda w: pl.BlockSpec(w.shape, lambda i: (0, 0))
    return pl.pallas_call(
        _merge_kernel,
        grid=(n // tm,),
        in_specs=[rowspec(d), rowspec(512), rowspec(512), rowspec(512), rowspec(3 * d),
                  full(wa), full(wb), full(wc), full(wo)],
        out_specs=rowspec(d),
        out_shape=jax.ShapeDtypeStruct((n, d), F32),
        compiler_params=pltpu.CompilerParams(
            dimension_semantics=("arbitrary",), vmem_limit_bytes=VMEM_LIMIT_BYTES),
        name="merge",
    )(x2d, oa, ob, oc, gates, wa, wb, wc, wo)


def _router_kernel(x_ref, g_ref, rw_ref, comb_ref, idx_ref, cnt_ref, run_s):
    @pl.when(pl.program_id(0) == 0)
    def _():
        run_s[...] = jnp.zeros_like(run_s)

    hn = _rms(x_ref[...], g_ref[...])
    logits = _dot3(hn, rw_ref[...])
    tm = logits.shape[0]
    lane = _iota2(logits.shape, 1)
    m1 = jnp.max(logits, axis=-1, keepdims=True)
    i1 = jnp.min(jnp.where(logits == m1, lane, N_EXPERTS), axis=-1, keepdims=True)
    sel1 = lane == i1
    rest = jnp.where(sel1, -jnp.inf, logits)
    m2 = jnp.max(rest, axis=-1, keepdims=True)
    i2 = jnp.min(jnp.where(rest == m2, lane, N_EXPERTS), axis=-1, keepdims=True)
    sel2 = lane == i2
    e2 = jnp.exp(m2 - m1)
    w1 = 1.0 / (1.0 + e2)
    comb_ref[...] = jnp.where(sel1, w1, 0.0) + jnp.where(sel2, e2 * w1, 0.0)

    member = (sel1 | sel2).astype(BF16)
    before = (_iota2((tm, tm), 0) > _iota2((tm, tm), 1)).astype(BF16)
    rank = jnp.dot(before, member, preferred_element_type=F32) + run_s[...]
    r1 = jnp.sum(jnp.where(sel1, rank, 0.0), axis=-1, keepdims=True).astype(jnp.int32)
    r2 = jnp.sum(jnp.where(sel2, rank, 0.0), axis=-1, keepdims=True).astype(jnp.int32)
    run_s[...] += jnp.sum(member.astype(F32), axis=0, keepdims=True)
    cnt_ref[...] = run_s[...]
    lane128 = _iota2((tm, 128), 1)
    idx_ref[...] = jnp.where(lane128 == 0, i1, jnp.where(lane128 == 1, i2,
                             jnp.where(lane128 == 2, r1, jnp.where(lane128 == 3, r2, 0))))


def _router(x2d, g, rw, tm=512):
    n, d = x2d.shape
    return pl.pallas_call(
        _router_kernel,
        grid=(n // tm,),
        in_specs=[pl.BlockSpec((tm, d), lambda i: (i, 0)), pl.BlockSpec((1, d), lambda i: (0, 0)),
                  pl.BlockSpec(rw.shape, lambda i: (0, 0))],
        out_specs=[pl.BlockSpec((tm, N_EXPERTS), lambda i: (i, 0)),
                   pl.BlockSpec((tm, 128), lambda i: (i, 0)),
                   pl.BlockSpec((1, N_EXPERTS), lambda i: (0, 0))],
        out_shape=[jax.ShapeDtypeStruct((n, N_EXPERTS), F32),
                   jax.ShapeDtypeStruct((n, 128), jnp.int32),
                   jax.ShapeDtypeStruct((1, N_EXPERTS), F32)],
        scratch_shapes=[pltpu.VMEM((1, N_EXPERTS), F32)],
        compiler_params=pltpu.CompilerParams(dimension_semantics=("arbitrary",)),
        name="router",
    )(x2d, g.reshape(1, d), rw.astype(F32))


def _row_copy(src_ref, src_row, dst_ref, dst_row, sem):
    return pltpu.make_async_copy(src_ref.at[pl.ds(src_row, 1), :], dst_ref.at[pl.ds(dst_row, 1), :], sem)


def _drain_rows(n_copies, src_ref, dst_ref, sem):
    def body(t, carry):
        _row_copy(src_ref, 0, dst_ref, 0, sem).wait()
        return carry

    lax.fori_loop(0, n_copies, body, 0, unroll=8)


def _dispatch_kernel(pos1_ref, pos2_ref, x_ref, g_ref, xs_in_ref, xs_ref, hn_s, sems):
    del xs_in_ref
    tm = hn_s.shape[1]
    i = pl.program_id(0)
    slot = i % 2
    hn_s[slot] = _rms(x_ref[...], g_ref[...])
    src = hn_s.at[slot]
    sem = sems.at[slot]

    def issue(t, carry):
        _row_copy(src, t, xs_ref, pos1_ref[t], sem).start()
        _row_copy(src, t, xs_ref, pos2_ref[t], sem).start()
        return carry

    lax.fori_loop(0, tm, issue, 0, unroll=8)

    @pl.when(i > 0)
    def _():
        _drain_rows(2 * tm, hn_s.at[1 - slot], xs_ref, sems.at[1 - slot])

    @pl.when(i == pl.num_programs(0) - 1)
    def _():
        _drain_rows(2 * tm, src, xs_ref, sem)


def _dispatch(x2d, g, pos1, pos2, n_rows, tm=512):
    n, d = x2d.shape
    smem = lambda: pl.BlockSpec((tm,), lambda i: (i,), memory_space=pltpu.SMEM)
    return pl.pallas_call(
        _dispatch_kernel,
        grid=(n // tm,),
        in_specs=[smem(), smem(), pl.BlockSpec((tm, d), lambda i: (i, 0)),
                  pl.BlockSpec((1, d), lambda i: (0, 0)), pl.BlockSpec(memory_space=pl.ANY)],
        out_specs=pl.BlockSpec(memory_space=pl.ANY),
        out_shape=jax.ShapeDtypeStruct((n_rows, d), F32),
        scratch_shapes=[pltpu.VMEM((2, tm, d), F32), pltpu.SemaphoreType.DMA((2,))],
        input_output_aliases={4: 0},
        compiler_params=pltpu.CompilerParams(dimension_semantics=("arbitrary",)),
        name="moe_dispatch",
    )(pos1, pos2, x2d, g.reshape(1, d), jnp.zeros((n_rows, d), F32))


def _grouped_ffn_kernel(te_ref, nt_ref, xs_ref, wg_ref, wu_ref, wd_ref, o_ref, xb_s, acc_s):
    i = pl.program_id(0)
    f = pl.program_id(1)

    @pl.when(f == 0)
    def _():
        xb_s[...] = xs_ref[...].astype(BF16)
        acc_s[...] = jnp.zeros_like(acc_s)

    @pl.when(i < nt_ref[0])
    def _():
        xb = xb_s[...]
        act = _silu(jnp.dot(xb, wg_ref[...], preferred_element_type=F32)) * jnp.dot(
            xb, wu_ref[...], preferred_element_type=F32)
        acc_s[...] += jnp.dot(act.astype(BF16), wd_ref[...], preferred_element_type=F32)

    @pl.when(f == pl.num_programs(1) - 1)
    def _():
        o_ref[...] = acc_s[...]


def _grouped_ffn(xs, tile_expert, n_tiles, wg, wu, wd, tm, tf=1792):
    n_rows, d = xs.shape
    dff = wg.shape[2]
    nf = dff // tf
    fsel = lambda i, f, te, nt: jnp.where(i < nt[0], f, nf - 1)
    return pl.pallas_call(
        _grouped_ffn_kernel,
        grid_spec=pltpu.PrefetchScalarGridSpec(
            num_scalar_prefetch=2,
            grid=(n_rows // tm, nf),
            in_specs=[pl.BlockSpec((tm, d), lambda i, f, te, nt: (i, 0)),
                      pl.BlockSpec((None, d, tf), lambda i, f, te, nt: (te[i], 0, fsel(i, f, te, nt))),
                      pl.BlockSpec((None, d, tf), lambda i, f, te, nt: (te[i], 0, fsel(i, f, te, nt))),
                      pl.BlockSpec((None, tf, d), lambda i, f, te, nt: (te[i], fsel(i, f, te, nt), 0))],
            out_specs=pl.BlockSpec((tm, d), lambda i, f, te, nt: (i, 0)),
            scratch_shapes=[pltpu.VMEM((tm, d), BF16), pltpu.VMEM((tm, d), F32)]),
        out_shape=jax.ShapeDtypeStruct((n_rows, d), F32),
        compiler_params=pltpu.CompilerParams(
            dimension_semantics=("arbitrary", "arbitrary"), vmem_limit_bytes=VMEM_LIMIT_BYTES),
        name="moe_grouped_ffn",
    )(tile_expert, n_tiles, xs, wg, wu, wd)


def _combine_kernel(*refs, final):
    if final:
        pos1_ref, pos2_ref, npos1_ref, npos2_ref, x_ref, comb_ref, fg_ref, ys_ref, o_ref, y1_s, y2_s, sems = refs
    else:
        pos1_ref, pos2_ref, npos1_ref, npos2_ref, x_ref, comb_ref, ys_ref, o_ref, y1_s, y2_s, sems = refs
    tm = y1_s.shape[1]
    i = pl.program_id(0)
    slot = i % 2

    def gather(p1_ref, p2_ref, sl):
        def issue(t, carry):
            _row_copy(ys_ref, p1_ref[t], y1_s.at[sl], t, sems.at[sl]).start()
            _row_copy(ys_ref, p2_ref[t], y2_s.at[sl], t, sems.at[sl]).start()
            return carry

        lax.fori_loop(0, tm, issue, 0, unroll=8)

    @pl.when(i == 0)
    def _():
        gather(pos1_ref, pos2_ref, 0)

    @pl.when(i < pl.num_programs(0) - 1)
    def _():
        gather(npos1_ref, npos2_ref, 1 - slot)

    _drain_rows(2 * tm, ys_ref, y1_s.at[slot], sems.at[slot])
    comb = comb_ref[...]
    w1 = jnp.max(comb, axis=-1, keepdims=True)
    w2 = jnp.sum(comb, axis=-1, keepdims=True) - w1
    out = x_ref[...] + w1 * y1_s[slot] + w2 * y2_s[slot]
    if final:
        out = _rms(out, fg_ref[...])
    o_ref[...] = out


def _combine(x2d, comb, pos1, pos2, ys, final_g=None, tm=512):
    n, d = x2d.shape
    final = final_g is not None
    nsteps = n // tm
    smem = lambda: pl.BlockSpec((tm,), lambda i: (i,), memory_space=pltpu.SMEM)
    smem_next = lambda: pl.BlockSpec((tm,), lambda i: (jnp.minimum(i + 1, nsteps - 1),),
                                     memory_space=pltpu.SMEM)
    in_specs = [smem(), smem(), smem_next(), smem_next(), pl.BlockSpec((tm, d), lambda i: (i, 0)),
                pl.BlockSpec((tm, N_EXPERTS), lambda i: (i, 0))]
    args = [pos1, pos2, pos1, pos2, x2d, comb]
    if final:
        in_specs.append(pl.BlockSpec((1, d), lambda i: (0, 0)))
        args.append(final_g.reshape(1, d))
    in_specs.append(pl.BlockSpec(memory_space=pl.ANY))
    args.append(ys)
    return pl.pallas_call(
        functools.partial(_combine_kernel, final=final),
        grid=(nsteps,),
        in_specs=in_specs,
        out_specs=pl.BlockSpec((tm, d), lambda i: (i, 0)),
        out_shape=jax.ShapeDtypeStruct((n, d), F32),
        scratch_shapes=[pltpu.VMEM((2, tm, d), F32), pltpu.VMEM((2, tm, d), F32),
                        pltpu.SemaphoreType.DMA((2,))],
        compiler_params=pltpu.CompilerParams(dimension_semantics=("arbitrary",)),
        name="moe_combine",
    )(*args)


def _moe(x2d, g, rw, wg, wu, wd, final_g=None, tm=MOE_ROW_TILE):
    n, d = x2d.shape
    comb, idx, cnt = _router(x2d, g, rw)
    counts = cnt[0].astype(jnp.int32)
    padded = ((counts + tm - 1) // tm) * tm
    ends = jnp.cumsum(padded)
    starts = ends - padded
    pos1 = starts[idx[:, 0]] + idx[:, 2]
    pos2 = starts[idx[:, 1]] + idx[:, 3]
    n_rows = 2 * n + N_EXPERTS * tm
    n_tiles = (ends[-1] // tm).reshape(1)
    tile_expert = jnp.minimum(
        jnp.searchsorted(ends // tm, jnp.arange(n_rows // tm, dtype=jnp.int32), side="right"),
        N_EXPERTS - 1).astype(jnp.int32)
    xs = _dispatch(x2d, g, pos1, pos2, n_rows)
    ys = _grouped_ffn(xs, tile_expert, n_tiles, wg, wu, wd, tm)
    return _combine(x2d, comb, pos1, pos2, ys, final_g=final_g)


def _ffn_kernel(*refs, final):
    if final:
        x_ref, g_ref, wg_ref, wu_ref, wd_ref, fg_ref, o_ref = refs
    else:
        x_ref, g_ref, wg_ref, wu_ref, wd_ref, o_ref = refs
    x = x_ref[...]
    hn = _rms(x, g_ref[...]).astype(BF16)
    act = _silu(jnp.dot(hn, wg_ref[...], preferred_element_type=F32)) * jnp.dot(
        hn, wu_ref[...], preferred_element_type=F32)
    out = x + jnp.dot(act.astype(BF16), wd_ref[...], preferred_element_type=F32)
    if final:
        out = _rms(out, fg_ref[...])
    o_ref[...] = out


def _ffn(x2d, g, wg, wu, wd, final_g=None, tm=512):
    n, d = x2d.shape
    final = final_g is not None
    xmap = lambda i: (i, 0)
    cmap = lambda i: (0, 0)
    resident = lambda w: pl.BlockSpec(w.shape, cmap, pipeline_mode=pl.Buffered(1))
    in_specs = [pl.BlockSpec((tm, d), xmap), pl.BlockSpec((1, d), cmap), resident(wg), resident(wu), resident(wd)]
    args = [x2d, g.reshape(1, d), wg, wu, wd]
    if final:
        in_specs.append(pl.BlockSpec((1, d), cmap))
        args.append(final_g.reshape(1, d))
    return pl.pallas_call(
        functools.partial(_ffn_kernel, final=final),
        grid=(n // tm,),
        in_specs=in_specs,
        out_specs=pl.BlockSpec((tm, d), xmap),
        out_shape=jax.ShapeDtypeStruct((n, d), F32),
        compiler_params=pltpu.CompilerParams(
            dimension_semantics=("arbitrary",), vmem_limit_bytes=VMEM_LIMIT_BYTES),
        name="dense_ffn",
    )(*args)


def kernel(x, norm_mix_g, w_in, gdn_conv_w, gdn_a_log, gdn_dt_bias, gdn_norm_g, rwkv_mu, rwkv_w0, rwkv_w2, rwkv_a0, rwkv_a2, rwkv_g2, rwkv_k_k, rwkv_k_a, rwkv_r_k, rwkv_ln_g, rwkv_ln_b, w_branch_gdn, w_branch_rwkv, w_branch_sb, w_out, norm_ffn_g, ffn_w_gate, ffn_w_up, ffn_w_down, router_w, moe_w_gate, moe_w_up, moe_w_down, final_norm_g):
    b, t, d = x.shape
    n = b * t
    depth = w_in.shape[0]
    x2 = x.reshape(n, d).astype(F32)
    for layer in range(depth):
        w = w_in[layer]
        w_gdn = w[:, 0:2048].astype(BF16)
        w_ba = jnp.pad(w[:, 2048:2056], ((0, 0), (0, 120))).astype(BF16)
        w_rwkv = w[:, 2056:3848].astype(BF16)
        w_sb = w[:, 3848:5384].astype(BF16)
        w_gates = w[:, 5384:8456].astype(BF16)
        g_mix = norm_mix_g[layer].astype(F32)
        p_gdn, p_ba, p_rwkv = _normproj(x2, g_mix, [w_gdn, w_ba, w_rwkv], [F32, F32, F32])
        p_sb, p_gates = _normproj(x2, g_mix, [w_sb, w_gates], [F32, BF16])
        o_a = _gdn(p_gdn.reshape(b, t, -1), p_ba.reshape(b, t, -1), gdn_conv_w[layer], gdn_a_log[layer], gdn_dt_bias[layer],
                   gdn_norm_g[layer])
        o_b = _rwkv(p_rwkv.reshape(b, t, -1), rwkv_mu[layer], rwkv_w0[layer], rwkv_w2[layer],
                    rwkv_a0[layer], rwkv_a2[layer], rwkv_g2[layer], rwkv_k_k[layer], rwkv_k_a[layer],
                    rwkv_r_k[layer].reshape(-1), rwkv_ln_g[layer], rwkv_ln_b[layer])
        o_c = _stick_breaking(p_sb.reshape(b, t, -1))
        x2 = _merge(x2, o_a.reshape(n, -1), o_b.reshape(n, -1), o_c.reshape(n, -1), p_gates,
                    w_branch_gdn[layer].astype(BF16), w_branch_rwkv[layer].astype(BF16),
                    w_branch_sb[layer].astype(BF16), w_out[layer].astype(BF16))
        g_ffn = norm_ffn_g[layer].astype(F32)
        final_g = final_norm_g.astype(F32) if layer == depth - 1 else None
        i = layer // 2
        if layer % 2 == 0:
            x2 = _ffn(x2, g_ffn, ffn_w_gate[i].astype(BF16), ffn_w_up[i].astype(BF16),
                      ffn_w_down[i].astype(BF16), final_g=final_g)
        else:
            x2 = _moe(x2, g_ffn, router_w[i], moe_w_gate[i].astype(BF16), moe_w_up[i].astype(BF16),
                      moe_w_down[i].astype(BF16), final_g=final_g)
    return x2.reshape(b, t, d)
```

```python
import functools

import jax
import jax.numpy as jnp
from jax import lax
from jax.experimental import pallas as pl
from jax.experimental.pallas import tpu as pltpu

F32 = jnp.float32
BF16 = jnp.bfloat16

RMS_EPS = 1e-6
GN_EPS = 64e-5
L2_EPS = 1e-6

D_MODEL = 1024
GDN_HEADS = 4
GDN_HEAD_DIM = 128
GDN_WIDTH = 512
RWKV_WIDTH = 512
RWKV_HEAD_DIM = 64
RWKV_IN = 1792
SB_WIDTH = 512
SB_HEAD_DIM = 64
N_EXPERTS = 8
CHUNK = 64
LOG2E = 1.4426950408889634
SB_LOG2_CUTOFF = -160.0
SB_ROW_TILE = 128
GDN_CHUNKS_PER_STEP = 8
RWKV_CHUNKS_PER_STEP = 8
MOE_ROW_TILE = 512
VMEM_LIMIT_BYTES = 56 * 1024 * 1024


def _dot(a, b):
    return jnp.dot(a.astype(BF16), b.astype(BF16), preferred_element_type=F32)


def _dot_nt(a, b):
    return lax.dot_general(a.astype(BF16), b.astype(BF16), (((1,), (1,)), ((), ())),
                           preferred_element_type=F32)


def _split2(a):
    hi = a.astype(BF16)
    lo = (a - hi.astype(F32)).astype(BF16)
    return hi, lo


def _split3(a):
    hi = a.astype(BF16)
    r = a - hi.astype(F32)
    mid = r.astype(BF16)
    lo = (r - mid.astype(F32)).astype(BF16)
    return hi, mid, lo


def _dot3(a, b):
    ah, al = _split2(a)
    bh, bl = _split2(b)
    return (jnp.dot(ah, bh, preferred_element_type=F32)
            + jnp.dot(ah, bl, preferred_element_type=F32)
            + jnp.dot(al, bh, preferred_element_type=F32))


def _dot_exact_lhs(m, x, parts=3):
    xs = _split3(x) if parts == 3 else _split2(x)
    out = jnp.dot(m, xs[0], preferred_element_type=F32)
    for p in xs[1:]:
        out = out + jnp.dot(m, p, preferred_element_type=F32)
    return out


def _dot_exact_rhs(x, m, parts=2):
    xs = _split3(x) if parts == 3 else _split2(x)
    out = jnp.dot(xs[0], m, preferred_element_type=F32)
    for p in xs[1:]:
        out = out + jnp.dot(p, m, preferred_element_type=F32)
    return out


def _iota2(shape, dim):
    return lax.broadcasted_iota(jnp.int32, shape, dim)


def _eye(n, dtype=F32):
    return (_iota2((n, n), 0) == _iota2((n, n), 1)).astype(dtype)


def _softplus(x):
    return jnp.maximum(x, 0.0) + jnp.log(1.0 + jnp.exp(-jnp.abs(x)))


def _sigmoid(x):
    return 1.0 / (1.0 + jnp.exp(-x))


def _silu(x):
    return x * _sigmoid(x)


def _rms(x, g):
    return x * lax.rsqrt(jnp.mean(x * x, axis=-1, keepdims=True) + RMS_EPS) * g


def _nilpotent_inverse(n, eye, dot):
    t = eye + n
    x = n
    for _ in range(5):
        x = dot(x, x)
        t = t + dot(t, x)
    return t


def _nilpotent_inverse_many(ns, eye, dot):
    ts = [eye + n for n in ns]
    xs = list(ns)
    for _ in range(5):
        xs = [dot(x, x) for x in xs]
        ts = [t + dot(t, x) for t, x in zip(ts, xs)]
    return ts


def _chunk_cumsum(x):
    i = _iota2((128, 128), 0)
    j = _iota2((128, 128), 1)
    m = ((j <= i) & ((i >> 6) == (j >> 6))).astype(BF16)
    return jnp.concatenate(
        [_dot_exact_lhs(m, x[r:r + 128]) for r in range(0, x.shape[0], 128)], axis=0)


def _head_sum(x):
    i = _iota2((128, 128), 0)
    j = _iota2((128, 128), 1)
    m = ((i >> 6) == (j >> 6)).astype(BF16)
    return jnp.concatenate(
        [_dot_exact_rhs(x[:, c:c + 128], m) for c in range(0, x.shape[1], 128)], axis=1)


def _normproj_kernel(x_ref, g_ref, *refs, n_out):
    w_refs = refs[:n_out]
    o_refs = refs[n_out:]
    hn = _rms(x_ref[...], g_ref[...]).astype(BF16)
    for w_ref, o_ref in zip(w_refs, o_refs):
        o_ref[...] = jnp.dot(hn, w_ref[...], preferred_element_type=F32).astype(o_ref.dtype)


def _normproj(x2d, g, weights, out_dtypes, tm=512):
    n, d = x2d.shape
    n_out = len(weights)
    in_specs = [pl.BlockSpec((tm, d), lambda i: (i, 0)), pl.BlockSpec((1, d), lambda i: (0, 0))]
    in_specs += [pl.BlockSpec(w.shape, lambda i: (0, 0), pipeline_mode=pl.Buffered(1)) for w in weights]
    out_specs = [pl.BlockSpec((tm, w.shape[1]), lambda i: (i, 0)) for w in weights]
    out_shape = [jax.ShapeDtypeStruct((n, w.shape[1]), dt) for w, dt in zip(weights, out_dtypes)]
    return pl.pallas_call(
        functools.partial(_normproj_kernel, n_out=n_out),
        grid=(n // tm,),
        in_specs=in_specs,
        out_specs=out_specs,
        out_shape=out_shape,
        compiler_params=pltpu.CompilerParams(
            dimension_semantics=("arbitrary",), vmem_limit_bytes=VMEM_LIMIT_BYTES),
        name="normproj",
    )(x2d, g.reshape(1, d), *weights)


def _gdn_kernel(qkv_ref, z_ref, ba_ref, cw_ref, alog_ref, dtb_ref, ng_ref, o_ref,
                ext_s, state_s, q_s, k_s, v_s, beta_s, gc_s, p_s, qq_s, r_s, zz_s, oc_s, *, tt):
    nc = tt // CHUNK
    w3 = 3 * GDN_WIDTH
    t = pl.program_id(1)

    @pl.when(t == 0)
    def _():
        ext_s[0:8, :] = jnp.zeros((8, w3), F32)
        state_s[...] = jnp.zeros_like(state_s)

    raw = qkv_ref[0]
    ext_s[8:8 + tt, :] = raw
    cw = cw_ref[...]
    y = raw * cw[3:4, :]
    for i in range(3):
        y = y + ext_s[5 + i:5 + i + tt, :] * cw[i:i + 1, :]
    ext_s[0:8, :] = raw[tt - 8:tt, :]
    y = _silu(y)

    for h in range(GDN_HEADS):
        sl = slice(128 * h, 128 * h + 128)
        qh = y[:, 128 * h:128 * h + 128]
        kh = y[:, GDN_WIDTH + 128 * h:GDN_WIDTH + 128 * h + 128]
        vh = y[:, 2 * GDN_WIDTH + 128 * h:2 * GDN_WIDTH + 128 * h + 128]
        qh = qh * lax.rsqrt(jnp.sum(qh * qh, axis=-1, keepdims=True) + L2_EPS) * (GDN_HEAD_DIM ** -0.5)
        kh = kh * lax.rsqrt(jnp.sum(kh * kh, axis=-1, keepdims=True) + L2_EPS)
        q_s[:, sl] = qh
        k_s[:, sl] = kh
        v_s[:, sl] = vh

    ec = _iota2((128, 2 * GDN_WIDTH), 0)
    el = _iota2((128, 2 * GDN_WIDTH), 1)
    ba = _dot_exact_rhs(ba_ref[0], (ec == (el >> 7)).astype(BF16), parts=3)
    beta_s[...] = _sigmoid(ba[:, 0:GDN_WIDTH])
    g = -jnp.exp(alog_ref[...]) * _softplus(ba[:, GDN_WIDTH:2 * GDN_WIDTH] + dtb_ref[...])
    gc_s[...] = _chunk_cumsum(g)

    eye64 = _eye(CHUNK)
    eye128 = _eye(128)
    ii = _iota2((CHUNK, CHUNK), 0)
    jj = _iota2((CHUNK, CHUNK), 1)

    def chunk_body(ci, carry):
        probs = [(ci * GDN_CHUNKS_PER_STEP + u, h) for u in range(GDN_CHUNKS_PER_STEP)
                 for h in range(GDN_HEADS)]
        rows = [pl.ds(pl.multiple_of(c * CHUNK, CHUNK), CHUNK) for c, _ in probs]
        lanes = [slice(128 * h, 128 * h + 128) for _, h in probs]
        idx = range(len(probs))
        q = [q_s[rows[i], lanes[i]] for i in idx]
        k = [k_s[rows[i], lanes[i]] for i in idx]
        gcc = [gc_s[rows[i], lanes[i]] for i in idx]
        gl = [g[CHUNK - 1:CHUNK, :] for g in gcc]
        gcr = [g.T[0:CHUNK, :] for g in gcc]
        dec_incl = [jnp.exp(jnp.where(ii >= jj, gcc[i][:, 0:CHUNK] - gcr[i], -jnp.inf)) for i in idx]
        kb = [k[i] * beta_s[rows[i], lanes[i]] for i in idx]
        a_mat = [_dot_nt(kb[i], k[i]) for i in idx]
        attn = [_dot_nt(q[i], k[i]) * dec_incl[i] for i in idx]
        kdt = [(k[i] * jnp.exp(gl[i] - gcc[i])).T for i in idx]
        tinv = _nilpotent_inverse_many(
            [-a_mat[i] * jnp.where(ii > jj, dec_incl[i], 0.0) for i in idx], eye64, _dot)
        u = [_dot(tinv[i], v_s[rows[i], lanes[i]] * beta_s[rows[i], lanes[i]]) for i in idx]
        w = [_dot(tinv[i], kb[i] * jnp.exp(gcc[i])) for i in idx]
        for i, (c, h) in enumerate(probs):
            p_s[c, h] = jnp.exp(gl[i]) * eye128 - _dot(kdt[i], w[i])
        for i, (c, h) in enumerate(probs):
            qq_s[c, h] = _dot(kdt[i], u[i])
        for i, (c, h) in enumerate(probs):
            r_s[c, h] = q[i] * jnp.exp(gcc[i]) - _dot(attn[i], w[i])
        for i, (c, h) in enumerate(probs):
            zz_s[c, h] = _dot(attn[i], u[i])
        return carry

    lax.fori_loop(0, nc // GDN_CHUNKS_PER_STEP, chunk_body, 0)

    def scan_body(c, carry):
        rows = pl.ds(pl.multiple_of(c * CHUNK, CHUNK), CHUNK)
        s = [state_s[h] for h in range(GDN_HEADS)]
        s_new = [_dot3(p_s[c, h], s[h]) for h in range(GDN_HEADS)]
        o = [_dot(r_s[c, h], s[h]) for h in range(GDN_HEADS)]
        for h in range(GDN_HEADS):
            state_s[h] = s_new[h] + qq_s[c, h]
            oc_s[rows, 128 * h:128 * h + 128] = o[h] + zz_s[c, h]
        return carry

    lax.fori_loop(0, nc, scan_body, 0)

    o = oc_s[...]
    z = z_ref[0]
    ng = ng_ref[...]
    for h in range(GDN_HEADS):
        sl = slice(128 * h, 128 * h + 128)
        oh = o[:, sl]
        oh = oh * lax.rsqrt(jnp.mean(oh * oh, axis=-1, keepdims=True) + RMS_EPS) * ng[:, sl]
        o_ref[0, :, sl] = (oh * _silu(z[:, sl])).astype(o_ref.dtype)


def _gdn(proj_a, proj_ba, conv_w, a_log, dt_bias, norm_g, tt=512):
    b, t, _ = proj_a.shape
    nc = tt // CHUNK
    w3 = 3 * GDN_WIDTH
    rep = lambda p: jnp.repeat(p.astype(F32), 128).reshape(1, GDN_WIDTH)
    small = lambda shape: pl.BlockSpec(shape, lambda i, j: (0, 0))
    return pl.pallas_call(
        functools.partial(_gdn_kernel, tt=tt),
        grid=(b, t // tt),
        in_specs=[
            pl.BlockSpec((1, tt, w3), lambda i, j: (i, j, 0)),
            pl.BlockSpec((1, tt, GDN_WIDTH), lambda i, j: (i, j, 3)),
            pl.BlockSpec((1, tt, 128), lambda i, j: (i, j, 0)),
            small((4, w3)), small((1, GDN_WIDTH)), small((1, GDN_WIDTH)), small((1, GDN_WIDTH)),
        ],
        out_specs=pl.BlockSpec((1, tt, GDN_WIDTH), lambda i, j: (i, j, 0)),
        out_shape=jax.ShapeDtypeStruct((b, t, GDN_WIDTH), BF16),
        scratch_shapes=[
            pltpu.VMEM((tt + 8, w3), F32),
            pltpu.VMEM((GDN_HEADS, 128, 128), F32),
            pltpu.VMEM((tt, GDN_WIDTH), F32), pltpu.VMEM((tt, GDN_WIDTH), F32),
            pltpu.VMEM((tt, GDN_WIDTH), F32), pltpu.VMEM((tt, GDN_WIDTH), F32),
            pltpu.VMEM((tt, GDN_WIDTH), F32),
            pltpu.VMEM((nc, GDN_HEADS, 128, 128), F32), pltpu.VMEM((nc, GDN_HEADS, 128, 128), F32),
            pltpu.VMEM((nc, GDN_HEADS, CHUNK, 128), F32), pltpu.VMEM((nc, GDN_HEADS, CHUNK, 128), F32),
            pltpu.VMEM((tt, GDN_WIDTH), F32),
        ],
        compiler_params=pltpu.CompilerParams(
            dimension_semantics=("arbitrary", "arbitrary"), vmem_limit_bytes=VMEM_LIMIT_BYTES),
        name="gdn",
    )(proj_a, proj_a, proj_ba, conv_w.astype(F32), rep(a_log), rep(dt_bias),
      jnp.tile(norm_g.astype(F32), GDN_HEADS).reshape(1, GDN_WIDTH))


def _rwkv_kernel(h_ref, mu_ref, w0_ref, a0_ref, kk_ref, ka_ref, rk_ref, lng_ref, lnb_ref,
                 wa_ref, g2_ref, o_ref,
                 ext_s, state_s, r_s, kn_s, k2_s, v_s, a_s, lw_s, lc_s, p_s, qq_s, rh_s, yc_s, y_s,
                 *, tt):
    nc = tt // CHUNK
    npair = RWKV_WIDTH // 128
    t = pl.program_id(1)

    @pl.when(t == 0)
    def _():
        ext_s[0:8, :] = jnp.zeros((8, RWKV_IN), F32)
        state_s[...] = jnp.zeros_like(state_s)

    raw = h_ref[0]
    ext_s[8:8 + tt, :] = raw
    prev = ext_s[7:7 + tt, :]
    ext_s[0:8, :] = raw[tt - 8:tt, :]
    hl = raw + (prev - raw) * mu_ref[...]
    r = hl[:, 0:512]
    k = hl[:, 512:1024]
    v = hl[:, 1024:1536]
    xwa = hl[:, 1536:1664]
    xg = hl[:, 1664:1792]
    lane128 = _iota2((1, 128), 1)
    xwa = jnp.where(lane128 < 64, jnp.tanh(xwa), xwa)
    lora = _dot3(xwa, wa_ref[...])
    w_log = -_softplus(-(w0_ref[...] + lora[:, 0:512])) - 0.5
    lw = -jnp.exp(w_log)
    a = _sigmoid(a0_ref[...] + lora[:, 512:1024])
    gate = _dot3(_sigmoid(xg), g2_ref[...])
    kk = k * kk_ref[...]
    k2 = k * (1.0 + (a - 1.0) * ka_ref[...])
    kn = kk * lax.rsqrt(_head_sum(kk * kk) + L2_EPS)
    r_s[...] = r
    kn_s[...] = kn
    k2_s[...] = k2
    v_s[...] = v
    a_s[...] = a
    lw_s[...] = lw
    lc_s[...] = _chunk_cumsum(lw)

    eye128 = _eye(128)
    row = _iota2((CHUNK, 128), 0)
    lane = _iota2((CHUNK, 128), 1)
    col = lane & (CHUNK - 1)
    strict = row > col
    incl = row >= col
    eye2 = (row == col).astype(F32)
    head0 = lane < CHUNK
    head0_wide = (_iota2((CHUNK, 256), 1) & 127) < CHUNK
    bi = _iota2((128, 128), 0)
    bj = _iota2((128, 128), 1)
    blockdiag = (bi < 64) == (bj < 64)

    def per_head_rows(x, mask=head0):
        return jnp.concatenate([jnp.where(mask, x, 0.0), jnp.where(mask, 0.0, x)], axis=0)

    def chunk_body(ci, carry):
        units = [(ci * RWKV_CHUNKS_PER_STEP + u, p) for u in range(RWKV_CHUNKS_PER_STEP) for p in range(npair)]
        nu = range(len(units))
        rows = [pl.ds(pl.multiple_of(c * CHUNK, CHUNK), CHUNK) for c, _ in units]
        sls = [slice(128 * p, 128 * p + 128) for _, p in units]
        lcc = [lc_s[rows[u], sls[u]] for u in nu]
        vc = [v_s[rows[u], sls[u]] for u in nu]
        lcl = [x[CHUNK - 1:CHUNK, :] for x in lcc]
        ginv = [jnp.exp(-x) for x in lcc]
        gend = [jnp.exp(lcl[u] - lcc[u]) for u in nu]
        kna = [kn_s[rows[u], sls[u]] * a_s[rows[u], sls[u]] for u in nu]
        at = [-kn_s[rows[u], sls[u]] * jnp.exp(lcc[u] - lw_s[rows[u], sls[u]]) for u in nu]
        rt = [r_s[rows[u], sls[u]] * jnp.exp(lcc[u]) for u in nu]
        ar = [jnp.concatenate([at[u], rt[u]], axis=0) for u in nu]
        xb = [_dot_nt(ar[u], per_head_rows(kna[u] * ginv[u])) for u in nu]
        xk = [_dot_nt(ar[u], per_head_rows(k2_s[rows[u], sls[u]] * ginv[u])) for u in nu]
        a_ab = [jnp.where(strict, x[0:CHUNK], 0.0) for x in xb]
        a_rb = [jnp.where(incl, x[CHUNK:2 * CHUNK], 0.0) for x in xb]
        a_ak = [jnp.where(strict, x[0:CHUNK], 0.0) for x in xk]
        a_rk = [jnp.where(incl, x[CHUNK:2 * CHUNK], 0.0) for x in xk]
        vrows = [per_head_rows(vc[u]) for u in nu]
        akv = [_dot(a_ak[u], vrows[u]) for u in nu]
        arkv = [_dot(a_rk[u], vrows[u]) for u in nu]
        bbt = [(kna[u] * gend[u]).T for u in nu]
        kbt = [(k2_s[rows[u], sls[u]] * gend[u]).T for u in nu]
        ts = [eye2 + a for a in a_ab]
        xs = [_dot(x, per_head_rows(x)) for x in a_ab]
        for step in range(5):
            if step < 4:
                xt = [_dot(jnp.concatenate([x, t], axis=0), per_head_rows(x)) for x, t in zip(xs, ts)]
                xs = [y[0:CHUNK] for y in xt]
                ts = [t + y[CHUNK:2 * CHUNK] for t, y in zip(ts, xt)]
            else:
                ts = [t + _dot(t, per_head_rows(x)) for t, x in zip(ts, xs)]
        au = [_dot(ts[u], per_head_rows(jnp.concatenate([at[u], akv[u]], axis=1), head0_wide)) for u in nu]
        ry = [_dot(a_rb[u], per_head_rows(au[u], head0_wide)) for u in nu]
        kv = [_dot(kbt[u], vc[u]) for u in nu]
        pq = [_dot(bbt[u], au[u]) for u in nu]
        for u, (c, p) in enumerate(units):
            rh_s[c, p] = rt[u] + ry[u][:, 0:128]
            yc_s[c, p] = ry[u][:, 128:256] + arkv[u]
            p_s[c, p] = jnp.where(blockdiag, pq[u][:, 0:128], 0.0) + eye128 * jnp.exp(lcl[u])
            qq_s[c, p] = jnp.where(blockdiag, pq[u][:, 128:256] + kv[u], 0.0)
        return carry

    lax.fori_loop(0, nc // RWKV_CHUNKS_PER_STEP, chunk_body, 0)

    def scan_body(c, carry):
        rows = pl.ds(pl.multiple_of(c * CHUNK, CHUNK), CHUNK)
        s = [state_s[p] for p in range(npair)]
        s_new = [_dot3(p_s[c, p], s[p]) for p in range(npair)]
        y = [_dot(rh_s[c, p], s[p]) for p in range(npair)]
        for p in range(npair):
            state_s[p] = s_new[p] + qq_s[c, p]
            y_s[rows, 128 * p:128 * p + 128] = y[p] + yc_s[c, p]
        return carry

    lax.fori_loop(0, nc, scan_body, 0)

    y = y_s[...]
    mean = _head_sum(y) * (1.0 / RWKV_HEAD_DIM)
    yc = y - mean
    var = _head_sum(yc * yc) * (1.0 / RWKV_HEAD_DIM)
    yn = yc * lax.rsqrt(var + GN_EPS) * lng_ref[...] + lnb_ref[...]
    bonus = _head_sum(r * k2 * rk_ref[...]) * v
    o_ref[0] = ((yn + bonus) * gate).astype(o_ref.dtype)


def _dot3_nt(a, b):
    ah, al = _split2(a)
    bh, bl = _split2(b)
    dn = (((1,), (1,)), ((), ()))
    return (lax.dot_general(ah, bh, dn, preferred_element_type=F32)
            + lax.dot_general(ah, bl, dn, preferred_element_type=F32)
            + lax.dot_general(al, bh, dn, preferred_element_type=F32))


def _rwkv(h, mu, w0, w2, a0, a2, g2, k_k, k_a, r_k, ln_g, ln_b, tt=512):
    b, t, _ = h.shape
    nc = tt // CHUNK
    npair = RWKV_WIDTH // 128
    row = lambda p: p.astype(F32).reshape(1, -1)
    wa = jnp.zeros((128, 2 * RWKV_WIDTH), F32)
    wa = wa.at[0:64, 0:RWKV_WIDTH].set(w2.astype(F32)).at[64:128, RWKV_WIDTH:].set(a2.astype(F32))
    small = lambda shape: pl.BlockSpec(shape, lambda i, j: (0, 0))
    vec = small((1, RWKV_WIDTH))
    return pl.pallas_call(
        functools.partial(_rwkv_kernel, tt=tt),
        grid=(b, t // tt),
        in_specs=[pl.BlockSpec((1, tt, RWKV_IN), lambda i, j: (i, j, 0)), small((1, RWKV_IN)),
                  vec, vec, vec, vec, vec, vec, vec,
                  small((128, 2 * RWKV_WIDTH)), small((128, RWKV_WIDTH))],
        out_specs=pl.BlockSpec((1, tt, RWKV_WIDTH), lambda i, j: (i, j, 0)),
        out_shape=jax.ShapeDtypeStruct((b, t, RWKV_WIDTH), BF16),
        scratch_shapes=[
            pltpu.VMEM((tt + 8, RWKV_IN), F32),
            pltpu.VMEM((npair, 128, 128), F32),
        ] + [pltpu.VMEM((tt, RWKV_WIDTH), F32)] * 7 + [
            pltpu.VMEM((nc, npair, 128, 128), F32), pltpu.VMEM((nc, npair, 128, 128), F32),
            pltpu.VMEM((nc, npair, CHUNK, 128), F32), pltpu.VMEM((nc, npair, CHUNK, 128), F32),
            pltpu.VMEM((tt, RWKV_WIDTH), F32),
        ],
        compiler_params=pltpu.CompilerParams(
            dimension_semantics=("arbitrary", "arbitrary"), vmem_limit_bytes=VMEM_LIMIT_BYTES),
        name="rwkv7",
    )(h, row(mu), row(w0), row(a0), row(k_k), row(k_a), row(r_k), row(ln_g), row(ln_b),
      wa, g2.astype(F32))


def _sb_kernel(q_ref, k_ref, v_ref, o_ref, acc_s, aux_s, *, bq, bk):
    qi = pl.program_id(2)
    nsub = bq // bk
    q = q_ref[0] * (SB_HEAD_DIM ** -0.5 * LOG2E)
    lane = _iota2((1, 128), 1)
    qm = [jnp.where(lane < 64, q, 0.0).astype(BF16), jnp.where(lane >= 64, q, 0.0).astype(BF16)]
    ti = _iota2((2 * bk, bk), 0)
    tj = _iota2((2 * bk, bk), 1)
    cum_mat = ((ti & (bk - 1)) > tj).astype(BF16)
    acc_s[...] = jnp.zeros_like(acc_s)
    aux_s[...] = jnp.zeros_like(aux_s)

    def load_kv(kb):
        start = pl.multiple_of(kb * bk, bk)
        return (start, k_ref[0, pl.ds(start, bk), :].astype(BF16), v_ref[0, pl.ds(start, bk), :].astype(BF16))

    def stage_scores(item):
        (start, k, v), ra, nr, j, masked = item
        z = lax.dot_general(qm[j][ra:ra + nr], k, (((1,), (1,)), ((), ())), preferred_element_type=F32)
        return z

    def stage_cumsum(item, z):
        (start, k, v), ra, nr, j, masked = item
        rows = slice(ra, ra + nr)
        neg_abs = lax.bitcast_convert_type(
            lax.bitcast_convert_type(z, jnp.uint32) | jnp.uint32(0x80000000), F32)
        lsig = jnp.minimum(z, 0.0) - jnp.log(1.0 + jnp.exp2(neg_abs)) * LOG2E
        l1 = lsig - z
        msk = None
        if masked:
            msk = (start + _iota2((nr, bk), 1)) < (qi * bq + ra + _iota2((nr, bk), 0))
            l1 = jnp.where(msk, l1, 0.0)
        l1_hi = l1.astype(BF16)
        l1_lo = (l1 - l1_hi.astype(F32)).astype(BF16)
        cr = (jnp.dot(jnp.concatenate([l1_hi, l1_lo], axis=1), cum_mat, preferred_element_type=F32)
              + aux_s[j, rows, 0:1])
        aux_s[j, rows, :] = cr + l1
        return lsig, cr, msk

    def stage_values(item, state):
        (start, k, v), ra, nr, j, masked = item
        lsig, cr, msk = state
        att = jnp.exp2(lsig + cr)
        if masked:
            att = jnp.where(msk, att, 0.0)
        acc_s[j, ra:ra + nr, :] += jnp.dot(att.astype(BF16), v, preferred_element_type=F32)

    def run_items(items):
        n = len(items)
        zs = {}
        states = {}
        for s in range(n + 2):
            if s < n:
                zs[s] = stage_scores(items[s])
            if 0 <= s - 1 < n:
                states[s - 1] = stage_cumsum(items[s - 1], zs.pop(s - 1))
            if 0 <= s - 2 < n:
                stage_values(items[s - 2], states.pop(s - 2))

    def max_carry(ra, nr):
        r = jnp.maximum(aux_s[0, ra:ra + nr, :], aux_s[1, ra:ra + nr, :])
        r = jnp.max(jnp.where(lane == 0, r, -jnp.inf), axis=0, keepdims=True)
        return jnp.max(r, axis=1, keepdims=True)[0, 0]

    items = []
    for d in range(nsub - 1, -1, -1):
        kv = load_kv(qi * nsub + d)
        ra = d * bk
        while ra < bq:
            nr = SB_ROW_TILE if (bq - ra) % (2 * SB_ROW_TILE) else 2 * SB_ROW_TILE
            items += [(kv, ra, nr, 0, True), (kv, ra, nr, 1, True)]
            ra += nr
    run_items(items)

    def cond(c):
        kb, rmax = c
        return (kb >= 0) & (rmax > SB_LOG2_CUTOFF)

    for ra in range(0, bq, 2 * SB_ROW_TILE):
        nr = 2 * SB_ROW_TILE

        def body(c, ra=ra, nr=nr):
            kb, _ = c
            items = []
            for kv in (load_kv(kb), load_kv(kb - 1)):
                items += [(kv, ra, nr, 0, False), (kv, ra, nr, 1, False)]
            run_items(items)
            return kb - 2, max_carry(ra, nr)

        lax.while_loop(cond, body, (qi * nsub - 1, max_carry(ra, nr)))
    o_ref[0] = jnp.where(lane < 64, acc_s[0], acc_s[1]).astype(o_ref.dtype)


def _stick_breaking(h_sb, bq=512, bk=128):
    b, t, _ = h_sb.shape
    npair = SB_WIDTH // 128
    return pl.pallas_call(
        functools.partial(_sb_kernel, bq=bq, bk=bk),
        grid=(b, npair, t // bq),
        in_specs=[
            pl.BlockSpec((1, bq, 128), lambda i, p, j: (i, j, p)),
            pl.BlockSpec((1, t, 128), lambda i, p, j: (i, 0, npair + p)),
            pl.BlockSpec((1, t, 128), lambda i, p, j: (i, 0, 2 * npair + p)),
        ],
        out_specs=pl.BlockSpec((1, bq, 128), lambda i, p, j: (i, j, p)),
        out_shape=jax.ShapeDtypeStruct((b, t, SB_WIDTH), BF16),
        scratch_shapes=[pltpu.VMEM((2, bq, 128), F32), pltpu.VMEM((2, bq, bk), F32)],
        compiler_params=pltpu.CompilerParams(
            dimension_semantics=("arbitrary", "arbitrary", "arbitrary"),
            vmem_limit_bytes=VMEM_LIMIT_BYTES),
        name="stickbreak",
    )(h_sb, h_sb, h_sb)


def _merge_kernel(x_ref, oa_ref, ob_ref, oc_ref, gt_ref, wa_ref, wb_ref, wc_ref, wo_ref, o_ref):
    d = D_MODEL
    g = _sigmoid(gt_ref[...].astype(F32))
    m = (g[:, 0:d] * _dot(oa_ref[...], wa_ref[...])
         + g[:, d:2 * d] * _dot(ob_ref[...], wb_ref[...])
         + g[:, 2 * d:3 * d] * _dot(oc_ref[...], wc_ref[...]))
    o_ref[...] = x_ref[...] + _dot(m, wo_ref[...])


def _merge(x2d, oa, ob, oc, gates, wa, wb, wc, wo, tm=512):
    n, d = x2d.shape
    rowspec = lambda w: pl.BlockSpec((tm, w), lambda i: (i, 0))
    full = lambda w: pl.BlockSpec(w.shape, lambda i: (0, 0))
    return pl.pallas_call(
        _merge_kernel,
        grid=(n // tm,),
        in_specs=[rowspec(d), rowspec(512), rowspec(512), rowspec(512), rowspec(3 * d),
                  full(wa), full(wb), full(wc), full(wo)],
        out_specs=rowspec(d),
        out_shape=jax.ShapeDtypeStruct((n, d), F32),
        compiler_params=pltpu.CompilerParams(
            dimension_semantics=("arbitrary",), vmem_limit_bytes=VMEM_LIMIT_BYTES),
        name="merge",
    )(x2d, oa, ob, oc, gates, wa, wb, wc, wo)


def _router_kernel(x_ref, g_ref, rw_ref, comb_ref, idx_ref, cnt_ref, run_s):
    @pl.when(pl.program_id(0) == 0)
    def _():
        run_s[...] = jnp.zeros_like(run_s)

    hn = _rms(x_ref[...], g_ref[...])
    logits = _dot3(hn, rw_ref[...])
    tm = logits.shape[0]
    lane = _iota2(logits.shape, 1)
    m1 = jnp.max(logits, axis=-1, keepdims=True)
    i1 = jnp.min(jnp.where(logits == m1, lane, N_EXPERTS), axis=-1, keepdims=True)
    sel1 = lane == i1
    rest = jnp.where(sel1, -jnp.inf, logits)
    m2 = jnp.max(rest, axis=-1, keepdims=True)
    i2 = jnp.min(jnp.where(rest == m2, lane, N_EXPERTS), axis=-1, keepdims=True)
    sel2 = lane == i2
    e2 = jnp.exp(m2 - m1)
    w1 = 1.0 / (1.0 + e2)
    comb_ref[...] = jnp.where(sel1, w1, 0.0) + jnp.where(sel2, e2 * w1, 0.0)

    member = (sel1 | sel2).astype(BF16)
    before = (_iota2((tm, tm), 0) > _iota2((tm, tm), 1)).astype(BF16)
    rank = jnp.dot(before, member, preferred_element_type=F32) + run_s[...]
    r1 = jnp.sum(jnp.where(sel1, rank, 0.0), axis=-1, keepdims=True).astype(jnp.int32)
    r2 = jnp.sum(jnp.where(sel2, rank, 0.0), axis=-1, keepdims=True).astype(jnp.int32)
    run_s[...] += jnp.sum(member.astype(F32), axis=0, keepdims=True)
    cnt_ref[...] = run_s[...]
    lane128 = _iota2((tm, 128), 1)
    idx_ref[...] = jnp.where(lane128 == 0, i1, jnp.where(lane128 == 1, i2,
                             jnp.where(lane128 == 2, r1, jnp.where(lane128 == 3, r2, 0))))


def _router(x2d, g, rw, tm=512):
    n, d = x2d.shape
    return pl.pallas_call(
        _router_kernel,
        grid=(n // tm,),
        in_specs=[pl.BlockSpec((tm, d), lambda i: (i, 0)), pl.BlockSpec((1, d), lambda i: (0, 0)),
                  pl.BlockSpec(rw.shape, lambda i: (0, 0))],
        out_specs=[pl.BlockSpec((tm, N_EXPERTS), lambda i: (i, 0)),
                   pl.BlockSpec((tm, 128), lambda i: (i, 0)),
                   pl.BlockSpec((1, N_EXPERTS), lambda i: (0, 0))],
        out_shape=[jax.ShapeDtypeStruct((n, N_EXPERTS), F32),
                   jax.ShapeDtypeStruct((n, 128), jnp.int32),
                   jax.ShapeDtypeStruct((1, N_EXPERTS), F32)],
        scratch_shapes=[pltpu.VMEM((1, N_EXPERTS), F32)],
        compiler_params=pltpu.CompilerParams(dimension_semantics=("arbitrary",)),
        name="router",
    )(x2d, g.reshape(1, d), rw.astype(F32))


def _row_copy(src_ref, src_row, dst_ref, dst_row, sem):
    return pltpu.make_async_copy(src_ref.at[pl.ds(src_row, 1), :], dst_ref.at[pl.ds(dst_row, 1), :], sem)


def _drain_rows(n_copies, src_ref, dst_ref, sem):
    def body(t, carry):
        _row_copy(src_ref, 0, dst_ref, 0, sem).wait()
        return carry

    lax.fori_loop(0, n_copies, body, 0, unroll=8)


def _dispatch_kernel(pos1_ref, pos2_ref, x_ref, g_ref, xs_in_ref, xs_ref, hn_s, sems):
    del xs_in_ref
    tm = hn_s.shape[1]
    i = pl.program_id(0)
    slot = i % 2
    hn_s[slot] = _rms(x_ref[...], g_ref[...])
    src = hn_s.at[slot]
    sem = sems.at[slot]

    def issue(t, carry):
        _row_copy(src, t, xs_ref, pos1_ref[t], sem).start()
        _row_copy(src, t, xs_ref, pos2_ref[t], sem).start()
        return carry

    lax.fori_loop(0, tm, issue, 0, unroll=8)

    @pl.when(i > 0)
    def _():
        _drain_rows(2 * tm, hn_s.at[1 - slot], xs_ref, sems.at[1 - slot])

    @pl.when(i == pl.num_programs(0) - 1)
    def _():
        _drain_rows(2 * tm, src, xs_ref, sem)


def _dispatch(x2d, g, pos1, pos2, n_rows, tm=512):
    n, d = x2d.shape
    smem = lambda: pl.BlockSpec((tm,), lambda i: (i,), memory_space=pltpu.SMEM)
    return pl.pallas_call(
        _dispatch_kernel,
        grid=(n // tm,),
        in_specs=[smem(), smem(), pl.BlockSpec((tm, d), lambda i: (i, 0)),
                  pl.BlockSpec((1, d), lambda i: (0, 0)), pl.BlockSpec(memory_space=pl.ANY)],
        out_specs=pl.BlockSpec(memory_space=pl.ANY),
        out_shape=jax.ShapeDtypeStruct((n_rows, d), F32),
        scratch_shapes=[pltpu.VMEM((2, tm, d), F32), pltpu.SemaphoreType.DMA((2,))],
        input_output_aliases={4: 0},
        compiler_params=pltpu.CompilerParams(dimension_semantics=("arbitrary",)),
        name="moe_dispatch",
    )(pos1, pos2, x2d, g.reshape(1, d), jnp.zeros((n_rows, d), F32))


def _grouped_ffn_kernel(te_ref, nt_ref, xs_ref, wg_ref, wu_ref, wd_ref, o_ref, xb_s, acc_s):
    i = pl.program_id(0)
    f = pl.program_id(1)

    @pl.when(f == 0)
    def _():
        xb_s[...] = xs_ref[...].astype(BF16)
        acc_s[...] = jnp.zeros_like(acc_s)

    @pl.when(i < nt_ref[0])
    def _():
        xb = xb_s[...]
        act = _silu(jnp.dot(xb, wg_ref[...], preferred_element_type=F32)) * jnp.dot(
            xb, wu_ref[...], preferred_element_type=F32)
        acc_s[...] += jnp.dot(act.astype(BF16), wd_ref[...], preferred_element_type=F32)

    @pl.when(f == pl.num_programs(1) - 1)
    def _():
        o_ref[...] = acc_s[...]


def _grouped_ffn(xs, tile_expert, n_tiles, wg, wu, wd, tm, tf=1792):
    n_rows, d = xs.shape
    dff = wg.shape[2]
    nf = dff // tf
    fsel = lambda i, f, te, nt: jnp.where(i < nt[0], f, nf - 1)
    return pl.pallas_call(
        _grouped_ffn_kernel,
        grid_spec=pltpu.PrefetchScalarGridSpec(
            num_scalar_prefetch=2,
            grid=(n_rows // tm, nf),
            in_specs=[pl.BlockSpec((tm, d), lambda i, f, te, nt: (i, 0)),
                      pl.BlockSpec((None, d, tf), lambda i, f, te, nt: (te[i], 0, fsel(i, f, te, nt))),
                      pl.BlockSpec((None, d, tf), lambda i, f, te, nt: (te[i], 0, fsel(i, f, te, nt))),
                      pl.BlockSpec((None, tf, d), lambda i, f, te, nt: (te[i], fsel(i, f, te, nt), 0))],
            out_specs=pl.BlockSpec((tm, d), lambda i, f, te, nt: (i, 0)),
            scratch_shapes=[pltpu.VMEM((tm, d), BF16), pltpu.VMEM((tm, d), F32)]),
        out_shape=jax.ShapeDtypeStruct((n_rows, d), F32),
        compiler_params=pltpu.CompilerParams(
            dimension_semantics=("arbitrary", "arbitrary"), vmem_limit_bytes=VMEM_LIMIT_BYTES),
        name="moe_grouped_ffn",
    )(tile_expert, n_tiles, xs, wg, wu, wd)


def _combine_kernel(*refs, final):
    if final:
        pos1_ref, pos2_ref, npos1_ref, npos2_ref, x_ref, comb_ref, fg_ref, ys_ref, o_ref, y1_s, y2_s, sems = refs
    else:
        pos1_ref, pos2_ref, npos1_ref, npos2_ref, x_ref, comb_ref, ys_ref, o_ref, y1_s, y2_s, sems = refs
    tm = y1_s.shape[1]
    i = pl.program_id(0)
    slot = i % 2

    def gather(p1_ref, p2_ref, sl):
        def issue(t, carry):
            _row_copy(ys_ref, p1_ref[t], y1_s.at[sl], t, sems.at[sl]).start()
            _row_copy(ys_ref, p2_ref[t], y2_s.at[sl], t, sems.at[sl]).start()
            return carry

        lax.fori_loop(0, tm, issue, 0, unroll=8)

    @pl.when(i == 0)
    def _():
        gather(pos1_ref, pos2_ref, 0)

    @pl.when(i < pl.num_programs(0) - 1)
    def _():
        gather(npos1_ref, npos2_ref, 1 - slot)

    _drain_rows(2 * tm, ys_ref, y1_s.at[slot], sems.at[slot])
    comb = comb_ref[...]
    w1 = jnp.max(comb, axis=-1, keepdims=True)
    w2 = jnp.sum(comb, axis=-1, keepdims=True) - w1
    out = x_ref[...] + w1 * y1_s[slot] + w2 * y2_s[slot]
    if final:
        out = _rms(out, fg_ref[...])
    o_ref[...] = out


def _combine(x2d, comb, pos1, pos2, ys, final_g=None, tm=512):
    n, d = x2d.shape
    final = final_g is not None
    nsteps = n // tm
    smem = lambda: pl.BlockSpec((tm,), lambda i: (i,), memory_space=pltpu.SMEM)
    smem_next = lambda: pl.BlockSpec((tm,), lambda i: (jnp.minimum(i + 1, nsteps - 1),),
                                     memory_space=pltpu.SMEM)
    in_specs = [smem(), smem(), smem_next(), smem_next(), pl.BlockSpec((tm, d), lambda i: (i, 0)),
                pl.BlockSpec((tm, N_EXPERTS), lambda i: (i, 0))]
    args = [pos1, pos2, pos1, pos2, x2d, comb]
    if final:
        in_specs.append(pl.BlockSpec((1, d), lambda i: (0, 0)))
        args.append(final_g.reshape(1, d))
    in_specs.append(pl.BlockSpec(memory_space=pl.ANY))
    args.append(ys)
    return pl.pallas_call(
        functools.partial(_combine_kernel, final=final),
        grid=(nsteps,),
        in_specs=in_specs,
        out_specs=pl.BlockSpec((tm, d), lambda i: (i, 0)),
        out_shape=jax.ShapeDtypeStruct((n, d), F32),
        scratch_shapes=[pltpu.VMEM((2, tm, d), F32), pltpu.VMEM((2, tm, d), F32),
                        pltpu.SemaphoreType.DMA((2,))],
        compiler_params=pltpu.CompilerParams(dimension_semantics=("arbitrary",)),
        name="moe_combine",
    )(*args)


def _moe(x2d, g, rw, wg, wu, wd, final_g=None, tm=MOE_ROW_TILE):
    n, d = x2d.shape
    comb, idx, cnt = _router(x2d, g, rw)
    counts = cnt[0].astype(jnp.int32)
    padded = ((counts + tm - 1) // tm) * tm
    ends = jnp.cumsum(padded)
    starts = ends - padded
    pos1 = starts[idx[:, 0]] + idx[:, 2]
    pos2 = starts[idx[:, 1]] + idx[:, 3]
    n_rows = 2 * n + N_EXPERTS * tm
    n_tiles = (ends[-1] // tm).reshape(1)
    tile_expert = jnp.minimum(
        jnp.searchsorted(ends // tm, jnp.arange(n_rows // tm, dtype=jnp.int32), side="right"),
        N_EXPERTS - 1).astype(jnp.int32)
    xs = _dispatch(x2d, g, pos1, pos2, n_rows)
    ys = _grouped_ffn(xs, tile_expert, n_tiles, wg, wu, wd, tm)
    return _combine(x2d, comb, pos1, pos2, ys, final_g=final_g)


def _ffn_kernel(*refs, final):
    if final:
        x_ref, g_ref, wg_ref, wu_ref, wd_ref, fg_ref, o_ref = refs
    else:
        x_ref, g_ref, wg_ref, wu_ref, wd_ref, o_ref = refs
    x = x_ref[...]
    hn = _rms(x, g_ref[...]).astype(BF16)
    act = _silu(jnp.dot(hn, wg_ref[...], preferred_element_type=F32)) * jnp.dot(
        hn, wu_ref[...], preferred_element_type=F32)
    out = x + jnp.dot(act.astype(BF16), wd_ref[...], preferred_element_type=F32)
    if final:
        out = _rms(out, fg_ref[...])
    o_ref[...] = out


def _ffn(x2d, g, wg, wu, wd, final_g=None, tm=512):
    n, d = x2d.shape
    final = final_g is not None
    xmap = lambda i: (i, 0)
    cmap = lambda i: (0, 0)
    resident = lambda w: pl.BlockSpec(w.shape, cmap, pipeline_mode=pl.Buffered(1))
    in_specs = [pl.BlockSpec((tm, d), xmap), pl.BlockSpec((1, d), cmap), resident(wg), resident(wu), resident(wd)]
    args = [x2d, g.reshape(1, d), wg, wu, wd]
    if final:
        in_specs.append(pl.BlockSpec((1, d), cmap))
        args.append(final_g.reshape(1, d))
    return pl.pallas_call(
        functools.partial(_ffn_kernel, final=final),
        grid=(n // tm,),
        in_specs=in_specs,
        out_specs=pl.BlockSpec((tm, d), xmap),
        out_shape=jax.ShapeDtypeStruct((n, d), F32),
        compiler_params=pltpu.CompilerParams(
            dimension_semantics=("arbitrary",), vmem_limit_bytes=VMEM_LIMIT_BYTES),
        name="dense_ffn",
    )(*args)


def kernel(x, norm_mix_g, w_in, gdn_conv_w, gdn_a_log, gdn_dt_bias, gdn_norm_g, rwkv_mu, rwkv_w0, rwkv_w2, rwkv_a0, rwkv_a2, rwkv_g2, rwkv_k_k, rwkv_k_a, rwkv_r_k, rwkv_ln_g, rwkv_ln_b, w_branch_gdn, w_branch_rwkv, w_branch_sb, w_out, norm_ffn_g, ffn_w_gate, ffn_w_up, ffn_w_down, router_w, moe_w_gate, moe_w_up, moe_w_down, final_norm_g):
    b, t, d = x.shape
    n = b * t
    depth = w_in.shape[0]
    x2 = x.reshape(n, d).astype(F32)
    for layer in range(depth):
        w = w_in[layer]
        w_gdn = w[:, 0:2048].astype(BF16)
        w_ba = jnp.pad(w[:, 2048:2056], ((0, 0), (0, 120))).astype(BF16)
        w_rwkv = w[:, 2056:3848].astype(BF16)
        w_sb = w[:, 3848:5384].astype(BF16)
        w_gates = w[:, 5384:8456].astype(BF16)
        g_mix = norm_mix_g[layer].astype(F32)
        p_gdn, p_ba, p_rwkv = _normproj(x2, g_mix, [w_gdn, w_ba, w_rwkv], [F32, F32, F32])
        p_sb, p_gates = _normproj(x2, g_mix, [w_sb, w_gates], [F32, BF16])
        o_a = _gdn(p_gdn.reshape(b, t, -1), p_ba.reshape(b, t, -1), gdn_conv_w[layer], gdn_a_log[layer], gdn_dt_bias[layer],
                   gdn_norm_g[layer])
        o_b = _rwkv(p_rwkv.reshape(b, t, -1), rwkv_mu[layer], rwkv_w0[layer], rwkv_w2[layer],
                    rwkv_a0[layer], rwkv_a2[layer], rwkv_g2[layer], rwkv_k_k[layer], rwkv_k_a[layer],
                    rwkv_r_k[layer].reshape(-1), rwkv_ln_g[layer], rwkv_ln_b[layer])
        o_c = _stick_breaking(p_sb.reshape(b, t, -1))
        x2 = _merge(x2, o_a.reshape(n, -1), o_b.reshape(n, -1), o_c.reshape(n, -1), p_gates,
                    w_branch_gdn[layer].astype(BF16), w_branch_rwkv[layer].astype(BF16),
                    w_branch_sb[layer].astype(BF16), w_out[layer].astype(BF16))
        g_ffn = norm_ffn_g[layer].astype(F32)
        final_g = final_norm_g.astype(F32) if layer == depth - 1 else None
        i = layer // 2
        if layer % 2 == 0:
            x2 = _ffn(x2, g_ffn, ffn_w_gate[i].astype(BF16), ffn_w_up[i].astype(BF16),
                      ffn_w_down[i].astype(BF16), final_g=final_g)
        else:
            x2 = _moe(x2, g_ffn, router_w[i], moe_w_gate[i].astype(BF16), moe_w_up[i].astype(BF16),
                      moe_w_down[i].astype(BF16), final_g=final_g)
    return x2.reshape(b, t, d)
```

```python
import functools

import jax
import jax.numpy as jnp
from jax import lax
from jax.experimental import pallas as pl
from jax.experimental.pallas import tpu as pltpu

F32 = jnp.float32
BF16 = jnp.bfloat16

RMS_EPS = 1e-6
GN_EPS = 64e-5
L2_EPS = 1e-6

D_MODEL = 1024
GDN_HEADS = 4
GDN_HEAD_DIM = 128
GDN_WIDTH = 512
RWKV_WIDTH = 512
RWKV_HEAD_DIM = 64
RWKV_IN = 1792
SB_WIDTH = 512
SB_HEAD_DIM = 64
N_EXPERTS = 8
CHUNK = 64
LOG2E = 1.4426950408889634
SB_LOG2_CUTOFF = -160.0
SB_ROW_TILE = 128
GDN_CHUNKS_PER_STEP = 8
RWKV_CHUNKS_PER_STEP = 8
MOE_ROW_TILE = 512
VMEM_LIMIT_BYTES = 56 * 1024 * 1024


def _dot(a, b):
    return jnp.dot(a.astype(BF16), b.astype(BF16), preferred_element_type=F32)


def _dot_nt(a, b):
    return lax.dot_general(a.astype(BF16), b.astype(BF16), (((1,), (1,)), ((), ())),
                           preferred_element_type=F32)


def _split2(a):
    hi = a.astype(BF16)
    lo = (a - hi.astype(F32)).astype(BF16)
    return hi, lo


def _split3(a):
    hi = a.astype(BF16)
    r = a - hi.astype(F32)
    mid = r.astype(BF16)
    lo = (r - mid.astype(F32)).astype(BF16)
    return hi, mid, lo


def _dot3(a, b):
    ah, al = _split2(a)
    bh, bl = _split2(b)
    return (jnp.dot(ah, bh, preferred_element_type=F32)
            + jnp.dot(ah, bl, preferred_element_type=F32)
            + jnp.dot(al, bh, preferred_element_type=F32))


def _dot_exact_lhs(m, x, parts=3):
    xs = _split3(x) if parts == 3 else _split2(x)
    out = jnp.dot(m, xs[0], preferred_element_type=F32)
    for p in xs[1:]:
        out = out + jnp.dot(m, p, preferred_element_type=F32)
    return out


def _dot_exact_rhs(x, m, parts=2):
    xs = _split3(x) if parts == 3 else _split2(x)
    out = jnp.dot(xs[0], m, preferred_element_type=F32)
    for p in xs[1:]:
        out = out + jnp.dot(p, m, preferred_element_type=F32)
    return out


def _iota2(shape, dim):
    return lax.broadcasted_iota(jnp.int32, shape, dim)


def _eye(n, dtype=F32):
    return (_iota2((n, n), 0) == _iota2((n, n), 1)).astype(dtype)


def _softplus(x):
    return jnp.maximum(x, 0.0) + jnp.log(1.0 + jnp.exp(-jnp.abs(x)))


def _sigmoid(x):
    return 1.0 / (1.0 + jnp.exp(-x))


def _silu(x):
    return x * _sigmoid(x)


def _rms(x, g):
    return x * lax.rsqrt(jnp.mean(x * x, axis=-1, keepdims=True) + RMS_EPS) * g


def _nilpotent_inverse(n, eye, dot):
    t = eye + n
    x = n
    for _ in range(5):
        x = dot(x, x)
        t = t + dot(t, x)
    return t


def _nilpotent_inverse_many(ns, eye, dot):
    ts = [eye + n for n in ns]
    xs = list(ns)
    for _ in range(5):
        xs = [dot(x, x) for x in xs]
        ts = [t + dot(t, x) for t, x in zip(ts, xs)]
    return ts


def _chunk_cumsum(x):
    i = _iota2((128, 128), 0)
    j = _iota2((128, 128), 1)
    m = ((j <= i) & ((i >> 6) == (j >> 6))).astype(BF16)
    return jnp.concatenate(
        [_dot_exact_lhs(m, x[r:r + 128]) for r in range(0, x.shape[0], 128)], axis=0)


def _head_sum(x):
    i = _iota2((128, 128), 0)
    j = _iota2((128, 128), 1)
    m = ((i >> 6) == (j >> 6)).astype(BF16)
    return jnp.concatenate(
        [_dot_exact_rhs(x[:, c:c + 128], m) for c in range(0, x.shape[1], 128)], axis=1)


def _normproj_kernel(x_ref, g_ref, *refs, n_out):
    w_refs = refs[:n_out]
    o_refs = refs[n_out:]
    hn = _rms(x_ref[...], g_ref[...]).astype(BF16)
    for w_ref, o_ref in zip(w_refs, o_refs):
        o_ref[...] = jnp.dot(hn, w_ref[...], preferred_element_type=F32).astype(o_ref.dtype)


def _normproj(x2d, g, weights, out_dtypes, tm=512):
    n, d = x2d.shape
    n_out = len(weights)
    in_specs = [pl.BlockSpec((tm, d), lambda i: (i, 0)), pl.BlockSpec((1, d), lambda i: (0, 0))]
    in_specs += [pl.BlockSpec(w.shape, lambda i: (0, 0), pipeline_mode=pl.Buffered(1)) for w in weights]
    out_specs = [pl.BlockSpec((tm, w.shape[1]), lambda i: (i, 0)) for w in weights]
    out_shape = [jax.ShapeDtypeStruct((n, w.shape[1]), dt) for w, dt in zip(weights, out_dtypes)]
    return pl.pallas_call(
        functools.partial(_normproj_kernel, n_out=n_out),
        grid=(n // tm,),
        in_specs=in_specs,
        out_specs=out_specs,
        out_shape=out_shape,
        compiler_params=pltpu.CompilerParams(
            dimension_semantics=("arbitrary",), vmem_limit_bytes=VMEM_LIMIT_BYTES),
        name="normproj",
    )(x2d, g.reshape(1, d), *weights)


def _gdn_kernel(qkv_ref, z_ref, ba_ref, cw_ref, alog_ref, dtb_ref, ng_ref, o_ref,
                ext_s, state_s, q_s, k_s, v_s, beta_s, gc_s, p_s, qq_s, r_s, zz_s, oc_s, *, tt):
    nc = tt // CHUNK
    w3 = 3 * GDN_WIDTH
    t = pl.program_id(1)

    @pl.when(t == 0)
    def _():
        ext_s[0:8, :] = jnp.zeros((8, w3), F32)
        state_s[...] = jnp.zeros_like(state_s)

    raw = qkv_ref[0]
    ext_s[8:8 + tt, :] = raw
    cw = cw_ref[...]
    y = raw * cw[3:4, :]
    for i in range(3):
        y = y + ext_s[5 + i:5 + i + tt, :] * cw[i:i + 1, :]
    ext_s[0:8, :] = raw[tt - 8:tt, :]
    y = _silu(y)

    for h in range(GDN_HEADS):
        sl = slice(128 * h, 128 * h + 128)
        qh = y[:, 128 * h:128 * h + 128]
        kh = y[:, GDN_WIDTH + 128 * h:GDN_WIDTH + 128 * h + 128]
        vh = y[:, 2 * GDN_WIDTH + 128 * h:2 * GDN_WIDTH + 128 * h + 128]
        qh = qh * lax.rsqrt(jnp.sum(qh * qh, axis=-1, keepdims=True) + L2_EPS) * (GDN_HEAD_DIM ** -0.5)
        kh = kh * lax.rsqrt(jnp.sum(kh * kh, axis=-1, keepdims=True) + L2_EPS)
        q_s[:, sl] = qh
        k_s[:, sl] = kh
        v_s[:, sl] = vh

    ba = ba_ref[0]
    gate = jnp.where(_iota2(ba.shape, 1) < GDN_HEADS, _sigmoid(ba),
                     -jnp.exp(alog_ref[...]) * _softplus(ba + dtb_ref[...]))
    ec = _iota2((128, 2 * GDN_WIDTH), 0)
    el = _iota2((128, 2 * GDN_WIDTH), 1)
    gate = _dot_exact_rhs(gate, (ec == (el >> 7)).astype(BF16), parts=3)
    beta_s[...] = gate[:, 0:GDN_WIDTH]
    gc_s[...] = _chunk_cumsum(gate[:, GDN_WIDTH:2 * GDN_WIDTH])

    eye64 = _eye(CHUNK)
    eye128 = _eye(128)
    ii = _iota2((CHUNK, CHUNK), 0)
    jj = _iota2((CHUNK, CHUNK), 1)

    def chunk_body(ci, carry):
        probs = [(ci * GDN_CHUNKS_PER_STEP + u, h) for u in range(GDN_CHUNKS_PER_STEP)
                 for h in range(GDN_HEADS)]
        rows = [pl.ds(pl.multiple_of(c * CHUNK, CHUNK), CHUNK) for c, _ in probs]
        lanes = [slice(128 * h, 128 * h + 128) for _, h in probs]
        idx = range(len(probs))
        q = [q_s[rows[i], lanes[i]] for i in idx]
        k = [k_s[rows[i], lanes[i]] for i in idx]
        gcc = [gc_s[rows[i], lanes[i]] for i in idx]
        gl = [g[CHUNK - 1:CHUNK, :] for g in gcc]
        gcr = [g.T[0:CHUNK, :] for g in gcc]
        dec_incl = [jnp.exp(jnp.where(ii >= jj, gcc[i][:, 0:CHUNK] - gcr[i], -jnp.inf)) for i in idx]
        kb = [k[i] * beta_s[rows[i], lanes[i]] for i in idx]
        a_mat = [_dot_nt(kb[i], k[i]) for i in idx]
        attn = [_dot_nt(q[i], k[i]) * dec_incl[i] for i in idx]
        kdt = [(k[i] * jnp.exp(gl[i] - gcc[i])).T for i in idx]
        tinv = _nilpotent_inverse_many(
            [-a_mat[i] * jnp.where(ii > jj, dec_incl[i], 0.0) for i in idx], eye64, _dot)
        u = [_dot(tinv[i], v_s[rows[i], lanes[i]] * beta_s[rows[i], lanes[i]]) for i in idx]
        w = [_dot(tinv[i], kb[i] * jnp.exp(gcc[i])) for i in idx]
        for i, (c, h) in enumerate(probs):
            p_s[c, h] = jnp.exp(gl[i]) * eye128 - _dot(kdt[i], w[i])
        for i, (c, h) in enumerate(probs):
            qq_s[c, h] = _dot(kdt[i], u[i])
        for i, (c, h) in enumerate(probs):
            r_s[c, h] = q[i] * jnp.exp(gcc[i]) - _dot(attn[i], w[i])
        for i, (c, h) in enumerate(probs):
            zz_s[c, h] = _dot(attn[i], u[i])
        return carry

    lax.fori_loop(0, nc // GDN_CHUNKS_PER_STEP, chunk_body, 0)

    def scan_body(c, carry):
        rows = pl.ds(pl.multiple_of(c * CHUNK, CHUNK), CHUNK)
        s = [state_s[h] for h in range(GDN_HEADS)]
        s_new = [_dot3(p_s[c, h], s[h]) for h in range(GDN_HEADS)]
        o = [_dot(r_s[c, h], s[h]) for h in range(GDN_HEADS)]
        for h in range(GDN_HEADS):
            state_s[h] = s_new[h] + qq_s[c, h]
            oc_s[rows, 128 * h:128 * h + 128] = o[h] + zz_s[c, h]
        return carry

    lax.fori_loop(0, nc, scan_body, 0)

    o = oc_s[...]
    z = z_ref[0]
    ng = ng_ref[...]
    for h in range(GDN_HEADS):
        sl = slice(128 * h, 128 * h + 128)
        oh = o[:, sl]
        oh = oh * lax.rsqrt(jnp.mean(oh * oh, axis=-1, keepdims=True) + RMS_EPS) * ng[:, sl]
        o_ref[0, :, sl] = (oh * _silu(z[:, sl])).astype(o_ref.dtype)


def _gdn(proj_a, proj_ba, conv_w, a_log, dt_bias, norm_g, tt=512):
    b, t, _ = proj_a.shape
    nc = tt // CHUNK
    w3 = 3 * GDN_WIDTH
    rep = lambda p: jnp.pad(p.astype(F32), (GDN_HEADS, 128 - 2 * GDN_HEADS)).reshape(1, 128)
    small = lambda shape: pl.BlockSpec(shape, lambda i, j: (0, 0))
    return pl.pallas_call(
        functools.partial(_gdn_kernel, tt=tt),
        grid=(b, t // tt),
        in_specs=[
            pl.BlockSpec((1, tt, w3), lambda i, j: (i, j, 0)),
            pl.BlockSpec((1, tt, GDN_WIDTH), lambda i, j: (i, j, 3)),
            pl.BlockSpec((1, tt, 128), lambda i, j: (i, j, 0)),
            small((4, w3)), small((1, 128)), small((1, 128)), small((1, GDN_WIDTH)),
        ],
        out_specs=pl.BlockSpec((1, tt, GDN_WIDTH), lambda i, j: (i, j, 0)),
        out_shape=jax.ShapeDtypeStruct((b, t, GDN_WIDTH), BF16),
        scratch_shapes=[
            pltpu.VMEM((tt + 8, w3), F32),
            pltpu.VMEM((GDN_HEADS, 128, 128), F32),
            pltpu.VMEM((tt, GDN_WIDTH), F32), pltpu.VMEM((tt, GDN_WIDTH), F32),
            pltpu.VMEM((tt, GDN_WIDTH), F32), pltpu.VMEM((tt, GDN_WIDTH), F32),
            pltpu.VMEM((tt, GDN_WIDTH), F32),
            pltpu.VMEM((nc, GDN_HEADS, 128, 128), F32), pltpu.VMEM((nc, GDN_HEADS, 128, 128), F32),
            pltpu.VMEM((nc, GDN_HEADS, CHUNK, 128), F32), pltpu.VMEM((nc, GDN_HEADS, CHUNK, 128), F32),
            pltpu.VMEM((tt, GDN_WIDTH), F32),
        ],
        compiler_params=pltpu.CompilerParams(
            dimension_semantics=("arbitrary", "arbitrary"), vmem_limit_bytes=VMEM_LIMIT_BYTES),
        name="gdn",
    )(proj_a, proj_a, proj_ba, conv_w.astype(F32), rep(a_log), rep(dt_bias),
      jnp.tile(norm_g.astype(F32), GDN_HEADS).reshape(1, GDN_WIDTH))


def _rwkv_kernel(h_ref, mu_ref, w0_ref, a0_ref, kk_ref, ka_ref, rk_ref, lng_ref, lnb_ref,
                 wa_ref, g2_ref, o_ref,
                 ext_s, state_s, r_s, kn_s, k2_s, v_s, a_s, lw_s, lc_s, p_s, qq_s, rh_s, yc_s, y_s,
                 *, tt):
    nc = tt // CHUNK
    npair = RWKV_WIDTH // 128
    t = pl.program_id(1)

    @pl.when(t == 0)
    def _():
        ext_s[0:8, :] = jnp.zeros((8, RWKV_IN), F32)
        state_s[...] = jnp.zeros_like(state_s)

    raw = h_ref[0]
    ext_s[8:8 + tt, :] = raw
    prev = ext_s[7:7 + tt, :]
    ext_s[0:8, :] = raw[tt - 8:tt, :]
    hl = raw + (prev - raw) * mu_ref[...]
    r = hl[:, 0:512]
    k = hl[:, 512:1024]
    v = hl[:, 1024:1536]
    xwa = hl[:, 1536:1664]
    xg = hl[:, 1664:1792]
    lane128 = _iota2((1, 128), 1)
    xwa = jnp.where(lane128 < 64, jnp.tanh(xwa), xwa)
    lora = _dot3(xwa, wa_ref[...])
    w_log = -_softplus(-(w0_ref[...] + lora[:, 0:512])) - 0.5
    lw = -jnp.exp(w_log)
    a = _sigmoid(a0_ref[...] + lora[:, 512:1024])
    gate = _dot3(_sigmoid(xg), g2_ref[...])
    kk = k * kk_ref[...]
    k2 = k * (1.0 + (a - 1.0) * ka_ref[...])
    kn = kk * lax.rsqrt(_head_sum(kk * kk) + L2_EPS)
    r_s[...] = r
    kn_s[...] = kn
    k2_s[...] = k2
    v_s[...] = v
    a_s[...] = a
    lw_s[...] = lw
    lc_s[...] = _chunk_cumsum(lw)

    eye128 = _eye(128)
    row = _iota2((CHUNK, 128), 0)
    lane = _iota2((CHUNK, 128), 1)
    col = lane & (CHUNK - 1)
    strict = row > col
    incl = row >= col
    eye2 = (row == col).astype(F32)
    head0 = lane < CHUNK
    head0_wide = (_iota2((CHUNK, 256), 1) & 127) < CHUNK
    bi = _iota2((128, 128), 0)
    bj = _iota2((128, 128), 1)
    blockdiag = (bi < 64) == (bj < 64)

    def per_head_rows(x, mask=head0):
        return jnp.concatenate([jnp.where(mask, x, 0.0), jnp.where(mask, 0.0, x)], axis=0)

    def chunk_body(ci, carry):
        units = [(ci * RWKV_CHUNKS_PER_STEP + u, p) for u in range(RWKV_CHUNKS_PER_STEP) for p in range(npair)]
        nu = range(len(units))
        rows = [pl.ds(pl.multiple_of(c * CHUNK, CHUNK), CHUNK) for c, _ in units]
        sls = [slice(128 * p, 128 * p + 128) for _, p in units]
        lcc = [lc_s[rows[u], sls[u]] for u in nu]
        vc = [v_s[rows[u], sls[u]] for u in nu]
        lcl = [x[CHUNK - 1:CHUNK, :] for x in lcc]
        ginv = [jnp.exp(-x) for x in lcc]
        gend = [jnp.exp(lcl[u] - lcc[u]) for u in nu]
        kna = [kn_s[rows[u], sls[u]] * a_s[rows[u], sls[u]] for u in nu]
        at = [-kn_s[rows[u], sls[u]] * jnp.exp(lcc[u] - lw_s[rows[u], sls[u]]) for u in nu]
        rt = [r_s[rows[u], sls[u]] * jnp.exp(lcc[u]) for u in nu]
        ar = [jnp.concatenate([at[u], rt[u]], axis=0) for u in nu]
        xb = [_dot_nt(ar[u], per_head_rows(kna[u] * ginv[u])) for u in nu]
        xk = [_dot_nt(ar[u], per_head_rows(k2_s[rows[u], sls[u]] * ginv[u])) for u in nu]
        a_ab = [jnp.where(strict, x[0:CHUNK], 0.0) for x in xb]
        a_rb = [jnp.where(incl, x[CHUNK:2 * CHUNK], 0.0) for x in xb]
        a_ak = [jnp.where(strict, x[0:CHUNK], 0.0) for x in xk]
        a_rk = [jnp.where(incl, x[CHUNK:2 * CHUNK], 0.0) for x in xk]
        vrows = [per_head_rows(vc[u]) for u in nu]
        akv = [_dot(a_ak[u], vrows[u]) for u in nu]
        arkv = [_dot(a_rk[u], vrows[u]) for u in nu]
        bbt = [(kna[u] * gend[u]).T for u in nu]
        kbt = [(k2_s[rows[u], sls[u]] * gend[u]).T for u in nu]
        ts = [eye2 + a for a in a_ab]
        xs = [_dot(x, per_head_rows(x)) for x in a_ab]
        for step in range(5):
            if step < 4:
                xt = [_dot(jnp.concatenate([x, t], axis=0), per_head_rows(x)) for x, t in zip(xs, ts)]
                xs = [y[0:CHUNK] for y in xt]
                ts = [t + y[CHUNK:2 * CHUNK] for t, y in zip(ts, xt)]
            else:
                ts = [t + _dot(t, per_head_rows(x)) for t, x in zip(ts, xs)]
        au = [_dot(ts[u], per_head_rows(jnp.concatenate([at[u], akv[u]], axis=1), head0_wide)) for u in nu]
        ry = [_dot(a_rb[u], per_head_rows(au[u], head0_wide)) for u in nu]
        kv = [_dot(kbt[u], vc[u]) for u in nu]
        pq = [_dot(bbt[u], au[u]) for u in nu]
        for u, (c, p) in enumerate(units):
            rh_s[c, p] = rt[u] + ry[u][:, 0:128]
            yc_s[c, p] = ry[u][:, 128:256] + arkv[u]
            p_s[c, p] = jnp.where(blockdiag, pq[u][:, 0:128], 0.0) + eye128 * jnp.exp(lcl[u])
            qq_s[c, p] = jnp.where(blockdiag, pq[u][:, 128:256] + kv[u], 0.0)
        return carry

    lax.fori_loop(0, nc // RWKV_CHUNKS_PER_STEP, chunk_body, 0)

    def scan_body(c, carry):
        rows = pl.ds(pl.multiple_of(c * CHUNK, CHUNK), CHUNK)
        s = [state_s[p] for p in range(npair)]
        s_new = [_dot3(p_s[c, p], s[p]) for p in range(npair)]
        y = [_dot(rh_s[c, p], s[p]) for p in range(npair)]
        for p in range(npair):
            state_s[p] = s_new[p] + qq_s[c, p]
            y_s[rows, 128 * p:128 * p + 128] = y[p] + yc_s[c, p]
        return carry

    lax.fori_loop(0, nc, scan_body, 0)

    y = y_s[...]
    mean = _head_sum(y) * (1.0 / RWKV_HEAD_DIM)
    yc = y - mean
    var = _head_sum(yc * yc) * (1.0 / RWKV_HEAD_DIM)
    yn = yc * lax.rsqrt(var + GN_EPS) * lng_ref[...] + lnb_ref[...]
    bonus = _head_sum(r * k2 * rk_ref[...]) * v
    o_ref[0] = ((yn + bonus) * gate).astype(o_ref.dtype)


def _dot3_nt(a, b):
    ah, al = _split2(a)
    bh, bl = _split2(b)
    dn = (((1,), (1,)), ((), ()))
    return (lax.dot_general(ah, bh, dn, preferred_element_type=F32)
            + lax.dot_general(ah, bl, dn, preferred_element_type=F32)
            + lax.dot_general(al, bh, dn, preferred_element_type=F32))


def _rwkv(h, mu, w0, w2, a0, a2, g2, k_k, k_a, r_k, ln_g, ln_b, tt=512):
    b, t, _ = h.shape
    nc = tt // CHUNK
    npair = RWKV_WIDTH // 128
    row = lambda p: p.astype(F32).reshape(1, -1)
    wa = jnp.zeros((128, 2 * RWKV_WIDTH), F32)
    wa = wa.at[0:64, 0:RWKV_WIDTH].set(w2.astype(F32)).at[64:128, RWKV_WIDTH:].set(a2.astype(F32))
    small = lambda shape: pl.BlockSpec(shape, lambda i, j: (0, 0))
    vec = small((1, RWKV_WIDTH))
    return pl.pallas_call(
        functools.partial(_rwkv_kernel, tt=tt),
        grid=(b, t // tt),
        in_specs=[pl.BlockSpec((1, tt, RWKV_IN), lambda i, j: (i, j, 0)), small((1, RWKV_IN)),
                  vec, vec, vec, vec, vec, vec, vec,
                  small((128, 2 * RWKV_WIDTH)), small((128, RWKV_WIDTH))],
        out_specs=pl.BlockSpec((1, tt, RWKV_WIDTH), lambda i, j: (i, j, 0)),
        out_shape=jax.ShapeDtypeStruct((b, t, RWKV_WIDTH), BF16),
        scratch_shapes=[
            pltpu.VMEM((tt + 8, RWKV_IN), F32),
            pltpu.VMEM((npair, 128, 128), F32),
        ] + [pltpu.VMEM((tt, RWKV_WIDTH), F32)] * 7 + [
            pltpu.VMEM((nc, npair, 128, 128), F32), pltpu.VMEM((nc, npair, 128, 128), F32),
            pltpu.VMEM((nc, npair, CHUNK, 128), F32), pltpu.VMEM((nc, npair, CHUNK, 128), F32),
            pltpu.VMEM((tt, RWKV_WIDTH), F32),
        ],
        compiler_params=pltpu.CompilerParams(
            dimension_semantics=("arbitrary", "arbitrary"), vmem_limit_bytes=VMEM_LIMIT_BYTES),
        name="rwkv7",
    )(h, row(mu), row(w0), row(a0), row(k_k), row(k_a), row(r_k), row(ln_g), row(ln_b),
      wa, g2.astype(F32))


def _sb_kernel(q_ref, k_ref, v_ref, o_ref, acc_s, aux_s, *, bq, bk):
    qi = pl.program_id(2)
    nsub = bq // bk
    q = q_ref[0] * (SB_HEAD_DIM ** -0.5 * LOG2E)
    lane = _iota2((1, 128), 1)
    qm = [jnp.where(lane < 64, q, 0.0).astype(BF16), jnp.where(lane >= 64, q, 0.0).astype(BF16)]
    ti = _iota2((2 * bk, bk), 0)
    tj = _iota2((2 * bk, bk), 1)
    cum_mat = ((ti & (bk - 1)) > tj).astype(BF16)
    acc_s[...] = jnp.zeros_like(acc_s)
    aux_s[...] = jnp.zeros_like(aux_s)

    def load_kv(kb):
        start = pl.multiple_of(kb * bk, bk)
        return (start, k_ref[0, pl.ds(start, bk), :].astype(BF16), v_ref[0, pl.ds(start, bk), :].astype(BF16))

    def stage_scores(item):
        (start, k, v), ra, nr, j, masked = item
        z = lax.dot_general(qm[j][ra:ra + nr], k, (((1,), (1,)), ((), ())), preferred_element_type=F32)
        return z

    def stage_cumsum(item, z):
        (start, k, v), ra, nr, j, masked = item
        rows = slice(ra, ra + nr)
        neg_abs = lax.bitcast_convert_type(
            lax.bitcast_convert_type(z, jnp.uint32) | jnp.uint32(0x80000000), F32)
        lsig = jnp.minimum(z, 0.0) - jnp.log(1.0 + jnp.exp2(neg_abs)) * LOG2E
        l1 = lsig - z
        msk = None
        if masked:
            msk = (start + _iota2((nr, bk), 1)) < (qi * bq + ra + _iota2((nr, bk), 0))
            l1 = jnp.where(msk, l1, 0.0)
        l1_hi = l1.astype(BF16)
        l1_lo = (l1 - l1_hi.astype(F32)).astype(BF16)
        cr = (jnp.dot(jnp.concatenate([l1_hi, l1_lo], axis=1), cum_mat, preferred_element_type=F32)
              + aux_s[j, rows, 0:1])
        aux_s[j, rows, :] = cr + l1
        return lsig, cr, msk

    def stage_values(item, state):
        (start, k, v), ra, nr, j, masked = item
        lsig, cr, msk = state
        att = jnp.exp2(lsig + cr)
        if masked:
            att = jnp.where(msk, att, 0.0)
        acc_s[j, ra:ra + nr, :] += jnp.dot(att.astype(BF16), v, preferred_element_type=F32)

    def run_items(items):
        n = len(items)
        zs = {}
        states = {}
        for s in range(n + 2):
            if s < n:
                zs[s] = stage_scores(items[s])
            if 0 <= s - 1 < n:
                states[s - 1] = stage_cumsum(items[s - 1], zs.pop(s - 1))
            if 0 <= s - 2 < n:
                stage_values(items[s - 2], states.pop(s - 2))

    def max_carry(ra, nr):
        r = jnp.maximum(aux_s[0, ra:ra + nr, :], aux_s[1, ra:ra + nr, :])
        r = jnp.max(jnp.where(lane == 0, r, -jnp.inf), axis=0, keepdims=True)
        return jnp.max(r, axis=1, keepdims=True)[0, 0]

    items = []
    for d in range(nsub - 1, -1, -1):
        kv = load_kv(qi * nsub + d)
        ra = d * bk
        while ra < bq:
            nr = SB_ROW_TILE if (bq - ra) % (2 * SB_ROW_TILE) else 2 * SB_ROW_TILE
            items += [(kv, ra, nr, 0, True), (kv, ra, nr, 1, True)]
            ra += nr
    run_items(items)

    def cond(c):
        kb, rmax = c
        return (kb >= 0) & (rmax > SB_LOG2_CUTOFF)

    for ra in range(0, bq, 2 * SB_ROW_TILE):
        nr = 2 * SB_ROW_TILE

        def body(c, ra=ra, nr=nr):
            kb, _ = c
            items = []
            for kv in (load_kv(kb), load_kv(kb - 1)):
                items += [(kv, ra, nr, 0, False), (kv, ra, nr, 1, False)]
            run_items(items)
            return kb - 2, max_carry(ra, nr)

        lax.while_loop(cond, body, (qi * nsub - 1, max_carry(ra, nr)))
    o_ref[0] = jnp.where(lane < 64, acc_s[0], acc_s[1]).astype(o_ref.dtype)


def _stick_breaking(h_sb, bq=512, bk=128):
    b, t, _ = h_sb.shape
    npair = SB_WIDTH // 128
    return pl.pallas_call(
        functools.partial(_sb_kernel, bq=bq, bk=bk),
        grid=(b, npair, t // bq),
        in_specs=[
            pl.BlockSpec((1, bq, 128), lambda i, p, j: (i, j, p)),
            pl.BlockSpec((1, t, 128), lambda i, p, j: (i, 0, npair + p)),
            pl.BlockSpec((1, t, 128), lambda i, p, j: (i, 0, 2 * npair + p)),
        ],
        out_specs=pl.BlockSpec((1, bq, 128), lambda i, p, j: (i, j, p)),
        out_shape=jax.ShapeDtypeStruct((b, t, SB_WIDTH), BF16),
        scratch_shapes=[pltpu.VMEM((2, bq, 128), F32), pltpu.VMEM((2, bq, bk), F32)],
        compiler_params=pltpu.CompilerParams(
            dimension_semantics=("arbitrary", "arbitrary", "arbitrary"),
            vmem_limit_bytes=VMEM_LIMIT_BYTES),
        name="stickbreak",
    )(h_sb, h_sb, h_sb)


def _merge_kernel(x_ref, oa_ref, ob_ref, oc_ref, gt_ref, wa_ref, wb_ref, wc_ref, wo_ref, o_ref):
    d = D_MODEL
    g = _sigmoid(gt_ref[...].astype(F32))
    m = (g[:, 0:d] * _dot(oa_ref[...], wa_ref[...])
         + g[:, d:2 * d] * _dot(ob_ref[...], wb_ref[...])
         + g[:, 2 * d:3 * d] * _dot(oc_ref[...], wc_ref[...]))
    o_ref[...] = x_ref[...] + _dot(m, wo_ref[...])


def _merge(x2d, oa, ob, oc, gates, wa, wb, wc, wo, tm=512):
    n, d = x2d.shape
    rowspec = lambda w: pl.BlockSpec((tm, w), lambda i: (i, 0))
    full = lambda w: pl.BlockSpec(w.shape, lambda i: (0, 0))
    return pl.pallas_call(
        _merge_kernel,
        grid=(n // tm,),
        in_specs=[rowspec(d), rowspec(512), rowspec(512), rowspec(512), rowspec(3 * d),
                  full(wa), full(wb), full(wc), full(wo)],
        out_specs=rowspec(d),
        out_shape=jax.ShapeDtypeStruct((n, d), F32),
        compiler_params=pltpu.CompilerParams(
            dimension_semantics=("arbitrary",), vmem_limit_bytes=VMEM_LIMIT_BYTES),
        name="merge",
    )(x2d, oa, ob, oc, gates, wa, wb, wc, wo)


def _router_kernel(x_ref, g_ref, rw_ref, comb_ref, idx_ref, cnt_ref, run_s):
    @pl.when(pl.program_id(0) == 0)
    def _():
        run_s[...] = jnp.zeros_like(run_s)

    hn = _rms(x_ref[...], g_ref[...])
    logits = _dot3(hn, rw_ref[...])
    tm = logits.shape[0]
    lane = _iota2(logits.shape, 1)
    m1 = jnp.max(logits, axis=-1, keepdims=True)
    i1 = jnp.min(jnp.where(logits == m1, lane, N_EXPERTS), axis=-1, keepdims=True)
    sel1 = lane == i1
    rest = jnp.where(sel1, -jnp.inf, logits)
    m2 = jnp.max(rest, axis=-1, keepdims=True)
    i2 = jnp.min(jnp.where(rest == m2, lane, N_EXPERTS), axis=-1, keepdims=True)
    sel2 = lane == i2
    e2 = jnp.exp(m2 - m1)
    w1 = 1.0 / (1.0 + e2)
    comb_ref[...] = jnp.where(sel1, w1, 0.0) + jnp.where(sel2, e2 * w1, 0.0)

    member = (sel1 | sel2).astype(BF16)
    before = (_iota2((tm, tm), 0) > _iota2((tm, tm), 1)).astype(BF16)
    rank = jnp.dot(before, member, preferred_element_type=F32) + run_s[...]
    r1 = jnp.sum(jnp.where(sel1, rank, 0.0), axis=-1, keepdims=True).astype(jnp.int32)
    r2 = jnp.sum(jnp.where(sel2, rank, 0.0), axis=-1, keepdims=True).astype(jnp.int32)
    run_s[...] += jnp.sum(member.astype(F32), axis=0, keepdims=True)
    cnt_ref[...] = run_s[...]
    lane128 = _iota2((tm, 128), 1)
    idx_ref[...] = jnp.where(lane128 == 0, i1, jnp.where(lane128 == 1, i2,
                             jnp.where(lane128 == 2, r1, jnp.where(lane128 == 3, r2, 0))))


def _router(x2d, g, rw, tm=512):
    n, d = x2d.shape
    return pl.pallas_call(
        _router_kernel,
        grid=(n // tm,),
        in_specs=[pl.BlockSpec((tm, d), lambda i: (i, 0)), pl.BlockSpec((1, d), lambda i: (0, 0)),
                  pl.BlockSpec(rw.shape, lambda i: (0, 0))],
        out_specs=[pl.BlockSpec((tm, N_EXPERTS), lambda i: (i, 0)),
                   pl.BlockSpec((tm, 128), lambda i: (i, 0)),
                   pl.BlockSpec((1, N_EXPERTS), lambda i: (0, 0))],
        out_shape=[jax.ShapeDtypeStruct((n, N_EXPERTS), F32),
                   jax.ShapeDtypeStruct((n, 128), jnp.int32),
                   jax.ShapeDtypeStruct((1, N_EXPERTS), F32)],
        scratch_shapes=[pltpu.VMEM((1, N_EXPERTS), F32)],
        compiler_params=pltpu.CompilerParams(dimension_semantics=("arbitrary",)),
        name="router",
    )(x2d, g.reshape(1, d), rw.astype(F32))


def _row_copy(src_ref, src_row, dst_ref, dst_row, sem):
    return pltpu.make_async_copy(src_ref.at[pl.ds(src_row, 1), :], dst_ref.at[pl.ds(dst_row, 1), :], sem)


def _drain_rows(n_copies, src_ref, dst_ref, sem):
    def body(t, carry):
        _row_copy(src_ref, 0, dst_ref, 0, sem).wait()
        return carry

    lax.fori_loop(0, n_copies, body, 0, unroll=8)


def _dispatch_kernel(pos1_ref, pos2_ref, x_ref, g_ref, xs_in_ref, xs_ref, hn_s, sems):
    del xs_in_ref
    tm = hn_s.shape[1]
    i = pl.program_id(0)
    slot = i % 2
    hn_s[slot] = _rms(x_ref[...], g_ref[...])
    src = hn_s.at[slot]
    sem = sems.at[slot]

    def issue(t, carry):
        _row_copy(src, t, xs_ref, pos1_ref[t], sem).start(priority=0)
        _row_copy(src, t, xs_ref, pos2_ref[t], sem).start(priority=1)
        return carry

    lax.fori_loop(0, tm, issue, 0, unroll=8)

    @pl.when(i > 0)
    def _():
        _drain_rows(2 * tm, hn_s.at[1 - slot], xs_ref, sems.at[1 - slot])

    @pl.when(i == pl.num_programs(0) - 1)
    def _():
        _drain_rows(2 * tm, src, xs_ref, sem)


def _dispatch(x2d, g, pos1, pos2, n_rows, tm=512):
    n, d = x2d.shape
    smem = lambda: pl.BlockSpec((tm,), lambda i: (i,), memory_space=pltpu.SMEM)
    return pl.pallas_call(
        _dispatch_kernel,
        grid=(n // tm,),
        in_specs=[smem(), smem(), pl.BlockSpec((tm, d), lambda i: (i, 0)),
                  pl.BlockSpec((1, d), lambda i: (0, 0)), pl.BlockSpec(memory_space=pl.ANY)],
        out_specs=pl.BlockSpec(memory_space=pl.ANY),
        out_shape=jax.ShapeDtypeStruct((n_rows, d), F32),
        scratch_shapes=[pltpu.VMEM((2, tm, d), F32), pltpu.SemaphoreType.DMA((2,))],
        input_output_aliases={4: 0},
        compiler_params=pltpu.CompilerParams(dimension_semantics=("arbitrary",)),
        name="moe_dispatch",
    )(pos1, pos2, x2d, g.reshape(1, d), jnp.zeros((n_rows, d), F32))


def _grouped_ffn_kernel(te_ref, nt_ref, xs_ref, wg_ref, wu_ref, wd_ref, o_ref, xb_s, acc_s):
    i = pl.program_id(0)
    f = pl.program_id(1)

    @pl.when(f == 0)
    def _():
        xb_s[...] = xs_ref[...].astype(BF16)
        acc_s[...] = jnp.zeros_like(acc_s)

    @pl.when(i < nt_ref[0])
    def _():
        xb = xb_s[...]
        act = _silu(jnp.dot(xb, wg_ref[...], preferred_element_type=F32)) * jnp.dot(
            xb, wu_ref[...], preferred_element_type=F32)
        acc_s[...] += jnp.dot(act.astype(BF16), wd_ref[...], preferred_element_type=F32)

    @pl.when(f == pl.num_programs(1) - 1)
    def _():
        o_ref[...] = acc_s[...]


def _grouped_ffn(xs, tile_expert, n_tiles, wg, wu, wd, tm, tf=1792):
    n_rows, d = xs.shape
    dff = wg.shape[2]
    nf = dff // tf
    fsel = lambda i, f, te, nt: jnp.where(i < nt[0], f, nf - 1)
    return pl.pallas_call(
        _grouped_ffn_kernel,
        grid_spec=pltpu.PrefetchScalarGridSpec(
            num_scalar_prefetch=2,
            grid=(n_rows // tm, nf),
            in_specs=[pl.BlockSpec((tm, d), lambda i, f, te, nt: (i, 0)),
                      pl.BlockSpec((None, d, tf), lambda i, f, te, nt: (te[i], 0, fsel(i, f, te, nt))),
                      pl.BlockSpec((None, d, tf), lambda i, f, te, nt: (te[i], 0, fsel(i, f, te, nt))),
                      pl.BlockSpec((None, tf, d), lambda i, f, te, nt: (te[i], fsel(i, f, te, nt), 0))],
            out_specs=pl.BlockSpec((tm, d), lambda i, f, te, nt: (i, 0)),
            scratch_shapes=[pltpu.VMEM((tm, d), BF16), pltpu.VMEM((tm, d), F32)]),
        out_shape=jax.ShapeDtypeStruct((n_rows, d), F32),
        compiler_params=pltpu.CompilerParams(
            dimension_semantics=("arbitrary", "arbitrary"), vmem_limit_bytes=VMEM_LIMIT_BYTES),
        name="moe_grouped_ffn",
    )(tile_expert, n_tiles, xs, wg, wu, wd)


def _combine_kernel(*refs, final):
    if final:
        pos1_ref, pos2_ref, npos1_ref, npos2_ref, x_ref, comb_ref, fg_ref, ys_ref, o_ref, y1_s, y2_s, sems = refs
    else:
        pos1_ref, pos2_ref, npos1_ref, npos2_ref, x_ref, comb_ref, ys_ref, o_ref, y1_s, y2_s, sems = refs
    tm = y1_s.shape[1]
    i = pl.program_id(0)
    slot = i % 2

    def gather(p1_ref, p2_ref, sl):
        def issue(t, carry):
            _row_copy(ys_ref, p1_ref[t], y1_s.at[sl], t, sems.at[sl]).start(priority=0)
            _row_copy(ys_ref, p2_ref[t], y2_s.at[sl], t, sems.at[sl]).start(priority=1)
            return carry

        lax.fori_loop(0, tm, issue, 0, unroll=8)

    @pl.when(i == 0)
    def _():
        gather(pos1_ref, pos2_ref, 0)

    @pl.when(i < pl.num_programs(0) - 1)
    def _():
        gather(npos1_ref, npos2_ref, 1 - slot)

    _drain_rows(2 * tm, ys_ref, y1_s.at[slot], sems.at[slot])
    comb = comb_ref[...]
    w1 = jnp.max(comb, axis=-1, keepdims=True)
    w2 = jnp.sum(comb, axis=-1, keepdims=True) - w1
    out = x_ref[...] + w1 * y1_s[slot] + w2 * y2_s[slot]
    if final:
        out = _rms(out, fg_ref[...])
    o_ref[...] = out


def _combine(x2d, comb, pos1, pos2, ys, final_g=None, tm=512):
    n, d = x2d.shape
    final = final_g is not None
    nsteps = n // tm
    smem = lambda: pl.BlockSpec((tm,), lambda i: (i,), memory_space=pltpu.SMEM)
    smem_next = lambda: pl.BlockSpec((tm,), lambda i: (jnp.minimum(i + 1, nsteps - 1),),
                                     memory_space=pltpu.SMEM)
    in_specs = [smem(), smem(), smem_next(), smem_next(), pl.BlockSpec((tm, d), lambda i: (i, 0)),
                pl.BlockSpec((tm, N_EXPERTS), lambda i: (i, 0))]
    args = [pos1, pos2, pos1, pos2, x2d, comb]
    if final:
        in_specs.append(pl.BlockSpec((1, d), lambda i: (0, 0)))
        args.append(final_g.reshape(1, d))
    in_specs.append(pl.BlockSpec(memory_space=pl.ANY))
    args.append(ys)
    return pl.pallas_call(
        functools.partial(_combine_kernel, final=final),
        grid=(nsteps,),
        in_specs=in_specs,
        out_specs=pl.BlockSpec((tm, d), lambda i: (i, 0)),
        out_shape=jax.ShapeDtypeStruct((n, d), F32),
        scratch_shapes=[pltpu.VMEM((2, tm, d), F32), pltpu.VMEM((2, tm, d), F32),
                        pltpu.SemaphoreType.DMA((2,))],
        compiler_params=pltpu.CompilerParams(dimension_semantics=("arbitrary",)),
        name="moe_combine",
    )(*args)


def _moe(x2d, g, rw, wg, wu, wd, final_g=None, tm=MOE_ROW_TILE):
    n, d = x2d.shape
    comb, idx, cnt = _router(x2d, g, rw)
    counts = cnt[0].astype(jnp.int32)
    padded = ((counts + tm - 1) // tm) * tm
    ends = jnp.cumsum(padded)
    starts = ends - padded
    pos1 = starts[idx[:, 0]] + idx[:, 2]
    pos2 = starts[idx[:, 1]] + idx[:, 3]
    n_rows = 2 * n + N_EXPERTS * tm
    n_tiles = (ends[-1] // tm).reshape(1)
    tile_expert = jnp.minimum(
        jnp.searchsorted(ends // tm, jnp.arange(n_rows // tm, dtype=jnp.int32), side="right"),
        N_EXPERTS - 1).astype(jnp.int32)
    xs = _dispatch(x2d, g, pos1, pos2, n_rows)
    ys = _grouped_ffn(xs, tile_expert, n_tiles, wg, wu, wd, tm)
    return _combine(x2d, comb, pos1, pos2, ys, final_g=final_g)


def _ffn_kernel(*refs, final):
    if final:
        x_ref, g_ref, wg_ref, wu_ref, wd_ref, fg_ref, o_ref = refs
    else:
        x_ref, g_ref, wg_ref, wu_ref, wd_ref, o_ref = refs
    x = x_ref[...]
    hn = _rms(x, g_ref[...]).astype(BF16)
    act = _silu(jnp.dot(hn, wg_ref[...], preferred_element_type=F32)) * jnp.dot(
        hn, wu_ref[...], preferred_element_type=F32)
    out = x + jnp.dot(act.astype(BF16), wd_ref[...], preferred_element_type=F32)
    if final:
        out = _rms(out, fg_ref[...])
    o_ref[...] = out


def _ffn(x2d, g, wg, wu, wd, final_g=None, tm=512):
    n, d = x2d.shape
    final = final_g is not None
    xmap = lambda i: (i, 0)
    cmap = lambda i: (0, 0)
    resident = lambda w: pl.BlockSpec(w.shape, cmap, pipeline_mode=pl.Buffered(1))
    in_specs = [pl.BlockSpec((tm, d), xmap), pl.BlockSpec((1, d), cmap), resident(wg), resident(wu), resident(wd)]
    args = [x2d, g.reshape(1, d), wg, wu, wd]
    if final:
        in_specs.append(pl.BlockSpec((1, d), cmap))
        args.append(final_g.reshape(1, d))
    return pl.pallas_call(
        functools.partial(_ffn_kernel, final=final),
        grid=(n // tm,),
        in_specs=in_specs,
        out_specs=pl.BlockSpec((tm, d), xmap),
        out_shape=jax.ShapeDtypeStruct((n, d), F32),
        compiler_params=pltpu.CompilerParams(
            dimension_semantics=("arbitrary",), vmem_limit_bytes=VMEM_LIMIT_BYTES),
        name="dense_ffn",
    )(*args)


def kernel(x, norm_mix_g, w_in, gdn_conv_w, gdn_a_log, gdn_dt_bias, gdn_norm_g, rwkv_mu, rwkv_w0, rwkv_w2, rwkv_a0, rwkv_a2, rwkv_g2, rwkv_k_k, rwkv_k_a, rwkv_r_k, rwkv_ln_g, rwkv_ln_b, w_branch_gdn, w_branch_rwkv, w_branch_sb, w_out, norm_ffn_g, ffn_w_gate, ffn_w_up, ffn_w_down, router_w, moe_w_gate, moe_w_up, moe_w_down, final_norm_g):
    b, t, d = x.shape
    n = b * t
    depth = w_in.shape[0]
    x2 = x.reshape(n, d).astype(F32)
    for layer in range(depth):
        w = w_in[layer]
        w_gdn = w[:, 0:2048].astype(BF16)
        w_ba = jnp.pad(w[:, 2048:2056], ((0, 0), (0, 120))).astype(BF16)
        w_rwkv = w[:, 2056:3848].astype(BF16)
        w_sb = w[:, 3848:5384].astype(BF16)
        w_gates = w[:, 5384:8456].astype(BF16)
        g_mix = norm_mix_g[layer].astype(F32)
        p_gdn, p_ba, p_rwkv = _normproj(x2, g_mix, [w_gdn, w_ba, w_rwkv], [F32, F32, F32])
        p_sb, p_gates = _normproj(x2, g_mix, [w_sb, w_gates], [F32, BF16])
        o_a = _gdn(p_gdn.reshape(b, t, -1), p_ba.reshape(b, t, -1), gdn_conv_w[layer], gdn_a_log[layer], gdn_dt_bias[layer],
                   gdn_norm_g[layer])
        o_b = _rwkv(p_rwkv.reshape(b, t, -1), rwkv_mu[layer], rwkv_w0[layer], rwkv_w2[layer],
                    rwkv_a0[layer], rwkv_a2[layer], rwkv_g2[layer], rwkv_k_k[layer], rwkv_k_a[layer],
                    rwkv_r_k[layer].reshape(-1), rwkv_ln_g[layer], rwkv_ln_b[layer])
        o_c = _stick_breaking(p_sb.reshape(b, t, -1))
        x2 = _merge(x2, o_a.reshape(n, -1), o_b.reshape(n, -1), o_c.reshape(n, -1), p_gates,
                    w_branch_gdn[layer].astype(BF16), w_branch_rwkv[layer].astype(BF16),
                    w_branch_sb[layer].astype(BF16), w_out[layer].astype(BF16))
        g_ffn = norm_ffn_g[layer].astype(F32)
        final_g = final_norm_g.astype(F32) if layer == depth - 1 else None
        i = layer // 2
        if layer % 2 == 0:
            x2 = _ffn(x2, g_ffn, ffn_w_gate[i].astype(BF16), ffn_w_up[i].astype(BF16),
                      ffn_w_down[i].astype(BF16), final_g=final_g)
        else:
            x2 = _moe(x2, g_ffn, router_w[i], moe_w_gate[i].astype(BF16), moe_w_up[i].astype(BF16),
                      moe_w_down[i].astype(BF16), final_g=final_g)
    return x2.reshape(b, t, d)
```

```python
import functools

import jax
import jax.numpy as jnp
from jax import lax
from jax.experimental import pallas as pl
from jax.experimental.pallas import tpu as pltpu

F32 = jnp.float32
BF16 = jnp.bfloat16

RMS_EPS = 1e-6
GN_EPS = 64e-5
L2_EPS = 1e-6

D_MODEL = 1024
GDN_HEADS = 4
GDN_HEAD_DIM = 128
GDN_WIDTH = 512
RWKV_WIDTH = 512
RWKV_HEAD_DIM = 64
RWKV_IN = 1792
SB_WIDTH = 512
SB_HEAD_DIM = 64
N_EXPERTS = 8
CHUNK = 64
LOG2E = 1.4426950408889634
SB_LOG2_CUTOFF = -160.0
SB_ROW_TILE = 128
SB_SKEW_CUMSUM = 3
SB_SKEW_VALUES = 6
GDN_CHUNKS_PER_STEP = 8
RWKV_CHUNKS_PER_STEP = 8
MOE_ROW_TILE = 512
VMEM_LIMIT_BYTES = 56 * 1024 * 1024


def _dot(a, b):
    return jnp.dot(a.astype(BF16), b.astype(BF16), preferred_element_type=F32)


def _dot_nt(a, b):
    return lax.dot_general(a.astype(BF16), b.astype(BF16), (((1,), (1,)), ((), ())),
                           preferred_element_type=F32)


def _split2(a):
    hi = a.astype(BF16)
    lo = (a - hi.astype(F32)).astype(BF16)
    return hi, lo


def _split3(a):
    hi = a.astype(BF16)
    r = a - hi.astype(F32)
    mid = r.astype(BF16)
    lo = (r - mid.astype(F32)).astype(BF16)
    return hi, mid, lo


def _dot3(a, b):
    ah, al = _split2(a)
    bh, bl = _split2(b)
    return (jnp.dot(ah, bh, preferred_element_type=F32)
            + jnp.dot(ah, bl, preferred_element_type=F32)
            + jnp.dot(al, bh, preferred_element_type=F32))


def _dot_exact_lhs(m, x, parts=3):
    xs = _split3(x) if parts == 3 else _split2(x)
    out = jnp.dot(m, xs[0], preferred_element_type=F32)
    for p in xs[1:]:
        out = out + jnp.dot(m, p, preferred_element_type=F32)
    return out


def _dot_exact_rhs(x, m, parts=2):
    xs = _split3(x) if parts == 3 else _split2(x)
    out = jnp.dot(xs[0], m, preferred_element_type=F32)
    for p in xs[1:]:
        out = out + jnp.dot(p, m, preferred_element_type=F32)
    return out


def _iota2(shape, dim):
    return lax.broadcasted_iota(jnp.int32, shape, dim)


def _eye(n, dtype=F32):
    return (_iota2((n, n), 0) == _iota2((n, n), 1)).astype(dtype)


def _softplus(x):
    return jnp.maximum(x, 0.0) + jnp.log(1.0 + jnp.exp(-jnp.abs(x)))


def _sigmoid(x):
    return 1.0 / (1.0 + jnp.exp(-x))


def _silu(x):
    return x * _sigmoid(x)


def _rms(x, g):
    return x * lax.rsqrt(jnp.mean(x * x, axis=-1, keepdims=True) + RMS_EPS) * g


def _nilpotent_inverse_many(ns, eye, dot):
    ts = [eye + n for n in ns]
    xs = [dot(n, n) for n in ns]
    for _ in range(4):
        xt = [dot(jnp.concatenate([x, t], axis=0), x) for x, t in zip(xs, ts)]
        xs = [y[0:CHUNK] for y in xt]
        ts = [t + y[CHUNK:2 * CHUNK] for t, y in zip(ts, xt)]
    return [t + dot(t, x) for t, x in zip(ts, xs)]


def _chunk_cumsum(x):
    i = _iota2((128, 128), 0)
    j = _iota2((128, 128), 1)
    m = ((j <= i) & ((i >> 6) == (j >> 6))).astype(BF16)
    return jnp.concatenate(
        [_dot_exact_lhs(m, x[r:r + 128]) for r in range(0, x.shape[0], 128)], axis=0)


def _head_sum(x):
    i = _iota2((128, 128), 0)
    j = _iota2((128, 128), 1)
    m = ((i >> 6) == (j >> 6)).astype(BF16)
    return jnp.concatenate(
        [_dot_exact_rhs(x[:, c:c + 128], m) for c in range(0, x.shape[1], 128)], axis=1)


def _normproj_kernel(x_ref, g_ref, *refs, n_out):
    w_refs = refs[:n_out]
    o_refs = refs[n_out:]
    hn = _rms(x_ref[...], g_ref[...]).astype(BF16)
    for w_ref, o_ref in zip(w_refs, o_refs):
        o_ref[...] = jnp.dot(hn, w_ref[...], preferred_element_type=F32).astype(o_ref.dtype)


def _normproj(x2d, g, weights, out_dtypes, tm=512):
    n, d = x2d.shape
    n_out = len(weights)
    in_specs = [pl.BlockSpec((tm, d), lambda i: (i, 0)), pl.BlockSpec((1, d), lambda i: (0, 0))]
    in_specs += [pl.BlockSpec(w.shape, lambda i: (0, 0), pipeline_mode=pl.Buffered(1)) for w in weights]
    out_specs = [pl.BlockSpec((tm, w.shape[1]), lambda i: (i, 0)) for w in weights]
    out_shape = [jax.ShapeDtypeStruct((n, w.shape[1]), dt) for w, dt in zip(weights, out_dtypes)]
    return pl.pallas_call(
        functools.partial(_normproj_kernel, n_out=n_out),
        grid=(n // tm,),
        in_specs=in_specs,
        out_specs=out_specs,
        out_shape=out_shape,
        compiler_params=pltpu.CompilerParams(
            dimension_semantics=("arbitrary",), vmem_limit_bytes=VMEM_LIMIT_BYTES),
        name="normproj",
    )(x2d, g.reshape(1, d), *weights)


def _gdn_kernel(qkv_ref, z_ref, ba_ref, cw_ref, alog_ref, dtb_ref, ng_ref, o_ref,
                ext_s, state_s, q_s, k_s, v_s, beta_s, gc_s, p_s, qq_s, r_s, zz_s, oc_s, *, tt):
    nc = tt // CHUNK
    w3 = 3 * GDN_WIDTH
    t = pl.program_id(1)

    @pl.when(t == 0)
    def _():
        ext_s[0:8, :] = jnp.zeros((8, w3), F32)
        state_s[...] = jnp.zeros_like(state_s)

    raw = qkv_ref[0]
    ext_s[8:8 + tt, :] = raw
    cw = cw_ref[...]
    y = raw * cw[3:4, :]
    for i in range(3):
        y = y + ext_s[5 + i:5 + i + tt, :] * cw[i:i + 1, :]
    ext_s[0:8, :] = raw[tt - 8:tt, :]
    y = _silu(y)

    for h in range(GDN_HEADS):
        sl = slice(128 * h, 128 * h + 128)
        qh = y[:, 128 * h:128 * h + 128]
        kh = y[:, GDN_WIDTH + 128 * h:GDN_WIDTH + 128 * h + 128]
        vh = y[:, 2 * GDN_WIDTH + 128 * h:2 * GDN_WIDTH + 128 * h + 128]
        qh = qh * lax.rsqrt(jnp.sum(qh * qh, axis=-1, keepdims=True) + L2_EPS) * (GDN_HEAD_DIM ** -0.5)
        kh = kh * lax.rsqrt(jnp.sum(kh * kh, axis=-1, keepdims=True) + L2_EPS)
        q_s[:, sl] = qh
        k_s[:, sl] = kh
        v_s[:, sl] = vh

    ba = ba_ref[0]
    gate = jnp.where(_iota2(ba.shape, 1) < GDN_HEADS, _sigmoid(ba),
                     -jnp.exp(alog_ref[...]) * _softplus(ba + dtb_ref[...]))
    ec = _iota2((128, 2 * GDN_WIDTH), 0)
    el = _iota2((128, 2 * GDN_WIDTH), 1)
    gate = _dot_exact_rhs(gate, (ec == (el >> 7)).astype(BF16), parts=3)
    beta_s[...] = gate[:, 0:GDN_WIDTH]
    gc_s[...] = _chunk_cumsum(gate[:, GDN_WIDTH:2 * GDN_WIDTH])

    eye64 = _eye(CHUNK)
    eye128 = _eye(128)
    ii = _iota2((CHUNK, CHUNK), 0)
    jj = _iota2((CHUNK, CHUNK), 1)

    def chunk_body(ci, carry):
        probs = [(ci * GDN_CHUNKS_PER_STEP + u, h) for u in range(GDN_CHUNKS_PER_STEP)
                 for h in range(GDN_HEADS)]
        rows = [pl.ds(pl.multiple_of(c * CHUNK, CHUNK), CHUNK) for c, _ in probs]
        lanes = [slice(128 * h, 128 * h + 128) for _, h in probs]
        idx = range(len(probs))
        q = [q_s[rows[i], lanes[i]] for i in idx]
        k = [k_s[rows[i], lanes[i]] for i in idx]
        gcc = [gc_s[rows[i], lanes[i]] for i in idx]
        gl = [g[CHUNK - 1:CHUNK, :] for g in gcc]
        gcr = [g.T[0:CHUNK, :] for g in gcc]
        dec_incl = [jnp.exp(jnp.where(ii >= jj, gcc[i][:, 0:CHUNK] - gcr[i], -jnp.inf)) for i in idx]
        kb = [k[i] * beta_s[rows[i], lanes[i]] for i in idx]
        a_mat = [_dot_nt(kb[i], k[i]) for i in idx]
        attn = [_dot_nt(q[i], k[i]) * dec_incl[i] for i in idx]
        kdt = [(k[i] * jnp.exp(gl[i] - gcc[i])).T for i in idx]
        tinv = _nilpotent_inverse_many(
            [-a_mat[i] * jnp.where(ii > jj, dec_incl[i], 0.0) for i in idx], eye64, _dot)
        u = [_dot(tinv[i], v_s[rows[i], lanes[i]] * beta_s[rows[i], lanes[i]]) for i in idx]
        w = [_dot(tinv[i], kb[i] * jnp.exp(gcc[i])) for i in idx]
        for i, (c, h) in enumerate(probs):
            p_s[c, h] = jnp.exp(gl[i]) * eye128 - _dot(kdt[i], w[i])
        for i, (c, h) in enumerate(probs):
            qq_s[c, h] = _dot(kdt[i], u[i])
        for i, (c, h) in enumerate(probs):
            r_s[c, h] = q[i] * jnp.exp(gcc[i]) - _dot(attn[i], w[i])
        for i, (c, h) in enumerate(probs):
            zz_s[c, h] = _dot(attn[i], u[i])
        return carry

    lax.fori_loop(0, nc // GDN_CHUNKS_PER_STEP, chunk_body, 0)

    def scan_body(c, carry):
        rows = pl.ds(pl.multiple_of(c * CHUNK, CHUNK), CHUNK)
        s = [state_s[h] for h in range(GDN_HEADS)]
        s_new = [_dot3(p_s[c, h], s[h]) for h in range(GDN_HEADS)]
        o = [_dot(r_s[c, h], s[h]) for h in range(GDN_HEADS)]
        for h in range(GDN_HEADS):
            state_s[h] = s_new[h] + qq_s[c, h]
            oc_s[rows, 128 * h:128 * h + 128] = o[h] + zz_s[c, h]
        return carry

    lax.fori_loop(0, nc, scan_body, 0)

    o = oc_s[...]
    z = z_ref[0]
    ng = ng_ref[...]
    for h in range(GDN_HEADS):
        sl = slice(128 * h, 128 * h + 128)
        oh = o[:, sl]
        oh = oh * lax.rsqrt(jnp.mean(oh * oh, axis=-1, keepdims=True) + RMS_EPS) * ng[:, sl]
        o_ref[0, :, sl] = (oh * _silu(z[:, sl])).astype(o_ref.dtype)


def _gdn(proj_a, proj_ba, conv_w, a_log, dt_bias, norm_g, tt=512):
    b, t, _ = proj_a.shape
    nc = tt // CHUNK
    w3 = 3 * GDN_WIDTH
    rep = lambda p: jnp.pad(p.astype(F32), (GDN_HEADS, 128 - 2 * GDN_HEADS)).reshape(1, 128)
    small = lambda shape: pl.BlockSpec(shape, lambda i, j: (0, 0))
    return pl.pallas_call(
        functools.partial(_gdn_kernel, tt=tt),
        grid=(b, t // tt),
        in_specs=[
            pl.BlockSpec((1, tt, w3), lambda i, j: (i, j, 0)),
            pl.BlockSpec((1, tt, GDN_WIDTH), lambda i, j: (i, j, 3)),
            pl.BlockSpec((1, tt, 128), lambda i, j: (i, j, 0)),
            small((4, w3)), small((1, 128)), small((1, 128)), small((1, GDN_WIDTH)),
        ],
        out_specs=pl.BlockSpec((1, tt, GDN_WIDTH), lambda i, j: (i, j, 0)),
        out_shape=jax.ShapeDtypeStruct((b, t, GDN_WIDTH), BF16),
        scratch_shapes=[
            pltpu.VMEM((tt + 8, w3), F32),
            pltpu.VMEM((GDN_HEADS, 128, 128), F32),
            pltpu.VMEM((tt, GDN_WIDTH), F32), pltpu.VMEM((tt, GDN_WIDTH), F32),
            pltpu.VMEM((tt, GDN_WIDTH), F32), pltpu.VMEM((tt, GDN_WIDTH), F32),
            pltpu.VMEM((tt, GDN_WIDTH), F32),
            pltpu.VMEM((nc, GDN_HEADS, 128, 128), F32), pltpu.VMEM((nc, GDN_HEADS, 128, 128), F32),
            pltpu.VMEM((nc, GDN_HEADS, CHUNK, 128), F32), pltpu.VMEM((nc, GDN_HEADS, CHUNK, 128), F32),
            pltpu.VMEM((tt, GDN_WIDTH), F32),
        ],
        compiler_params=pltpu.CompilerParams(
            dimension_semantics=("arbitrary", "arbitrary"), vmem_limit_bytes=VMEM_LIMIT_BYTES),
        name="gdn",
    )(proj_a, proj_a, proj_ba, conv_w.astype(F32), rep(a_log), rep(dt_bias),
      jnp.tile(norm_g.astype(F32), GDN_HEADS).reshape(1, GDN_WIDTH))


def _rwkv_kernel(h_ref, mu_ref, w0_ref, a0_ref, kk_ref, ka_ref, rk_ref, lng_ref, lnb_ref,
                 wa_ref, g2_ref, o_ref,
                 ext_s, state_s, r_s, kn_s, k2_s, v_s, a_s, lw_s, lc_s, p_s, qq_s, rh_s, yc_s, y_s,
                 *, tt):
    nc = tt // CHUNK
    npair = RWKV_WIDTH // 128
    t = pl.program_id(1)

    @pl.when(t == 0)
    def _():
        ext_s[0:8, :] = jnp.zeros((8, RWKV_IN), F32)
        state_s[...] = jnp.zeros_like(state_s)

    raw = h_ref[0]
    ext_s[8:8 + tt, :] = raw
    prev = ext_s[7:7 + tt, :]
    ext_s[0:8, :] = raw[tt - 8:tt, :]
    hl = raw + (prev - raw) * mu_ref[...]
    r = hl[:, 0:512]
    k = hl[:, 512:1024]
    v = hl[:, 1024:1536]
    xwa = hl[:, 1536:1664]
    xg = hl[:, 1664:1792]
    lane128 = _iota2((1, 128), 1)
    xwa = jnp.where(lane128 < 64, jnp.tanh(xwa), xwa)
    lora = _dot3(xwa, wa_ref[...])
    w_log = -_softplus(-(w0_ref[...] + lora[:, 0:512])) - 0.5
    lw = -jnp.exp(w_log)
    a = _sigmoid(a0_ref[...] + lora[:, 512:1024])
    gate = _dot3(_sigmoid(xg), g2_ref[...])
    kk = k * kk_ref[...]
    k2 = k * (1.0 + (a - 1.0) * ka_ref[...])
    kn = kk * lax.rsqrt(_head_sum(kk * kk) + L2_EPS)
    r_s[...] = r
    kn_s[...] = kn
    k2_s[...] = k2
    v_s[...] = v
    a_s[...] = a
    lw_s[...] = lw
    lc_s[...] = _chunk_cumsum(lw)

    eye128 = _eye(128)
    row = _iota2((CHUNK, 128), 0)
    lane = _iota2((CHUNK, 128), 1)
    col = lane & (CHUNK - 1)
    strict = row > col
    incl = row >= col
    eye2 = (row == col).astype(F32)
    head0 = lane < CHUNK
    head0_wide = (_iota2((CHUNK, 256), 1) & 127) < CHUNK
    bi = _iota2((128, 128), 0)
    bj = _iota2((128, 128), 1)
    blockdiag = (bi < 64) == (bj < 64)

    def per_head_rows(x, mask=head0):
        return jnp.concatenate([jnp.where(mask, x, 0.0), jnp.where(mask, 0.0, x)], axis=0)

    def chunk_body(ci, carry):
        units = [(ci * RWKV_CHUNKS_PER_STEP + u, p) for u in range(RWKV_CHUNKS_PER_STEP) for p in range(npair)]
        nu = range(len(units))
        rows = [pl.ds(pl.multiple_of(c * CHUNK, CHUNK), CHUNK) for c, _ in units]
        sls = [slice(128 * p, 128 * p + 128) for _, p in units]
        lcc = [lc_s[rows[u], sls[u]] for u in nu]
        vc = [v_s[rows[u], sls[u]] for u in nu]
        lcl = [x[CHUNK - 1:CHUNK, :] for x in lcc]
        ginv = [jnp.exp(-x) for x in lcc]
        gend = [jnp.exp(lcl[u] - lcc[u]) for u in nu]
        kna = [kn_s[rows[u], sls[u]] * a_s[rows[u], sls[u]] for u in nu]
        at = [-kn_s[rows[u], sls[u]] * jnp.exp(lcc[u] - lw_s[rows[u], sls[u]]) for u in nu]
        rt = [r_s[rows[u], sls[u]] * jnp.exp(lcc[u]) for u in nu]
        ar = [jnp.concatenate([at[u], rt[u]], axis=0) for u in nu]
        xb = [_dot_nt(ar[u], per_head_rows(kna[u] * ginv[u])) for u in nu]
        xk = [_dot_nt(ar[u], per_head_rows(k2_s[rows[u], sls[u]] * ginv[u])) for u in nu]
        a_ab = [jnp.where(strict, x[0:CHUNK], 0.0) for x in xb]
        a_rb = [jnp.where(incl, x[CHUNK:2 * CHUNK], 0.0) for x in xb]
        a_ak = [jnp.where(strict, x[0:CHUNK], 0.0) for x in xk]
        a_rk = [jnp.where(incl, x[CHUNK:2 * CHUNK], 0.0) for x in xk]
        vrows = [per_head_rows(vc[u]) for u in nu]
        akv = [_dot(a_ak[u], vrows[u]) for u in nu]
        arkv = [_dot(a_rk[u], vrows[u]) for u in nu]
        bbt = [(kna[u] * gend[u]).T for u in nu]
        kbt = [(k2_s[rows[u], sls[u]] * gend[u]).T for u in nu]
        ts = [eye2 + a for a in a_ab]
        xs = [_dot(x, per_head_rows(x)) for x in a_ab]
        for step in range(5):
            if step < 4:
                xt = [_dot(jnp.concatenate([x, t], axis=0), per_head_rows(x)) for x, t in zip(xs, ts)]
                xs = [y[0:CHUNK] for y in xt]
                ts = [t + y[CHUNK:2 * CHUNK] for t, y in zip(ts, xt)]
            else:
                ts = [t + _dot(t, per_head_rows(x)) for t, x in zip(ts, xs)]
        au = [_dot(ts[u], per_head_rows(jnp.concatenate([at[u], akv[u]], axis=1), head0_wide)) for u in nu]
        ry = [_dot(a_rb[u], per_head_rows(au[u], head0_wide)) for u in nu]
        kv = [_dot(kbt[u], vc[u]) for u in nu]
        pq = [_dot(bbt[u], au[u]) for u in nu]
        for u, (c, p) in enumerate(units):
            rh_s[c, p] = rt[u] + ry[u][:, 0:128]
            yc_s[c, p] = ry[u][:, 128:256] + arkv[u]
            p_s[c, p] = jnp.where(blockdiag, pq[u][:, 0:128], 0.0) + eye128 * jnp.exp(lcl[u])
            qq_s[c, p] = jnp.where(blockdiag, pq[u][:, 128:256] + kv[u], 0.0)
        return carry

    lax.fori_loop(0, nc // RWKV_CHUNKS_PER_STEP, chunk_body, 0)

    def scan_body(c, carry):
        rows = pl.ds(pl.multiple_of(c * CHUNK, CHUNK), CHUNK)
        s = [state_s[p] for p in range(npair)]
        s_new = [_dot3(p_s[c, p], s[p]) for p in range(npair)]
        y = [_dot(rh_s[c, p], s[p]) for p in range(npair)]
        for p in range(npair):
            state_s[p] = s_new[p] + qq_s[c, p]
            y_s[rows, 128 * p:128 * p + 128] = y[p] + yc_s[c, p]
        return carry

    lax.fori_loop(0, nc, scan_body, 0)

    y = y_s[...]
    mean = _head_sum(y) * (1.0 / RWKV_HEAD_DIM)
    yc = y - mean
    var = _head_sum(yc * yc) * (1.0 / RWKV_HEAD_DIM)
    yn = yc * lax.rsqrt(var + GN_EPS) * lng_ref[...] + lnb_ref[...]
    bonus = _head_sum(r * k2 * rk_ref[...]) * v
    o_ref[0] = ((yn + bonus) * gate).astype(o_ref.dtype)


def _dot3_nt(a, b):
    ah, al = _split2(a)
    bh, bl = _split2(b)
    dn = (((1,), (1,)), ((), ()))
    return (lax.dot_general(ah, bh, dn, preferred_element_type=F32)
            + lax.dot_general(ah, bl, dn, preferred_element_type=F32)
            + lax.dot_general(al, bh, dn, preferred_element_type=F32))


def _rwkv(h, mu, w0, w2, a0, a2, g2, k_k, k_a, r_k, ln_g, ln_b, tt=512):
    b, t, _ = h.shape
    nc = tt // CHUNK
    npair = RWKV_WIDTH // 128
    row = lambda p: p.astype(F32).reshape(1, -1)
    wa = jnp.zeros((128, 2 * RWKV_WIDTH), F32)
    wa = wa.at[0:64, 0:RWKV_WIDTH].set(w2.astype(F32)).at[64:128, RWKV_WIDTH:].set(a2.astype(F32))
    small = lambda shape: pl.BlockSpec(shape, lambda i, j: (0, 0))
    vec = small((1, RWKV_WIDTH))
    return pl.pallas_call(
        functools.partial(_rwkv_kernel, tt=tt),
        grid=(b, t // tt),
        in_specs=[pl.BlockSpec((1, tt, RWKV_IN), lambda i, j: (i, j, 0)), small((1, RWKV_IN)),
                  vec, vec, vec, vec, vec, vec, vec,
                  small((128, 2 * RWKV_WIDTH)), small((128, RWKV_WIDTH))],
        out_specs=pl.BlockSpec((1, tt, RWKV_WIDTH), lambda i, j: (i, j, 0)),
        out_shape=jax.ShapeDtypeStruct((b, t, RWKV_WIDTH), BF16),
        scratch_shapes=[
            pltpu.VMEM((tt + 8, RWKV_IN), F32),
            pltpu.VMEM((npair, 128, 128), F32),
        ] + [pltpu.VMEM((tt, RWKV_WIDTH), F32)] * 7 + [
            pltpu.VMEM((nc, npair, 128, 128), F32), pltpu.VMEM((nc, npair, 128, 128), F32),
            pltpu.VMEM((nc, npair, CHUNK, 128), F32), pltpu.VMEM((nc, npair, CHUNK, 128), F32),
            pltpu.VMEM((tt, RWKV_WIDTH), F32),
        ],
        compiler_params=pltpu.CompilerParams(
            dimension_semantics=("arbitrary", "arbitrary"), vmem_limit_bytes=VMEM_LIMIT_BYTES),
        name="rwkv7",
    )(h, row(mu), row(w0), row(a0), row(k_k), row(k_a), row(r_k), row(ln_g), row(ln_b),
      wa, g2.astype(F32))


def _sb_kernel(q_ref, k_ref, v_ref, o_ref, acc_s, aux_s, *, bq, bk):
    qi = pl.program_id(2)
    nsub = bq // bk
    q = q_ref[0] * (SB_HEAD_DIM ** -0.5 * LOG2E)
    lane = _iota2((1, 128), 1)
    qm = [jnp.where(lane < 64, q, 0.0).astype(BF16), jnp.where(lane >= 64, q, 0.0).astype(BF16)]
    ti = _iota2((2 * bk, bk), 0)
    tj = _iota2((2 * bk, bk), 1)
    cum_mat = ((ti & (bk - 1)) > tj).astype(BF16)
    acc_s[...] = jnp.zeros_like(acc_s)
    aux_s[...] = jnp.zeros_like(aux_s)

    def load_kv(kb):
        start = pl.multiple_of(kb * bk, bk)
        return (start, k_ref[0, pl.ds(start, bk), :].astype(BF16), v_ref[0, pl.ds(start, bk), :].astype(BF16))

    def stage_scores(item):
        (start, k, v), ra, nr, j, masked = item
        z = lax.dot_general(qm[j][ra:ra + nr], k, (((1,), (1,)), ((), ())), preferred_element_type=F32)
        return z

    def stage_cumsum(item, z):
        (start, k, v), ra, nr, j, masked = item
        rows = slice(ra, ra + nr)
        neg_abs = lax.bitcast_convert_type(
            lax.bitcast_convert_type(z, jnp.uint32) | jnp.uint32(0x80000000), F32)
        lsig = jnp.minimum(z, 0.0) - jnp.log(1.0 + jnp.exp2(neg_abs)) * LOG2E
        l1 = lsig - z
        msk = None
        if masked:
            msk = (start + _iota2((nr, bk), 1)) < (qi * bq + ra + _iota2((nr, bk), 0))
            l1 = jnp.where(msk, l1, 0.0)
        l1_hi = l1.astype(BF16)
        l1_lo = (l1 - l1_hi.astype(F32)).astype(BF16)
        cr = (jnp.dot(jnp.concatenate([l1_hi, l1_lo], axis=1), cum_mat, preferred_element_type=F32)
              + aux_s[j, rows, 0:1])
        aux_s[j, rows, :] = cr + l1
        return lsig, cr, msk

    def stage_values(item, state):
        (start, k, v), ra, nr, j, masked = item
        lsig, cr, msk = state
        att = jnp.exp2(lsig + cr)
        if masked:
            att = jnp.where(msk, att, 0.0)
        acc_s[j, ra:ra + nr, :] += jnp.dot(att.astype(BF16), v, preferred_element_type=F32)

    def run_items(items):
        n = len(items)
        zs = {}
        states = {}
        for s in range(n + SB_SKEW_VALUES):
            if s < n:
                zs[s] = stage_scores(items[s])
            if 0 <= s - SB_SKEW_CUMSUM < n:
                i = s - SB_SKEW_CUMSUM
                states[i] = stage_cumsum(items[i], zs.pop(i))
            if 0 <= s - SB_SKEW_VALUES < n:
                i = s - SB_SKEW_VALUES
                stage_values(items[i], states.pop(i))

    def max_carry(ra, nr):
        r = jnp.maximum(aux_s[0, ra:ra + nr, :], aux_s[1, ra:ra + nr, :])
        r = jnp.max(jnp.where(lane == 0, r, -jnp.inf), axis=0, keepdims=True)
        return jnp.max(r, axis=1, keepdims=True)[0, 0]

    items = []
    for d in range(nsub - 1, -1, -1):
        kv = load_kv(qi * nsub + d)
        ra = d * bk
        while ra < bq:
            nr = SB_ROW_TILE if (bq - ra) % (2 * SB_ROW_TILE) else 2 * SB_ROW_TILE
            items += [(kv, ra, nr, 0, True), (kv, ra, nr, 1, True)]
            ra += nr
    run_items(items)

    def cond(c):
        kb, rmax = c
        return (kb >= 0) & (rmax > SB_LOG2_CUTOFF)

    for ra in range(0, bq, 2 * SB_ROW_TILE):
        nr = 2 * SB_ROW_TILE

        def body(c, ra=ra, nr=nr):
            kb, _ = c
            items = []
            for kv in (load_kv(kb), load_kv(kb - 1)):
                items += [(kv, ra, nr, 0, False), (kv, ra, nr, 1, False)]
            run_items(items)
            return kb - 2, max_carry(ra, nr)

        lax.while_loop(cond, body, (qi * nsub - 1, max_carry(ra, nr)))
    o_ref[0] = jnp.where(lane < 64, acc_s[0], acc_s[1]).astype(o_ref.dtype)


def _stick_breaking(h_sb, bq=512, bk=128):
    b, t, _ = h_sb.shape
    npair = SB_WIDTH // 128
    return pl.pallas_call(
        functools.partial(_sb_kernel, bq=bq, bk=bk),
        grid=(b, npair, t // bq),
        in_specs=[
            pl.BlockSpec((1, bq, 128), lambda i, p, j: (i, j, p)),
            pl.BlockSpec((1, t, 128), lambda i, p, j: (i, 0, npair + p)),
            pl.BlockSpec((1, t, 128), lambda i, p, j: (i, 0, 2 * npair + p)),
        ],
        out_specs=pl.BlockSpec((1, bq, 128), lambda i, p, j: (i, j, p)),
        out_shape=jax.ShapeDtypeStruct((b, t, SB_WIDTH), BF16),
        scratch_shapes=[pltpu.VMEM((2, bq, 128), F32), pltpu.VMEM((2, bq, bk), F32)],
        compiler_params=pltpu.CompilerParams(
            dimension_semantics=("arbitrary", "arbitrary", "arbitrary"),
            vmem_limit_bytes=VMEM_LIMIT_BYTES),
        name="stickbreak",
    )(h_sb, h_sb, h_sb)


def _merge_kernel(x_ref, oa_ref, ob_ref, oc_ref, gt_ref, wa_ref, wb_ref, wc_ref, wo_ref, o_ref):
    d = D_MODEL
    g = _sigmoid(gt_ref[...].astype(F32))
    m = (g[:, 0:d] * _dot(oa_ref[...], wa_ref[...])
         + g[:, d:2 * d] * _dot(ob_ref[...], wb_ref[...])
         + g[:, 2 * d:3 * d] * _dot(oc_ref[...], wc_ref[...]))
    o_ref[...] = x_ref[...] + _dot(m, wo_ref[...])


def _merge(x2d, oa, ob, oc, gates, wa, wb, wc, wo, tm=512):
    n, d = x2d.shape
    rowspec = lambda w: pl.BlockSpec((tm, w), lambda i: (i, 0))
    full = lambda w: pl.BlockSpec(w.shape, lambda i: (0, 0))
    return pl.pallas_call(
        _merge_kernel,
        grid=(n // tm,),
        in_specs=[rowspec(d), rowspec(512), rowspec(512), rowspec(512), rowspec(3 * d),
                  full(wa), full(wb), full(wc), full(wo)],
        out_specs=rowspec(d),
        out_shape=jax.ShapeDtypeStruct((n, d), F32),
        compiler_params=pltpu.CompilerParams(
            dimension_semantics=("arbitrary",), vmem_limit_bytes=VMEM_LIMIT_BYTES),
        name="merge",
    )(x2d, oa, ob, oc, gates, wa, wb, wc, wo)


def _router_kernel(x_ref, g_ref, rw_ref, comb_ref, idx_ref, cnt_ref, run_s):
    @pl.when(pl.program_id(0) == 0)
    def _():
        run_s[...] = jnp.zeros_like(run_s)

    hn = _rms(x_ref[...], g_ref[...])
    logits = _dot3(hn, rw_ref[...])
    tm = logits.shape[0]
    lane = _iota2(logits.shape, 1)
    m1 = jnp.max(logits, axis=-1, keepdims=True)
    i1 = jnp.min(jnp.where(logits == m1, lane, N_EXPERTS), axis=-1, keepdims=True)
    sel1 = lane == i1
    rest = jnp.where(sel1, -jnp.inf, logits)
    m2 = jnp.max(rest, axis=-1, keepdims=True)
    i2 = jnp.min(jnp.where(rest == m2, lane, N_EXPERTS), axis=-1, keepdims=True)
    sel2 = lane == i2
    e2 = jnp.exp(m2 - m1)
    w1 = 1.0 / (1.0 + e2)
    comb_ref[...] = jnp.where(sel1, w1, 0.0) + jnp.where(sel2, e2 * w1, 0.0)

    member = (sel1 | sel2).astype(BF16)
    before = (_iota2((tm, tm), 0) > _iota2((tm, tm), 1)).astype(BF16)
    rank = jnp.dot(before, member, preferred_element_type=F32) + run_s[...]
    r1 = jnp.sum(jnp.where(sel1, rank, 0.0), axis=-1, keepdims=True).astype(jnp.int32)
    r2 = jnp.sum(jnp.where(sel2, rank, 0.0), axis=-1, keepdims=True).astype(jnp.int32)
    run_s[...] += jnp.sum(member.astype(F32), axis=0, keepdims=True)
    cnt_ref[...] = run_s[...]
    lane128 = _iota2((tm, 128), 1)
    idx_ref[...] = jnp.where(lane128 == 0, i1, jnp.where(lane128 == 1, i2,
                             jnp.where(lane128 == 2, r1, jnp.where(lane128 == 3, r2, 0))))


def _router(x2d, g, rw, tm=512):
    n, d = x2d.shape
    return pl.pallas_call(
        _router_kernel,
        grid=(n // tm,),
        in_specs=[pl.BlockSpec((tm, d), lambda i: (i, 0)), pl.BlockSpec((1, d), lambda i: (0, 0)),
                  pl.BlockSpec(rw.shape, lambda i: (0, 0))],
        out_specs=[pl.BlockSpec((tm, N_EXPERTS), lambda i: (i, 0)),
                   pl.BlockSpec((tm, 128), lambda i: (i, 0)),
                   pl.BlockSpec((1, N_EXPERTS), lambda i: (0, 0))],
        out_shape=[jax.ShapeDtypeStruct((n, N_EXPERTS), F32),
                   jax.ShapeDtypeStruct((n, 128), jnp.int32),
                   jax.ShapeDtypeStruct((1, N_EXPERTS), F32)],
        scratch_shapes=[pltpu.VMEM((1, N_EXPERTS), F32)],
        compiler_params=pltpu.CompilerParams(dimension_semantics=("arbitrary",)),
        name="router",
    )(x2d, g.reshape(1, d), rw.astype(F32))


def _row_copy(src_ref, src_row, dst_ref, dst_row, sem):
    return pltpu.make_async_copy(src_ref.at[pl.ds(src_row, 1), :], dst_ref.at[pl.ds(dst_row, 1), :], sem)


def _drain_rows(n_copies, src_ref, dst_ref, sem):
    def body(t, carry):
        _row_copy(src_ref, 0, dst_ref, 0, sem).wait()
        return carry

    lax.fori_loop(0, n_copies, body, 0, unroll=8)


def _dispatch_kernel(pos1_ref, pos2_ref, x_ref, g_ref, xs_in_ref, xs_ref, hn_s, sems):
    del xs_in_ref
    tm = hn_s.shape[1]
    i = pl.program_id(0)
    slot = i % 2
    hn_s[slot] = _rms(x_ref[...], g_ref[...])
    src = hn_s.at[slot]
    sem = sems.at[slot]

    def issue(t, carry):
        _row_copy(src, t, xs_ref, pos1_ref[t], sem).start(priority=0)
        _row_copy(src, t, xs_ref, pos2_ref[t], sem).start(priority=1)
        return carry

    lax.fori_loop(0, tm, issue, 0, unroll=8)

    @pl.when(i > 0)
    def _():
        _drain_rows(2 * tm, hn_s.at[1 - slot], xs_ref, sems.at[1 - slot])

    @pl.when(i == pl.num_programs(0) - 1)
    def _():
        _drain_rows(2 * tm, src, xs_ref, sem)


def _dispatch(x2d, g, pos1, pos2, n_rows, tm=512):
    n, d = x2d.shape
    smem = lambda: pl.BlockSpec((tm,), lambda i: (i,), memory_space=pltpu.SMEM)
    return pl.pallas_call(
        _dispatch_kernel,
        grid=(n // tm,),
        in_specs=[smem(), smem(), pl.BlockSpec((tm, d), lambda i: (i, 0)),
                  pl.BlockSpec((1, d), lambda i: (0, 0)), pl.BlockSpec(memory_space=pl.ANY)],
        out_specs=pl.BlockSpec(memory_space=pl.ANY),
        out_shape=jax.ShapeDtypeStruct((n_rows, d), F32),
        scratch_shapes=[pltpu.VMEM((2, tm, d), F32), pltpu.SemaphoreType.DMA((2,))],
        input_output_aliases={4: 0},
        compiler_params=pltpu.CompilerParams(dimension_semantics=("arbitrary",)),
        name="moe_dispatch",
    )(pos1, pos2, x2d, g.reshape(1, d), jnp.zeros((n_rows, d), F32))


def _grouped_ffn_kernel(te_ref, nt_ref, xs_ref, wg_ref, wu_ref, wd_ref, o_ref, xb_s, acc_s):
    i = pl.program_id(0)
    f = pl.program_id(1)

    @pl.when(f == 0)
    def _():
        xb_s[...] = xs_ref[...].astype(BF16)
        acc_s[...] = jnp.zeros_like(acc_s)

    @pl.when(i < nt_ref[0])
    def _():
        xb = xb_s[...]
        act = _silu(jnp.dot(xb, wg_ref[...], preferred_element_type=F32)) * jnp.dot(
            xb, wu_ref[...], preferred_element_type=F32)
        acc_s[...] += jnp.dot(act.astype(BF16), wd_ref[...], preferred_element_type=F32)

    @pl.when(f == pl.num_programs(1) - 1)
    def _():
        o_ref[...] = acc_s[...]


def _grouped_ffn(xs, tile_expert, n_tiles, wg, wu, wd, tm, tf=1792):
    n_rows, d = xs.shape
    dff = wg.shape[2]
    nf = dff // tf
    fsel = lambda i, f, te, nt: jnp.where(i < nt[0], f, nf - 1)
    return pl.pallas_call(
        _grouped_ffn_kernel,
        grid_spec=pltpu.PrefetchScalarGridSpec(
            num_scalar_prefetch=2,
            grid=(n_rows // tm, nf),
            in_specs=[pl.BlockSpec((tm, d), lambda i, f, te, nt: (i, 0)),
                      pl.BlockSpec((None, d, tf), lambda i, f, te, nt: (te[i], 0, fsel(i, f, te, nt))),
                      pl.BlockSpec((None, d, tf), lambda i, f, te, nt: (te[i], 0, fsel(i, f, te, nt))),
                      pl.BlockSpec((None, tf, d), lambda i, f, te, nt: (te[i], fsel(i, f, te, nt), 0))],
            out_specs=pl.BlockSpec((tm, d), lambda i, f, te, nt: (i, 0)),
            scratch_shapes=[pltpu.VMEM((tm, d), BF16), pltpu.VMEM((tm, d), F32)]),
        out_shape=jax.ShapeDtypeStruct((n_rows, d), F32),
        compiler_params=pltpu.CompilerParams(
            dimension_semantics=("arbitrary", "arbitrary"), vmem_limit_bytes=VMEM_LIMIT_BYTES),
        name="moe_grouped_ffn",
    )(tile_expert, n_tiles, xs, wg, wu, wd)


def _combine_kernel(*refs, final):
    if final:
        pos1_ref, pos2_ref, npos1_ref, npos2_ref, x_ref, comb_ref, fg_ref, ys_ref, o_ref, y1_s, y2_s, sems = refs
    else:
        pos1_ref, pos2_ref, npos1_ref, npos2_ref, x_ref, comb_ref, ys_ref, o_ref, y1_s, y2_s, sems = refs
    tm = y1_s.shape[1]
    i = pl.program_id(0)
    slot = i % 2

    def gather(p1_ref, p2_ref, sl):
        def issue(t, carry):
            _row_copy(ys_ref, p1_ref[t], y1_s.at[sl], t, sems.at[sl]).start(priority=0)
            _row_copy(ys_ref, p2_ref[t], y2_s.at[sl], t, sems.at[sl]).start(priority=1)
            return carry

        lax.fori_loop(0, tm, issue, 0, unroll=8)

    @pl.when(i == 0)
    def _():
        gather(pos1_ref, pos2_ref, 0)

    @pl.when(i < pl.num_programs(0) - 1)
    def _():
        gather(npos1_ref, npos2_ref, 1 - slot)

    _drain_rows(2 * tm, ys_ref, y1_s.at[slot], sems.at[slot])
    comb = comb_ref[...]
    w1 = jnp.max(comb, axis=-1, keepdims=True)
    w2 = jnp.sum(comb, axis=-1, keepdims=True) - w1
    out = x_ref[...] + w1 * y1_s[slot] + w2 * y2_s[slot]
    if final:
        out = _rms(out, fg_ref[...])
    o_ref[...] = out


def _combine(x2d, comb, pos1, pos2, ys, final_g=None, tm=512):
    n, d = x2d.shape
    final = final_g is not None
    nsteps = n // tm
    smem = lambda: pl.BlockSpec((tm,), lambda i: (i,), memory_space=pltpu.SMEM)
    smem_next = lambda: pl.BlockSpec((tm,), lambda i: (jnp.minimum(i + 1, nsteps - 1),),
                                     memory_space=pltpu.SMEM)
    in_specs = [smem(), smem(), smem_next(), smem_next(), pl.BlockSpec((tm, d), lambda i: (i, 0)),
                pl.BlockSpec((tm, N_EXPERTS), lambda i: (i, 0))]
    args = [pos1, pos2, pos1, pos2, x2d, comb]
    if final:
        in_specs.append(pl.BlockSpec((1, d), lambda i: (0, 0)))
        args.append(final_g.reshape(1, d))
    in_specs.append(pl.BlockSpec(memory_space=pl.ANY))
    args.append(ys)
    return pl.pallas_call(
        functools.partial(_combine_kernel, final=final),
        grid=(nsteps,),
        in_specs=in_specs,
        out_specs=pl.BlockSpec((tm, d), lambda i: (i, 0)),
        out_shape=jax.ShapeDtypeStruct((n, d), F32),
        scratch_shapes=[pltpu.VMEM((2, tm, d), F32), pltpu.VMEM((2, tm, d), F32),
                        pltpu.SemaphoreType.DMA((2,))],
        compiler_params=pltpu.CompilerParams(dimension_semantics=("arbitrary",)),
        name="moe_combine",
    )(*args)


def _moe(x2d, g, rw, wg, wu, wd, final_g=None, tm=MOE_ROW_TILE):
    n, d = x2d.shape
    comb, idx, cnt = _router(x2d, g, rw)
    counts = cnt[0].astype(jnp.int32)
    padded = ((counts + tm - 1) // tm) * tm
    ends = jnp.cumsum(padded)
    starts = ends - padded
    pos1 = starts[idx[:, 0]] + idx[:, 2]
    pos2 = starts[idx[:, 1]] + idx[:, 3]
    n_rows = 2 * n + N_EXPERTS * tm
    n_tiles = (ends[-1] // tm).reshape(1)
    tile_expert = jnp.minimum(
        jnp.searchsorted(ends // tm, jnp.arange(n_rows // tm, dtype=jnp.int32), side="right"),
        N_EXPERTS - 1).astype(jnp.int32)
    xs = _dispatch(x2d, g, pos1, pos2, n_rows)
    ys = _grouped_ffn(xs, tile_expert, n_tiles, wg, wu, wd, tm)
    return _combine(x2d, comb, pos1, pos2, ys, final_g=final_g)


def _ffn_kernel(*refs, final):
    if final:
        x_ref, g_ref, wg_ref, wu_ref, wd_ref, fg_ref, o_ref = refs
    else:
        x_ref, g_ref, wg_ref, wu_ref, wd_ref, o_ref = refs
    x = x_ref[...]
    hn = _rms(x, g_ref[...]).astype(BF16)
    act = _silu(jnp.dot(hn, wg_ref[...], preferred_element_type=F32)) * jnp.dot(
        hn, wu_ref[...], preferred_element_type=F32)
    out = x + jnp.dot(act.astype(BF16), wd_ref[...], preferred_element_type=F32)
    if final:
        out = _rms(out, fg_ref[...])
    o_ref[...] = out


def _ffn(x2d, g, wg, wu, wd, final_g=None, tm=512):
    n, d = x2d.shape
    final = final_g is not None
    xmap = lambda i: (i, 0)
    cmap = lambda i: (0, 0)
    resident = lambda w: pl.BlockSpec(w.shape, cmap, pipeline_mode=pl.Buffered(1))
    in_specs = [pl.BlockSpec((tm, d), xmap), pl.BlockSpec((1, d), cmap), resident(wg), resident(wu), resident(wd)]
    args = [x2d, g.reshape(1, d), wg, wu, wd]
    if final:
        in_specs.append(pl.BlockSpec((1, d), cmap))
        args.append(final_g.reshape(1, d))
    return pl.pallas_call(
        functools.partial(_ffn_kernel, final=final),
        grid=(n // tm,),
        in_specs=in_specs,
        out_specs=pl.BlockSpec((tm, d), xmap),
        out_shape=jax.ShapeDtypeStruct((n, d), F32),
        compiler_params=pltpu.CompilerParams(
            dimension_semantics=("arbitrary",), vmem_limit_bytes=VMEM_LIMIT_BYTES),
        name="dense_ffn",
    )(*args)


def kernel(x, norm_mix_g, w_in, gdn_conv_w, gdn_a_log, gdn_dt_bias, gdn_norm_g, rwkv_mu, rwkv_w0, rwkv_w2, rwkv_a0, rwkv_a2, rwkv_g2, rwkv_k_k, rwkv_k_a, rwkv_r_k, rwkv_ln_g, rwkv_ln_b, w_branch_gdn, w_branch_rwkv, w_branch_sb, w_out, norm_ffn_g, ffn_w_gate, ffn_w_up, ffn_w_down, router_w, moe_w_gate, moe_w_up, moe_w_down, final_norm_g):
    b, t, d = x.shape
    n = b * t
    depth = w_in.shape[0]
    x2 = x.reshape(n, d).astype(F32)
    for layer in range(depth):
        w = w_in[layer]
        w_gdn = w[:, 0:2048].astype(BF16)
        w_ba = jnp.pad(w[:, 2048:2056], ((0, 0), (0, 120))).astype(BF16)
        w_rwkv = w[:, 2056:3848].astype(BF16)
        w_sb = w[:, 3848:5384].astype(BF16)
        w_gates = w[:, 5384:8456].astype(BF16)
        g_mix = norm_mix_g[layer].astype(F32)
        p_gdn, p_ba, p_rwkv = _normproj(x2, g_mix, [w_gdn, w_ba, w_rwkv], [F32, F32, F32])
        p_sb, p_gates = _normproj(x2, g_mix, [w_sb, w_gates], [F32, BF16])
        o_a = _gdn(p_gdn.reshape(b, t, -1), p_ba.reshape(b, t, -1), gdn_conv_w[layer], gdn_a_log[layer], gdn_dt_bias[layer],
                   gdn_norm_g[layer])
        o_b = _rwkv(p_rwkv.reshape(b, t, -1), rwkv_mu[layer], rwkv_w0[layer], rwkv_w2[layer],
                    rwkv_a0[layer], rwkv_a2[layer], rwkv_g2[layer], rwkv_k_k[layer], rwkv_k_a[layer],
                    rwkv_r_k[layer].reshape(-1), rwkv_ln_g[layer], rwkv_ln_b[layer])
        o_c = _stick_breaking(p_sb.reshape(b, t, -1))
        x2 = _merge(x2, o_a.reshape(n, -1), o_b.reshape(n, -1), o_c.reshape(n, -1), p_gates,
                    w_branch_gdn[layer].astype(BF16), w_branch_rwkv[layer].astype(BF16),
                    w_branch_sb[layer].astype(BF16), w_out[layer].astype(BF16))
        g_ffn = norm_ffn_g[layer].astype(F32)
        final_g = final_norm_g.astype(F32) if layer == depth - 1 else None
        i = layer // 2
        if layer % 2 == 0:
            x2 = _ffn(x2, g_ffn, ffn_w_gate[i].astype(BF16), ffn_w_up[i].astype(BF16),
                      ffn_w_down[i].astype(BF16), final_g=final_g)
        else:
            x2 = _moe(x2, g_ffn, router_w[i], moe_w_gate[i].astype(BF16), moe_w_up[i].astype(BF16),
                      moe_w_down[i].astype(BF16), final_g=final_g)
    return x2.reshape(b, t, d)
```

```python
import functools

import jax
import jax.numpy as jnp
from jax import lax
from jax.experimental import pallas as pl
from jax.experimental.pallas import tpu as pltpu

F32 = jnp.float32
BF16 = jnp.bfloat16

RMS_EPS = 1e-6
GN_EPS = 64e-5
L2_EPS = 1e-6

D_MODEL = 1024
GDN_HEADS = 4
GDN_HEAD_DIM = 128
GDN_WIDTH = 512
RWKV_WIDTH = 512
RWKV_HEAD_DIM = 64
RWKV_IN = 1792
SB_WIDTH = 512
SB_HEAD_DIM = 64
N_EXPERTS = 8
CHUNK = 64
LOG2E = 1.4426950408889634
SB_LOG2_CUTOFF = -160.0
SB_ROW_TILE = 128
SB_SKEW_CUMSUM = 3
SB_SKEW_VALUES = 6
GDN_CHUNKS_PER_STEP = 8
RWKV_CHUNKS_PER_STEP = 8
MOE_ROW_TILE = 512
VMEM_LIMIT_BYTES = 56 * 1024 * 1024


def _dot(a, b):
    return jnp.dot(a.astype(BF16), b.astype(BF16), preferred_element_type=F32)


def _dot_nt(a, b):
    return lax.dot_general(a.astype(BF16), b.astype(BF16), (((1,), (1,)), ((), ())),
                           preferred_element_type=F32)


def _split2(a):
    hi = a.astype(BF16)
    lo = (a - hi.astype(F32)).astype(BF16)
    return hi, lo


def _split3(a):
    hi = a.astype(BF16)
    r = a - hi.astype(F32)
    mid = r.astype(BF16)
    lo = (r - mid.astype(F32)).astype(BF16)
    return hi, mid, lo


def _dot3(a, b):
    ah, al = _split2(a)
    bh, bl = _split2(b)
    return (jnp.dot(ah, bh, preferred_element_type=F32)
            + jnp.dot(ah, bl, preferred_element_type=F32)
            + jnp.dot(al, bh, preferred_element_type=F32))


def _dot_exact_lhs(m, x, parts=3):
    xs = _split3(x) if parts == 3 else _split2(x)
    out = jnp.dot(m, xs[0], preferred_element_type=F32)
    for p in xs[1:]:
        out = out + jnp.dot(m, p, preferred_element_type=F32)
    return out


def _dot_exact_rhs(x, m, parts=2):
    xs = _split3(x) if parts == 3 else _split2(x)
    out = jnp.dot(xs[0], m, preferred_element_type=F32)
    for p in xs[1:]:
        out = out + jnp.dot(p, m, preferred_element_type=F32)
    return out


def _iota2(shape, dim):
    return lax.broadcasted_iota(jnp.int32, shape, dim)


def _eye(n, dtype=F32):
    return (_iota2((n, n), 0) == _iota2((n, n), 1)).astype(dtype)


def _softplus(x):
    return jnp.maximum(x, 0.0) + jnp.log(1.0 + jnp.exp(-jnp.abs(x)))


def _sigmoid(x):
    return 1.0 / (1.0 + jnp.exp(-x))


def _silu(x):
    return x * _sigmoid(x)


def _rms(x, g):
    return x * lax.rsqrt(jnp.mean(x * x, axis=-1, keepdims=True) + RMS_EPS) * g


def _nilpotent_inverse_many(ns, eye, dot):
    ts = [eye + n for n in ns]
    xs = [dot(n, n) for n in ns]
    for _ in range(4):
        xt = [dot(jnp.concatenate([x, t], axis=0), x) for x, t in zip(xs, ts)]
        xs = [y[0:CHUNK] for y in xt]
        ts = [t + y[CHUNK:2 * CHUNK] for t, y in zip(ts, xt)]
    return [t + dot(t, x) for t, x in zip(ts, xs)]


def _chunk_cumsum(x):
    i = _iota2((128, 128), 0)
    j = _iota2((128, 128), 1)
    m = ((j <= i) & ((i >> 6) == (j >> 6))).astype(BF16)
    return jnp.concatenate(
        [_dot_exact_lhs(m, x[r:r + 128]) for r in range(0, x.shape[0], 128)], axis=0)


def _head_sum(x):
    i = _iota2((128, 128), 0)
    j = _iota2((128, 128), 1)
    m = ((i >> 6) == (j >> 6)).astype(BF16)
    return jnp.concatenate(
        [_dot_exact_rhs(x[:, c:c + 128], m) for c in range(0, x.shape[1], 128)], axis=1)


def _normproj_kernel(x_ref, g_ref, *refs, n_out):
    w_refs = refs[:n_out]
    o_refs = refs[n_out:]
    hn = _rms(x_ref[...], g_ref[...]).astype(BF16)
    for w_ref, o_ref in zip(w_refs, o_refs):
        o_ref[...] = jnp.dot(hn, w_ref[...], preferred_element_type=F32).astype(o_ref.dtype)


def _normproj(x2d, g, weights, out_dtypes, tm=512):
    n, d = x2d.shape
    n_out = len(weights)
    in_specs = [pl.BlockSpec((tm, d), lambda i: (i, 0)), pl.BlockSpec((1, d), lambda i: (0, 0))]
    in_specs += [pl.BlockSpec(w.shape, lambda i: (0, 0), pipeline_mode=pl.Buffered(1)) for w in weights]
    out_specs = [pl.BlockSpec((tm, w.shape[1]), lambda i: (i, 0)) for w in weights]
    out_shape = [jax.ShapeDtypeStruct((n, w.shape[1]), dt) for w, dt in zip(weights, out_dtypes)]
    return pl.pallas_call(
        functools.partial(_normproj_kernel, n_out=n_out),
        grid=(n // tm,),
        in_specs=in_specs,
        out_specs=out_specs,
        out_shape=out_shape,
        compiler_params=pltpu.CompilerParams(
            dimension_semantics=("arbitrary",), vmem_limit_bytes=VMEM_LIMIT_BYTES),
        name="normproj",
    )(x2d, g.reshape(1, d), *weights)


def _gdn_kernel(qkv_ref, z_ref, ba_ref, cw_ref, alog_ref, dtb_ref, ng_ref, o_ref,
                ext_s, state_s, q_s, k_s, v_s, beta_s, gc_s, p_s, qq_s, r_s, zz_s, oc_s, *, tt):
    nc = tt // CHUNK
    w3 = 3 * GDN_WIDTH
    t = pl.program_id(1)

    @pl.when(t == 0)
    def _():
        ext_s[0:8, :] = jnp.zeros((8, w3), F32)
        state_s[...] = jnp.zeros_like(state_s)

    raw = qkv_ref[0]
    ext_s[8:8 + tt, :] = raw
    cw = cw_ref[...]
    y = raw * cw[3:4, :]
    for i in range(3):
        y = y + ext_s[5 + i:5 + i + tt, :] * cw[i:i + 1, :]
    ext_s[0:8, :] = raw[tt - 8:tt, :]
    y = _silu(y)

    for h in range(GDN_HEADS):
        sl = slice(128 * h, 128 * h + 128)
        qh = y[:, 128 * h:128 * h + 128]
        kh = y[:, GDN_WIDTH + 128 * h:GDN_WIDTH + 128 * h + 128]
        vh = y[:, 2 * GDN_WIDTH + 128 * h:2 * GDN_WIDTH + 128 * h + 128]
        qh = qh * lax.rsqrt(jnp.sum(qh * qh, axis=-1, keepdims=True) + L2_EPS) * (GDN_HEAD_DIM ** -0.5)
        kh = kh * lax.rsqrt(jnp.sum(kh * kh, axis=-1, keepdims=True) + L2_EPS)
        q_s[:, sl] = qh
        k_s[:, sl] = kh
        v_s[:, sl] = vh

    ba = ba_ref[0]
    gate = jnp.where(_iota2(ba.shape, 1) < GDN_HEADS, _sigmoid(ba),
                     -jnp.exp(alog_ref[...]) * _softplus(ba + dtb_ref[...]))
    ec = _iota2((128, 2 * GDN_WIDTH), 0)
    el = _iota2((128, 2 * GDN_WIDTH), 1)
    gate = _dot_exact_rhs(gate, (ec == (el >> 7)).astype(BF16), parts=3)
    beta_s[...] = gate[:, 0:GDN_WIDTH]
    gc_s[...] = _chunk_cumsum(gate[:, GDN_WIDTH:2 * GDN_WIDTH])

    eye64 = _eye(CHUNK)
    eye128 = _eye(128)
    ii = _iota2((CHUNK, CHUNK), 0)
    jj = _iota2((CHUNK, CHUNK), 1)

    def chunk_body(ci, carry):
        probs = [(ci * GDN_CHUNKS_PER_STEP + u, h) for u in range(GDN_CHUNKS_PER_STEP)
                 for h in range(GDN_HEADS)]
        rows = [pl.ds(pl.multiple_of(c * CHUNK, CHUNK), CHUNK) for c, _ in probs]
        lanes = [slice(128 * h, 128 * h + 128) for _, h in probs]
        idx = range(len(probs))
        q = [q_s[rows[i], lanes[i]] for i in idx]
        k = [k_s[rows[i], lanes[i]] for i in idx]
        gcc = [gc_s[rows[i], lanes[i]] for i in idx]
        gl = [g[CHUNK - 1:CHUNK, :] for g in gcc]
        gcr = [g.T[0:CHUNK, :] for g in gcc]
        dec_incl = [jnp.exp(jnp.where(ii >= jj, gcc[i][:, 0:CHUNK] - gcr[i], -jnp.inf)) for i in idx]
        kb = [k[i] * beta_s[rows[i], lanes[i]] for i in idx]
        a_mat = [_dot_nt(kb[i], k[i]) for i in idx]
        attn = [_dot_nt(q[i], k[i]) * dec_incl[i] for i in idx]
        kdt = [(k[i] * jnp.exp(gl[i] - gcc[i])).T for i in idx]
        tinv = _nilpotent_inverse_many(
            [-a_mat[i] * jnp.where(ii > jj, dec_incl[i], 0.0) for i in idx], eye64, _dot)
        u = [_dot(tinv[i], v_s[rows[i], lanes[i]] * beta_s[rows[i], lanes[i]]) for i in idx]
        w = [_dot(tinv[i], kb[i] * jnp.exp(gcc[i])) for i in idx]
        for i, (c, h) in enumerate(probs):
            p_s[c, h] = jnp.exp(gl[i]) * eye128 - _dot(kdt[i], w[i])
        for i, (c, h) in enumerate(probs):
            qq_s[c, h] = _dot(kdt[i], u[i])
        for i, (c, h) in enumerate(probs):
            r_s[c, h] = q[i] * jnp.exp(gcc[i]) - _dot(attn[i], w[i])
        for i, (c, h) in enumerate(probs):
            zz_s[c, h] = _dot(attn[i], u[i])
        return carry

    lax.fori_loop(0, nc // GDN_CHUNKS_PER_STEP, chunk_body, 0)

    def scan_body(c, carry):
        rows = pl.ds(pl.multiple_of(c * CHUNK, CHUNK), CHUNK)
        s = [state_s[h] for h in range(GDN_HEADS)]
        s_new = [_dot3(p_s[c, h], s[h]) for h in range(GDN_HEADS)]
        o = [_dot(r_s[c, h], s[h]) for h in range(GDN_HEADS)]
        for h in range(GDN_HEADS):
            state_s[h] = s_new[h] + qq_s[c, h]
            oc_s[rows, 128 * h:128 * h + 128] = o[h] + zz_s[c, h]
        return carry

    lax.fori_loop(0, nc, scan_body, 0)

    o = oc_s[...]
    z = z_ref[0]
    ng = ng_ref[...]
    for h in range(GDN_HEADS):
        sl = slice(128 * h, 128 * h + 128)
        oh = o[:, sl]
        oh = oh * lax.rsqrt(jnp.mean(oh * oh, axis=-1, keepdims=True) + RMS_EPS) * ng[:, sl]
        o_ref[0, :, sl] = (oh * _silu(z[:, sl])).astype(o_ref.dtype)


def _gdn(proj_a, proj_ba, conv_w, a_log, dt_bias, norm_g, tt=512):
    b, t, _ = proj_a.shape
    nc = tt // CHUNK
    w3 = 3 * GDN_WIDTH
    rep = lambda p: jnp.pad(p.astype(F32), (GDN_HEADS, 128 - 2 * GDN_HEADS)).reshape(1, 128)
    small = lambda shape: pl.BlockSpec(shape, lambda i, j: (0, 0))
    return pl.pallas_call(
        functools.partial(_gdn_kernel, tt=tt),
        grid=(b, t // tt),
        in_specs=[
            pl.BlockSpec((1, tt, w3), lambda i, j: (i, j, 0)),
            pl.BlockSpec((1, tt, GDN_WIDTH), lambda i, j: (i, j, 3)),
            pl.BlockSpec((1, tt, 128), lambda i, j: (i, j, 0)),
            small((4, w3)), small((1, 128)), small((1, 128)), small((1, GDN_WIDTH)),
        ],
        out_specs=pl.BlockSpec((1, tt, GDN_WIDTH), lambda i, j: (i, j, 0)),
        out_shape=jax.ShapeDtypeStruct((b, t, GDN_WIDTH), BF16),
        scratch_shapes=[
            pltpu.VMEM((tt + 8, w3), F32),
            pltpu.VMEM((GDN_HEADS, 128, 128), F32),
            pltpu.VMEM((tt, GDN_WIDTH), F32), pltpu.VMEM((tt, GDN_WIDTH), F32),
            pltpu.VMEM((tt, GDN_WIDTH), F32), pltpu.VMEM((tt, GDN_WIDTH), F32),
            pltpu.VMEM((tt, GDN_WIDTH), F32),
            pltpu.VMEM((nc, GDN_HEADS, 128, 128), F32), pltpu.VMEM((nc, GDN_HEADS, 128, 128), F32),
            pltpu.VMEM((nc, GDN_HEADS, CHUNK, 128), F32), pltpu.VMEM((nc, GDN_HEADS, CHUNK, 128), F32),
            pltpu.VMEM((tt, GDN_WIDTH), F32),
        ],
        compiler_params=pltpu.CompilerParams(
            dimension_semantics=("arbitrary", "arbitrary"), vmem_limit_bytes=VMEM_LIMIT_BYTES),
        name="gdn",
    )(proj_a, proj_a, proj_ba, conv_w.astype(F32), rep(a_log), rep(dt_bias),
      jnp.tile(norm_g.astype(F32), GDN_HEADS).reshape(1, GDN_WIDTH))


def _rwkv_kernel(h_ref, mu_ref, w0_ref, a0_ref, kk_ref, ka_ref, rk_ref, lng_ref, lnb_ref,
                 wa_ref, g2_ref, o_ref,
                 ext_s, state_s, r_s, kn_s, k2_s, v_s, a_s, lw_s, lc_s, p_s, qq_s, rh_s, yc_s, y_s,
                 *, tt):
    nc = tt // CHUNK
    npair = RWKV_WIDTH // 128
    t = pl.program_id(1)

    @pl.when(t == 0)
    def _():
        ext_s[0:8, :] = jnp.zeros((8, RWKV_IN), F32)
        state_s[...] = jnp.zeros_like(state_s)

    raw = h_ref[0]
    ext_s[8:8 + tt, :] = raw
    prev = ext_s[7:7 + tt, :]
    ext_s[0:8, :] = raw[tt - 8:tt, :]
    hl = raw + (prev - raw) * mu_ref[...]
    r = hl[:, 0:512]
    k = hl[:, 512:1024]
    v = hl[:, 1024:1536]
    xwa = hl[:, 1536:1664]
    xg = hl[:, 1664:1792]
    lane128 = _iota2((1, 128), 1)
    xwa = jnp.where(lane128 < 64, jnp.tanh(xwa), xwa)
    lora = _dot3(xwa, wa_ref[...])
    w_log = -_softplus(-(w0_ref[...] + lora[:, 0:512])) - 0.5
    lw = -jnp.exp(w_log)
    a = _sigmoid(a0_ref[...] + lora[:, 512:1024])
    gate = _dot3(_sigmoid(xg), g2_ref[...])
    kk = k * kk_ref[...]
    k2 = k * (1.0 + (a - 1.0) * ka_ref[...])
    kn = kk * lax.rsqrt(_head_sum(kk * kk) + L2_EPS)
    r_s[...] = r
    kn_s[...] = kn
    k2_s[...] = k2
    v_s[...] = v
    a_s[...] = a
    lw_s[...] = lw
    lc_s[...] = _chunk_cumsum(lw)

    eye128 = _eye(128)
    row = _iota2((CHUNK, 128), 0)
    lane = _iota2((CHUNK, 128), 1)
    col = lane & (CHUNK - 1)
    strict = row > col
    incl = row >= col
    eye2 = (row == col).astype(F32)
    head0 = lane < CHUNK
    head0_wide = (_iota2((CHUNK, 256), 1) & 127) < CHUNK
    bi = _iota2((128, 128), 0)
    bj = _iota2((128, 128), 1)
    blockdiag = (bi < 64) == (bj < 64)

    def per_head_rows(x, mask=head0):
        return jnp.concatenate([jnp.where(mask, x, 0.0), jnp.where(mask, 0.0, x)], axis=0)

    def chunk_body(ci, carry):
        units = [(ci * RWKV_CHUNKS_PER_STEP + u, p) for u in range(RWKV_CHUNKS_PER_STEP) for p in range(npair)]
        nu = range(len(units))
        rows = [pl.ds(pl.multiple_of(c * CHUNK, CHUNK), CHUNK) for c, _ in units]
        sls = [slice(128 * p, 128 * p + 128) for _, p in units]
        lcc = [lc_s[rows[u], sls[u]] for u in nu]
        vc = [v_s[rows[u], sls[u]] for u in nu]
        lcl = [x[CHUNK - 1:CHUNK, :] for x in lcc]
        ginv = [jnp.exp(-x) for x in lcc]
        gend = [jnp.exp(lcl[u] - lcc[u]) for u in nu]
        kna = [kn_s[rows[u], sls[u]] * a_s[rows[u], sls[u]] for u in nu]
        at = [-kn_s[rows[u], sls[u]] * jnp.exp(lcc[u] - lw_s[rows[u], sls[u]]) for u in nu]
        rt = [r_s[rows[u], sls[u]] * jnp.exp(lcc[u]) for u in nu]
        ar = [jnp.concatenate([at[u], rt[u]], axis=0) for u in nu]
        xb = [_dot_nt(ar[u], per_head_rows(kna[u] * ginv[u])) for u in nu]
        xk = [_dot_nt(ar[u], per_head_rows(k2_s[rows[u], sls[u]] * ginv[u])) for u in nu]
        a_ab = [jnp.where(strict, x[0:CHUNK], 0.0) for x in xb]
        a_rb = [jnp.where(incl, x[CHUNK:2 * CHUNK], 0.0) for x in xb]
        a_ak = [jnp.where(strict, x[0:CHUNK], 0.0) for x in xk]
        a_rk = [jnp.where(incl, x[CHUNK:2 * CHUNK], 0.0) for x in xk]
        vrows = [per_head_rows(vc[u]) for u in nu]
        akv = [_dot(a_ak[u], vrows[u]) for u in nu]
        arkv = [_dot(a_rk[u], vrows[u]) for u in nu]
        bbt = [(kna[u] * gend[u]).T for u in nu]
        kbt = [(k2_s[rows[u], sls[u]] * gend[u]).T for u in nu]
        ts = [eye2 + a for a in a_ab]
        xs = [_dot(x, per_head_rows(x)) for x in a_ab]
        for step in range(5):
            if step < 4:
                xt = [_dot(jnp.concatenate([x, t], axis=0), per_head_rows(x)) for x, t in zip(xs, ts)]
                xs = [y[0:CHUNK] for y in xt]
                ts = [t + y[CHUNK:2 * CHUNK] for t, y in zip(ts, xt)]
            else:
                ts = [t + _dot(t, per_head_rows(x)) for t, x in zip(ts, xs)]
        au = [_dot(ts[u], per_head_rows(jnp.concatenate([at[u], akv[u]], axis=1), head0_wide)) for u in nu]
        ry = [_dot(a_rb[u], per_head_rows(au[u], head0_wide)) for u in nu]
        kv = [_dot(kbt[u], vc[u]) for u in nu]
        pq = [_dot(bbt[u], au[u]) for u in nu]
        for u, (c, p) in enumerate(units):
            rh_s[c, p] = rt[u] + ry[u][:, 0:128]
            yc_s[c, p] = ry[u][:, 128:256] + arkv[u]
            p_s[c, p] = jnp.where(blockdiag, pq[u][:, 0:128], 0.0) + eye128 * jnp.exp(lcl[u])
            qq_s[c, p] = jnp.where(blockdiag, pq[u][:, 128:256] + kv[u], 0.0)
        return carry

    lax.fori_loop(0, nc // RWKV_CHUNKS_PER_STEP, chunk_body, 0)

    def scan_body(c, carry):
        rows = pl.ds(pl.multiple_of(c * CHUNK, CHUNK), CHUNK)
        s = [state_s[p] for p in range(npair)]
        s_new = [_dot3(p_s[c, p], s[p]) for p in range(npair)]
        y = [_dot(rh_s[c, p], s[p]) for p in range(npair)]
        for p in range(npair):
            state_s[p] = s_new[p] + qq_s[c, p]
            y_s[rows, 128 * p:128 * p + 128] = y[p] + yc_s[c, p]
        return carry

    lax.fori_loop(0, nc, scan_body, 0)

    y = y_s[...]
    mean = _head_sum(y) * (1.0 / RWKV_HEAD_DIM)
    yc = y - mean
    var = _head_sum(yc * yc) * (1.0 / RWKV_HEAD_DIM)
    yn = yc * lax.rsqrt(var + GN_EPS) * lng_ref[...] + lnb_ref[...]
    bonus = _head_sum(r * k2 * rk_ref[...]) * v
    o_ref[0] = ((yn + bonus) * gate).astype(o_ref.dtype)


def _rwkv(h, mu, w0, w2, a0, a2, g2, k_k, k_a, r_k, ln_g, ln_b, tt=512):
    b, t, _ = h.shape
    nc = tt // CHUNK
    npair = RWKV_WIDTH // 128
    row = lambda p: p.astype(F32).reshape(1, -1)
    wa = jnp.zeros((128, 2 * RWKV_WIDTH), F32)
    wa = wa.at[0:64, 0:RWKV_WIDTH].set(w2.astype(F32)).at[64:128, RWKV_WIDTH:].set(a2.astype(F32))
    small = lambda shape: pl.BlockSpec(shape, lambda i, j: (0, 0))
    vec = small((1, RWKV_WIDTH))
    return pl.pallas_call(
        functools.partial(_rwkv_kernel, tt=tt),
        grid=(b, t // tt),
        in_specs=[pl.BlockSpec((1, tt, RWKV_IN), lambda i, j: (i, j, 0)), small((1, RWKV_IN)),
                  vec, vec, vec, vec, vec, vec, vec,
                  small((128, 2 * RWKV_WIDTH)), small((128, RWKV_WIDTH))],
        out_specs=pl.BlockSpec((1, tt, RWKV_WIDTH), lambda i, j: (i, j, 0)),
        out_shape=jax.ShapeDtypeStruct((b, t, RWKV_WIDTH), BF16),
        scratch_shapes=[
            pltpu.VMEM((tt + 8, RWKV_IN), F32),
            pltpu.VMEM((npair, 128, 128), F32),
        ] + [pltpu.VMEM((tt, RWKV_WIDTH), F32)] * 7 + [
            pltpu.VMEM((nc, npair, 128, 128), F32), pltpu.VMEM((nc, npair, 128, 128), F32),
            pltpu.VMEM((nc, npair, CHUNK, 128), F32), pltpu.VMEM((nc, npair, CHUNK, 128), F32),
            pltpu.VMEM((tt, RWKV_WIDTH), F32),
        ],
        compiler_params=pltpu.CompilerParams(
            dimension_semantics=("arbitrary", "arbitrary"), vmem_limit_bytes=VMEM_LIMIT_BYTES),
        name="rwkv7",
    )(h, row(mu), row(w0), row(a0), row(k_k), row(k_a), row(r_k), row(ln_g), row(ln_b),
      wa, g2.astype(F32))


def _sb_kernel(q_ref, k_ref, v_ref, o_ref, acc_s, aux_s, *, bq, bk):
    qi = pl.program_id(2)
    nsub = bq // bk
    q = q_ref[0] * (SB_HEAD_DIM ** -0.5 * LOG2E)
    lane = _iota2((1, 128), 1)
    qm = [jnp.where(lane < 64, q, 0.0).astype(BF16), jnp.where(lane >= 64, q, 0.0).astype(BF16)]
    ti = _iota2((2 * bk, bk), 0)
    tj = _iota2((2 * bk, bk), 1)
    cum_mat = ((ti & (bk - 1)) > tj).astype(BF16)
    acc_s[...] = jnp.zeros_like(acc_s)
    aux_s[...] = jnp.zeros_like(aux_s)

    def load_kv(kb):
        start = pl.multiple_of(kb * bk, bk)
        return (start, k_ref[0, pl.ds(start, bk), :].astype(BF16), v_ref[0, pl.ds(start, bk), :].astype(BF16))

    def stage_scores(item):
        (start, k, v), ra, nr, j, masked = item
        z = lax.dot_general(qm[j][ra:ra + nr], k, (((1,), (1,)), ((), ())), preferred_element_type=F32)
        return z

    def stage_cumsum(item, z):
        (start, k, v), ra, nr, j, masked = item
        rows = slice(ra, ra + nr)
        neg_abs = lax.bitcast_convert_type(
            lax.bitcast_convert_type(z, jnp.uint32) | jnp.uint32(0x80000000), F32)
        lsig = jnp.minimum(z, 0.0) - jnp.log(1.0 + jnp.exp2(neg_abs)) * LOG2E
        l1 = lsig - z
        msk = None
        if masked:
            msk = (start + _iota2((nr, bk), 1)) < (qi * bq + ra + _iota2((nr, bk), 0))
            l1 = jnp.where(msk, l1, 0.0)
        l1_hi = l1.astype(BF16)
        l1_lo = (l1 - l1_hi.astype(F32)).astype(BF16)
        cr = (jnp.dot(jnp.concatenate([l1_hi, l1_lo], axis=1), cum_mat, preferred_element_type=F32)
              + aux_s[j, rows, 0:1])
        aux_s[j, rows, :] = cr + l1
        return lsig, cr, msk

    def stage_values(item, state):
        (start, k, v), ra, nr, j, masked = item
        lsig, cr, msk = state
        att = jnp.exp2(lsig + cr)
        if masked:
            att = jnp.where(msk, att, 0.0)
        acc_s[j, ra:ra + nr, :] += jnp.dot(att.astype(BF16), v, preferred_element_type=F32)

    def run_items(items):
        n = len(items)
        zs = {}
        states = {}
        for s in range(n + SB_SKEW_VALUES):
            if s < n:
                zs[s] = stage_scores(items[s])
            if 0 <= s - SB_SKEW_CUMSUM < n:
                i = s - SB_SKEW_CUMSUM
                states[i] = stage_cumsum(items[i], zs.pop(i))
            if 0 <= s - SB_SKEW_VALUES < n:
                i = s - SB_SKEW_VALUES
                stage_values(items[i], states.pop(i))

    def max_carry(ra, nr):
        r = jnp.maximum(aux_s[0, ra:ra + nr, :], aux_s[1, ra:ra + nr, :])
        r = jnp.max(jnp.where(lane == 0, r, -jnp.inf), axis=0, keepdims=True)
        return jnp.max(r, axis=1, keepdims=True)[0, 0]

    items = []
    for d in range(nsub - 1, -1, -1):
        kv = load_kv(qi * nsub + d)
        ra = d * bk
        while ra < bq:
            nr = SB_ROW_TILE if (bq - ra) % (2 * SB_ROW_TILE) else 2 * SB_ROW_TILE
            items += [(kv, ra, nr, 0, True), (kv, ra, nr, 1, True)]
            ra += nr
    run_items(items)

    def cond(c):
        kb, rmax = c
        return (kb >= 0) & (rmax > SB_LOG2_CUTOFF)

    for ra in range(0, bq, 2 * SB_ROW_TILE):
        nr = 2 * SB_ROW_TILE

        def body(c, ra=ra, nr=nr):
            kb, _ = c
            items = []
            for kv in (load_kv(kb), load_kv(kb - 1)):
                items += [(kv, ra, nr, 0, False), (kv, ra, nr, 1, False)]
            run_items(items)
            return kb - 2, max_carry(ra, nr)

        lax.while_loop(cond, body, (qi * nsub - 1, max_carry(ra, nr)))
    o_ref[0] = jnp.where(lane < 64, acc_s[0], acc_s[1]).astype(o_ref.dtype)


def _stick_breaking(h_sb, bq=512, bk=128):
    b, t, _ = h_sb.shape
    npair = SB_WIDTH // 128
    return pl.pallas_call(
        functools.partial(_sb_kernel, bq=bq, bk=bk),
        grid=(b, npair, t // bq),
        in_specs=[
            pl.BlockSpec((1, bq, 128), lambda i, p, j: (i, j, p)),
            pl.BlockSpec((1, t, 128), lambda i, p, j: (i, 0, npair + p)),
            pl.BlockSpec((1, t, 128), lambda i, p, j: (i, 0, 2 * npair + p)),
        ],
        out_specs=pl.BlockSpec((1, bq, 128), lambda i, p, j: (i, j, p)),
        out_shape=jax.ShapeDtypeStruct((b, t, SB_WIDTH), BF16),
        scratch_shapes=[pltpu.VMEM((2, bq, 128), F32), pltpu.VMEM((2, bq, bk), F32)],
        compiler_params=pltpu.CompilerParams(
            dimension_semantics=("arbitrary", "arbitrary", "arbitrary"),
            vmem_limit_bytes=VMEM_LIMIT_BYTES),
        name="stickbreak",
    )(h_sb, h_sb, h_sb)


def _merge_kernel(x_ref, oa_ref, ob_ref, oc_ref, gt_ref, wa_ref, wb_ref, wc_ref, wo_ref, o_ref):
    d = D_MODEL
    g = _sigmoid(gt_ref[...].astype(F32))
    m = (g[:, 0:d] * _dot(oa_ref[...], wa_ref[...])
         + g[:, d:2 * d] * _dot(ob_ref[...], wb_ref[...])
         + g[:, 2 * d:3 * d] * _dot(oc_ref[...], wc_ref[...]))
    o_ref[...] = x_ref[...] + _dot(m, wo_ref[...])


def _merge(x2d, oa, ob, oc, gates, wa, wb, wc, wo, tm=512):
    n, d = x2d.shape
    rowspec = lambda w: pl.BlockSpec((tm, w), lambda i: (i, 0))
    full = lambda w: pl.BlockSpec(w.shape, lambda i: (0, 0))
    return pl.pallas_call(
        _merge_kernel,
        grid=(n // tm,),
        in_specs=[rowspec(d), rowspec(512), rowspec(512), rowspec(512), rowspec(3 * d),
                  full(wa), full(wb), full(wc), full(wo)],
        out_specs=rowspec(d),
        out_shape=jax.ShapeDtypeStruct((n, d), F32),
        compiler_params=pltpu.CompilerParams(
            dimension_semantics=("arbitrary",), vmem_limit_bytes=VMEM_LIMIT_BYTES),
        name="merge",
    )(x2d, oa, ob, oc, gates, wa, wb, wc, wo)


def _router_kernel(x_ref, g_ref, rw_ref, comb_ref, idx_ref, cnt_ref, run_s):
    @pl.when(pl.program_id(0) == 0)
    def _():
        run_s[...] = jnp.zeros_like(run_s)

    hn = _rms(x_ref[...], g_ref[...])
    logits = _dot3(hn, rw_ref[...])
    tm = logits.shape[0]
    lane = _iota2(logits.shape, 1)
    m1 = jnp.max(logits, axis=-1, keepdims=True)
    i1 = jnp.min(jnp.where(logits == m1, lane, N_EXPERTS), axis=-1, keepdims=True)
    sel1 = lane == i1
    rest = jnp.where(sel1, -jnp.inf, logits)
    m2 = jnp.max(rest, axis=-1, keepdims=True)
    i2 = jnp.min(jnp.where(rest == m2, lane, N_EXPERTS), axis=-1, keepdims=True)
    sel2 = lane == i2
    e2 = jnp.exp(m2 - m1)
    w1 = 1.0 / (1.0 + e2)
    comb_ref[...] = jnp.where(sel1, w1, 0.0) + jnp.where(sel2, e2 * w1, 0.0)

    member = (sel1 | sel2).astype(BF16)
    before = (_iota2((tm, tm), 0) > _iota2((tm, tm), 1)).astype(BF16)
    rank = jnp.dot(before, member, preferred_element_type=F32) + run_s[...]
    r1 = jnp.sum(jnp.where(sel1, rank, 0.0), axis=-1, keepdims=True).astype(jnp.int32)
    r2 = jnp.sum(jnp.where(sel2, rank, 0.0), axis=-1, keepdims=True).astype(jnp.int32)
    run_s[...] += jnp.sum(member.astype(F32), axis=0, keepdims=True)
    cnt_ref[...] = run_s[...]
    idx_ref[...] = jnp.where(lane == 0, i1, jnp.where(lane == 1, i2, jnp.where(lane == 2, r1,
                             jnp.where(lane == 3, r2, 0))))


def _router(x2d, g, rw, tm=512):
    n, d = x2d.shape
    return pl.pallas_call(
        _router_kernel,
        grid=(n // tm,),
        in_specs=[pl.BlockSpec((tm, d), lambda i: (i, 0)), pl.BlockSpec((1, d), lambda i: (0, 0)),
                  pl.BlockSpec(rw.shape, lambda i: (0, 0))],
        out_specs=[pl.BlockSpec((tm, N_EXPERTS), lambda i: (i, 0)),
                   pl.BlockSpec((tm, N_EXPERTS), lambda i: (i, 0)),
                   pl.BlockSpec((1, N_EXPERTS), lambda i: (0, 0))],
        out_shape=[jax.ShapeDtypeStruct((n, N_EXPERTS), F32),
                   jax.ShapeDtypeStruct((n, N_EXPERTS), jnp.int32),
                   jax.ShapeDtypeStruct((1, N_EXPERTS), F32)],
        scratch_shapes=[pltpu.VMEM((1, N_EXPERTS), F32)],
        compiler_params=pltpu.CompilerParams(dimension_semantics=("arbitrary",)),
        name="router",
    )(x2d, g.reshape(1, d), rw.astype(F32))


def _row_copy(src_ref, src_row, dst_ref, dst_row, sem):
    return pltpu.make_async_copy(src_ref.at[pl.ds(src_row, 1), :], dst_ref.at[pl.ds(dst_row, 1), :], sem)


def _drain_rows(n_copies, src_ref, dst_ref, sem):
    def body(t, carry):
        _row_copy(src_ref, 0, dst_ref, 0, sem).wait()
        return carry

    lax.fori_loop(0, n_copies, body, 0, unroll=8)


def _dispatch_kernel(pos1_ref, pos2_ref, x_ref, g_ref, xs_in_ref, xs_ref, hn_s, sems):
    del xs_in_ref
    tm = hn_s.shape[1]
    i = pl.program_id(0)
    slot = i % 2
    hn_s[slot] = _rms(x_ref[...], g_ref[...])
    src = hn_s.at[slot]
    sem = sems.at[slot]

    def issue(t, carry):
        _row_copy(src, t, xs_ref, pos1_ref[t], sem).start()
        _row_copy(src, t, xs_ref, pos2_ref[t], sem).start()
        return carry

    lax.fori_loop(0, tm, issue, 0, unroll=8)

    @pl.when(i > 0)
    def _():
        _drain_rows(2 * tm, hn_s.at[1 - slot], xs_ref, sems.at[1 - slot])

    @pl.when(i == pl.num_programs(0) - 1)
    def _():
        _drain_rows(2 * tm, src, xs_ref, sem)


def _dispatch(x2d, g, pos1, pos2, n_rows, tm=512):
    n, d = x2d.shape
    smem = lambda: pl.BlockSpec((tm,), lambda i: (i,), memory_space=pltpu.SMEM)
    return pl.pallas_call(
        _dispatch_kernel,
        grid=(n // tm,),
        in_specs=[smem(), smem(), pl.BlockSpec((tm, d), lambda i: (i, 0)),
                  pl.BlockSpec((1, d), lambda i: (0, 0)), pl.BlockSpec(memory_space=pl.ANY)],
        out_specs=pl.BlockSpec(memory_space=pl.ANY),
        out_shape=jax.ShapeDtypeStruct((n_rows, d), F32),
        scratch_shapes=[pltpu.VMEM((2, tm, d), F32), pltpu.SemaphoreType.DMA((2,))],
        input_output_aliases={4: 0},
        compiler_params=pltpu.CompilerParams(dimension_semantics=("arbitrary",)),
        name="moe_dispatch",
    )(pos1, pos2, x2d, g.reshape(1, d), jnp.zeros((n_rows, d), F32))


def _grouped_ffn_kernel(te_ref, nt_ref, xs_ref, wg_ref, wu_ref, wd_ref, o_ref, xb_s, acc_s):
    i = pl.program_id(0)
    f = pl.program_id(1)

    @pl.when(f == 0)
    def _():
        xb_s[...] = xs_ref[...].astype(BF16)
        acc_s[...] = jnp.zeros_like(acc_s)

    @pl.when(i < nt_ref[0])
    def _():
        xb = xb_s[...]
        act = _silu(jnp.dot(xb, wg_ref[...], preferred_element_type=F32)) * jnp.dot(
            xb, wu_ref[...], preferred_element_type=F32)
        acc_s[...] += jnp.dot(act.astype(BF16), wd_ref[...], preferred_element_type=F32)

    @pl.when(f == pl.num_programs(1) - 1)
    def _():
        o_ref[...] = acc_s[...]


def _grouped_ffn(xs, tile_expert, n_tiles, wg, wu, wd, tm, tf=1792):
    n_rows, d = xs.shape
    dff = wg.shape[2]
    nf = dff // tf
    fsel = lambda i, f, te, nt: jnp.where(i < nt[0], f, nf - 1)
    return pl.pallas_call(
        _grouped_ffn_kernel,
        grid_spec=pltpu.PrefetchScalarGridSpec(
            num_scalar_prefetch=2,
            grid=(n_rows // tm, nf),
            in_specs=[pl.BlockSpec((tm, d), lambda i, f, te, nt: (i, 0)),
                      pl.BlockSpec((None, d, tf), lambda i, f, te, nt: (te[i], 0, fsel(i, f, te, nt))),
                      pl.BlockSpec((None, d, tf), lambda i, f, te, nt: (te[i], 0, fsel(i, f, te, nt))),
                      pl.BlockSpec((None, tf, d), lambda i, f, te, nt: (te[i], fsel(i, f, te, nt), 0))],
            out_specs=pl.BlockSpec((tm, d), lambda i, f, te, nt: (i, 0)),
            scratch_shapes=[pltpu.VMEM((tm, d), BF16), pltpu.VMEM((tm, d), F32)]),
        out_shape=jax.ShapeDtypeStruct((n_rows, d), F32),
        compiler_params=pltpu.CompilerParams(
            dimension_semantics=("arbitrary", "arbitrary"), vmem_limit_bytes=VMEM_LIMIT_BYTES),
        name="moe_grouped_ffn",
    )(tile_expert, n_tiles, xs, wg, wu, wd)


def _combine_kernel(*refs, final):
    if final:
        pos1_ref, pos2_ref, npos1_ref, npos2_ref, x_ref, comb_ref, fg_ref, ys_ref, o_ref, y1_s, y2_s, sems = refs
    else:
        pos1_ref, pos2_ref, npos1_ref, npos2_ref, x_ref, comb_ref, ys_ref, o_ref, y1_s, y2_s, sems = refs
    tm = y1_s.shape[1]
    i = pl.program_id(0)
    slot = i % 2

    def gather(p1_ref, p2_ref, sl):
        def issue(t, carry):
            _row_copy(ys_ref, p1_ref[t], y1_s.at[sl], t, sems.at[sl]).start()
            _row_copy(ys_ref, p2_ref[t], y2_s.at[sl], t, sems.at[sl]).start()
            return carry

        lax.fori_loop(0, tm, issue, 0, unroll=8)

    @pl.when(i == 0)
    def _():
        gather(pos1_ref, pos2_ref, 0)

    @pl.when(i < pl.num_programs(0) - 1)
    def _():
        gather(npos1_ref, npos2_ref, 1 - slot)

    _drain_rows(2 * tm, ys_ref, y1_s.at[slot], sems.at[slot])
    comb = comb_ref[...]
    w1 = jnp.max(comb, axis=-1, keepdims=True)
    w2 = jnp.sum(comb, axis=-1, keepdims=True) - w1
    out = x_ref[...] + w1 * y1_s[slot] + w2 * y2_s[slot]
    if final:
        out = _rms(out, fg_ref[...])
    o_ref[...] = out


def _combine(x2d, comb, pos1, pos2, ys, final_g=None, tm=512):
    n, d = x2d.shape
    final = final_g is not None
    nsteps = n // tm
    smem = lambda: pl.BlockSpec((tm,), lambda i: (i,), memory_space=pltpu.SMEM)
    smem_next = lambda: pl.BlockSpec((tm,), lambda i: (jnp.minimum(i + 1, nsteps - 1),),
                                     memory_space=pltpu.SMEM)
    in_specs = [smem(), smem(), smem_next(), smem_next(), pl.BlockSpec((tm, d), lambda i: (i, 0)),
                pl.BlockSpec((tm, N_EXPERTS), lambda i: (i, 0))]
    args = [pos1, pos2, pos1, pos2, x2d, comb]
    if final:
        in_specs.append(pl.BlockSpec((1, d), lambda i: (0, 0)))
        args.append(final_g.reshape(1, d))
    in_specs.append(pl.BlockSpec(memory_space=pl.ANY))
    args.append(ys)
    return pl.pallas_call(
        functools.partial(_combine_kernel, final=final),
        grid=(nsteps,),
        in_specs=in_specs,
        out_specs=pl.BlockSpec((tm, d), lambda i: (i, 0)),
        out_shape=jax.ShapeDtypeStruct((n, d), F32),
        scratch_shapes=[pltpu.VMEM((2, tm, d), F32), pltpu.VMEM((2, tm, d), F32),
                        pltpu.SemaphoreType.DMA((2,))],
        compiler_params=pltpu.CompilerParams(dimension_semantics=("arbitrary",)),
        name="moe_combine",
    )(*args)


def _moe(x2d, g, rw, wg, wu, wd, final_g=None, tm=MOE_ROW_TILE):
    n, d = x2d.shape
    comb, idx, cnt = _router(x2d, g, rw)
    counts = cnt[0].astype(jnp.int32)
    padded = ((counts + tm - 1) // tm) * tm
    ends = jnp.cumsum(padded)
    starts = ends - padded
    pos1 = starts[idx[:, 0]] + idx[:, 2]
    pos2 = starts[idx[:, 1]] + idx[:, 3]
    n_rows = 2 * n + N_EXPERTS * tm
    n_tiles = (ends[-1] // tm).reshape(1)
    tile_ids = jnp.arange(n_rows // tm, dtype=jnp.int32)
    tile_expert = jnp.minimum(
        jnp.sum((tile_ids[:, None] >= (ends // tm)[None, :]).astype(jnp.int32), axis=1), N_EXPERTS - 1)
    xs = _dispatch(x2d, g, pos1, pos2, n_rows)
    ys = _grouped_ffn(xs, tile_expert, n_tiles, wg, wu, wd, tm)
    return _combine(x2d, comb, pos1, pos2, ys, final_g=final_g)


def _ffn_kernel(*refs, final):
    if final:
        x_ref, g_ref, wg_ref, wu_ref, wd_ref, fg_ref, o_ref = refs
    else:
        x_ref, g_ref, wg_ref, wu_ref, wd_ref, o_ref = refs
    x = x_ref[...]
    hn = _rms(x, g_ref[...]).astype(BF16)
    act = _silu(jnp.dot(hn, wg_ref[...], preferred_element_type=F32)) * jnp.dot(
        hn, wu_ref[...], preferred_element_type=F32)
    out = x + jnp.dot(act.astype(BF16), wd_ref[...], preferred_element_type=F32)
    if final:
        out = _rms(out, fg_ref[...])
    o_ref[...] = out


def _ffn(x2d, g, wg, wu, wd, final_g=None, tm=512):
    n, d = x2d.shape
    final = final_g is not None
    xmap = lambda i: (i, 0)
    cmap = lambda i: (0, 0)
    resident = lambda w: pl.BlockSpec(w.shape, cmap, pipeline_mode=pl.Buffered(1))
    in_specs = [pl.BlockSpec((tm, d), xmap), pl.BlockSpec((1, d), cmap), resident(wg), resident(wu), resident(wd)]
    args = [x2d, g.reshape(1, d), wg, wu, wd]
    if final:
        in_specs.append(pl.BlockSpec((1, d), cmap))
        args.append(final_g.reshape(1, d))
    return pl.pallas_call(
        functools.partial(_ffn_kernel, final=final),
        grid=(n // tm,),
        in_specs=in_specs,
        out_specs=pl.BlockSpec((tm, d), xmap),
        out_shape=jax.ShapeDtypeStruct((n, d), F32),
        compiler_params=pltpu.CompilerParams(
            dimension_semantics=("arbitrary",), vmem_limit_bytes=VMEM_LIMIT_BYTES),
        name="dense_ffn",
    )(*args)


def kernel(x, norm_mix_g, w_in, gdn_conv_w, gdn_a_log, gdn_dt_bias, gdn_norm_g, rwkv_mu, rwkv_w0, rwkv_w2, rwkv_a0, rwkv_a2, rwkv_g2, rwkv_k_k, rwkv_k_a, rwkv_r_k, rwkv_ln_g, rwkv_ln_b, w_branch_gdn, w_branch_rwkv, w_branch_sb, w_out, norm_ffn_g, ffn_w_gate, ffn_w_up, ffn_w_down, router_w, moe_w_gate, moe_w_up, moe_w_down, final_norm_g):
    b, t, d = x.shape
    n = b * t
    depth = w_in.shape[0]
    x2 = x.reshape(n, d).astype(F32)
    for layer in range(depth):
        w = w_in[layer]
        w_gdn = w[:, 0:2048].astype(BF16)
        w_ba = jnp.pad(w[:, 2048:2056], ((0, 0), (0, 120))).astype(BF16)
        w_rwkv = w[:, 2056:3848].astype(BF16)
        w_sb = w[:, 3848:5384].astype(BF16)
        w_gates = w[:, 5384:8456].astype(BF16)
        g_mix = norm_mix_g[layer].astype(F32)
        p_gdn, p_ba, p_rwkv = _normproj(x2, g_mix, [w_gdn, w_ba, w_rwkv], [F32, F32, F32])
        p_sb, p_gates = _normproj(x2, g_mix, [w_sb, w_gates], [F32, BF16])
        o_a = _gdn(p_gdn.reshape(b, t, -1), p_ba.reshape(b, t, -1), gdn_conv_w[layer], gdn_a_log[layer], gdn_dt_bias[layer],
                   gdn_norm_g[layer])
        o_b = _rwkv(p_rwkv.reshape(b, t, -1), rwkv_mu[layer], rwkv_w0[layer], rwkv_w2[layer],
                    rwkv_a0[layer], rwkv_a2[layer], rwkv_g2[layer], rwkv_k_k[layer], rwkv_k_a[layer],
                    rwkv_r_k[layer].reshape(-1), rwkv_ln_g[layer], rwkv_ln_b[layer])
        o_c = _stick_breaking(p_sb.reshape(b, t, -1))
        x2 = _merge(x2, o_a.reshape(n, -1), o_b.reshape(n, -1), o_c.reshape(n, -1), p_gates,
                    w_branch_gdn[layer].astype(BF16), w_branch_rwkv[layer].astype(BF16),
                    w_branch_sb[layer].astype(BF16), w_out[layer].astype(BF16))
        g_ffn = norm_ffn_g[layer].astype(F32)
        final_g = final_norm_g.astype(F32) if layer == depth - 1 else None
        i = layer // 2
        if layer % 2 == 0:
            x2 = _ffn(x2, g_ffn, ffn_w_gate[i].astype(BF16), ffn_w_up[i].astype(BF16),
                      ffn_w_down[i].astype(BF16), final_g=final_g)
        else:
            x2 = _moe(x2, g_ffn, router_w[i], moe_w_gate[i].astype(BF16), moe_w_up[i].astype(BF16),
                      moe_w_down[i].astype(BF16), final_g=final_g)
    return x2.reshape(b, t, d)
```

```python
import functools

import jax
import jax.numpy as jnp
from jax import lax
from jax.experimental import pallas as pl
from jax.experimental.pallas import tpu as pltpu

F32 = jnp.float32
BF16 = jnp.bfloat16

RMS_EPS = 1e-6
GN_EPS = 64e-5
L2_EPS = 1e-6

D_MODEL = 1024
GDN_HEADS = 4
GDN_HEAD_DIM = 128
GDN_WIDTH = 512
RWKV_WIDTH = 512
RWKV_HEAD_DIM = 64
RWKV_IN = 1792
SB_WIDTH = 512
SB_HEAD_DIM = 64
N_EXPERTS = 8
CHUNK = 64
LOG2E = 1.4426950408889634
SB_LOG2_CUTOFF = -160.0
SB_ROW_TILE = 128
SB_SKEW_CUMSUM = 3
SB_SKEW_VALUES = 6
GDN_CHUNKS_PER_STEP = 8
RWKV_CHUNKS_PER_STEP = 8
MOE_ROW_TILE = 512
VMEM_LIMIT_BYTES = 56 * 1024 * 1024


def _dot(a, b):
    return jnp.dot(a.astype(BF16), b.astype(BF16), preferred_element_type=F32)


def _dot_nt(a, b):
    return lax.dot_general(a.astype(BF16), b.astype(BF16), (((1,), (1,)), ((), ())),
                           preferred_element_type=F32)


def _split2(a):
    hi = a.astype(BF16)
    lo = (a - hi.astype(F32)).astype(BF16)
    return hi, lo


def _split3(a):
    hi = a.astype(BF16)
    r = a - hi.astype(F32)
    mid = r.astype(BF16)
    lo = (r - mid.astype(F32)).astype(BF16)
    return hi, mid, lo


def _dot3(a, b):
    ah, al = _split2(a)
    bh, bl = _split2(b)
    return (jnp.dot(ah, bh, preferred_element_type=F32)
            + jnp.dot(ah, bl, preferred_element_type=F32)
            + jnp.dot(al, bh, preferred_element_type=F32))


def _dot_exact_lhs(m, x, parts=3):
    xs = _split3(x) if parts == 3 else _split2(x)
    out = jnp.dot(m, xs[0], preferred_element_type=F32)
    for p in xs[1:]:
        out = out + jnp.dot(m, p, preferred_element_type=F32)
    return out


def _dot_exact_rhs(x, m, parts=2):
    xs = _split3(x) if parts == 3 else _split2(x)
    out = jnp.dot(xs[0], m, preferred_element_type=F32)
    for p in xs[1:]:
        out = out + jnp.dot(p, m, preferred_element_type=F32)
    return out


def _iota2(shape, dim):
    return lax.broadcasted_iota(jnp.int32, shape, dim)


def _eye(n, dtype=F32):
    return (_iota2((n, n), 0) == _iota2((n, n), 1)).astype(dtype)


def _softplus(x):
    return jnp.maximum(x, 0.0) + jnp.log(1.0 + jnp.exp(-jnp.abs(x)))


def _sigmoid(x):
    return 1.0 / (1.0 + jnp.exp(-x))


def _silu(x):
    return x * _sigmoid(x)


def _rms(x, g):
    return x * lax.rsqrt(jnp.mean(x * x, axis=-1, keepdims=True) + RMS_EPS) * g


def _nilpotent_inverse_many(ns, eye, dot):
    ts = [eye + n for n in ns]
    xs = [dot(n, n) for n in ns]
    for _ in range(4):
        xt = [dot(jnp.concatenate([x, t], axis=0), x) for x, t in zip(xs, ts)]
        xs = [y[0:CHUNK] for y in xt]
        ts = [t + y[CHUNK:2 * CHUNK] for t, y in zip(ts, xt)]
    return [t + dot(t, x) for t, x in zip(ts, xs)]


def _chunk_cumsum(x):
    i = _iota2((128, 128), 0)
    j = _iota2((128, 128), 1)
    m = ((j <= i) & ((i >> 6) == (j >> 6))).astype(BF16)
    return jnp.concatenate(
        [_dot_exact_lhs(m, x[r:r + 128]) for r in range(0, x.shape[0], 128)], axis=0)


def _head_sum(x):
    i = _iota2((128, 128), 0)
    j = _iota2((128, 128), 1)
    m = ((i >> 6) == (j >> 6)).astype(BF16)
    return jnp.concatenate(
        [_dot_exact_rhs(x[:, c:c + 128], m) for c in range(0, x.shape[1], 128)], axis=1)


def _normproj_kernel(x_ref, g_ref, *refs, n_out):
    w_refs = refs[:n_out]
    o_refs = refs[n_out:]
    hn = _rms(x_ref[...], g_ref[...]).astype(BF16)
    for w_ref, o_ref in zip(w_refs, o_refs):
        o_ref[...] = jnp.dot(hn, w_ref[...], preferred_element_type=F32).astype(o_ref.dtype)


def _normproj(x2d, g, weights, out_dtypes, tm=512):
    n, d = x2d.shape
    n_out = len(weights)
    in_specs = [pl.BlockSpec((tm, d), lambda i: (i, 0)), pl.BlockSpec((1, d), lambda i: (0, 0))]
    in_specs += [pl.BlockSpec(w.shape, lambda i: (0, 0), pipeline_mode=pl.Buffered(1)) for w in weights]
    out_specs = [pl.BlockSpec((tm, w.shape[1]), lambda i: (i, 0)) for w in weights]
    out_shape = [jax.ShapeDtypeStruct((n, w.shape[1]), dt) for w, dt in zip(weights, out_dtypes)]
    return pl.pallas_call(
        functools.partial(_normproj_kernel, n_out=n_out),
        grid=(n // tm,),
        in_specs=in_specs,
        out_specs=out_specs,
        out_shape=out_shape,
        compiler_params=pltpu.CompilerParams(
            dimension_semantics=("arbitrary",), vmem_limit_bytes=VMEM_LIMIT_BYTES),
        name="normproj",
    )(x2d, g.reshape(1, d), *weights)


def _gdn_kernel(qkv_ref, z_ref, ba_ref, cw_ref, alog_ref, dtb_ref, ng_ref, o_ref,
                ext_s, state_s, q_s, k_s, v_s, beta_s, gc_s, p_s, qq_s, r_s, zz_s, oc_s, *, tt):
    nc = tt // CHUNK
    w3 = 3 * GDN_WIDTH
    t = pl.program_id(1)

    @pl.when(t == 0)
    def _():
        ext_s[0:8, :] = jnp.zeros((8, w3), F32)
        state_s[...] = jnp.zeros_like(state_s)

    raw = qkv_ref[0]
    ext_s[8:8 + tt, :] = raw
    cw = cw_ref[...]
    y = raw * cw[3:4, :]
    for i in range(3):
        y = y + ext_s[5 + i:5 + i + tt, :] * cw[i:i + 1, :]
    ext_s[0:8, :] = raw[tt - 8:tt, :]
    y = _silu(y)

    for h in range(GDN_HEADS):
        sl = slice(128 * h, 128 * h + 128)
        qh = y[:, 128 * h:128 * h + 128]
        kh = y[:, GDN_WIDTH + 128 * h:GDN_WIDTH + 128 * h + 128]
        vh = y[:, 2 * GDN_WIDTH + 128 * h:2 * GDN_WIDTH + 128 * h + 128]
        qh = qh * lax.rsqrt(jnp.sum(qh * qh, axis=-1, keepdims=True) + L2_EPS) * (GDN_HEAD_DIM ** -0.5)
        kh = kh * lax.rsqrt(jnp.sum(kh * kh, axis=-1, keepdims=True) + L2_EPS)
        q_s[:, sl] = qh
        k_s[:, sl] = kh
        v_s[:, sl] = vh

    ba = ba_ref[0]
    gate = jnp.where(_iota2(ba.shape, 1) < GDN_HEADS, _sigmoid(ba),
                     -jnp.exp(alog_ref[...]) * _softplus(ba + dtb_ref[...]))
    ec = _iota2((128, 2 * GDN_WIDTH), 0)
    el = _iota2((128, 2 * GDN_WIDTH), 1)
    gate = _dot_exact_rhs(gate, (ec == (el >> 7)).astype(BF16), parts=3)
    beta_s[...] = gate[:, 0:GDN_WIDTH]
    gc_s[...] = _chunk_cumsum(gate[:, GDN_WIDTH:2 * GDN_WIDTH])

    eye64 = _eye(CHUNK)
    eye128 = _eye(128)
    ii = _iota2((CHUNK, CHUNK), 0)
    jj = _iota2((CHUNK, CHUNK), 1)

    def chunk_body(ci, carry):
        probs = [(ci * GDN_CHUNKS_PER_STEP + u, h) for u in range(GDN_CHUNKS_PER_STEP)
                 for h in range(GDN_HEADS)]
        rows = [pl.ds(pl.multiple_of(c * CHUNK, CHUNK), CHUNK) for c, _ in probs]
        lanes = [slice(128 * h, 128 * h + 128) for _, h in probs]
        idx = range(len(probs))
        q = [q_s[rows[i], lanes[i]] for i in idx]
        k = [k_s[rows[i], lanes[i]] for i in idx]
        gcc = [gc_s[rows[i], lanes[i]] for i in idx]
        gl = [g[CHUNK - 1:CHUNK, :] for g in gcc]
        gcr = [g.T[0:CHUNK, :] for g in gcc]
        dec_incl = [jnp.exp(jnp.where(ii >= jj, gcc[i][:, 0:CHUNK] - gcr[i], -jnp.inf)) for i in idx]
        kb = [k[i] * beta_s[rows[i], lanes[i]] for i in idx]
        a_mat = [_dot_nt(kb[i], k[i]) for i in idx]
        attn = [_dot_nt(q[i], k[i]) * dec_incl[i] for i in idx]
        kdt = [(k[i] * jnp.exp(gl[i] - gcc[i])).T for i in idx]
        tinv = _nilpotent_inverse_many(
            [-a_mat[i] * jnp.where(ii > jj, dec_incl[i], 0.0) for i in idx], eye64, _dot)
        u = [_dot(tinv[i], v_s[rows[i], lanes[i]] * beta_s[rows[i], lanes[i]]) for i in idx]
        w = [_dot(tinv[i], kb[i] * jnp.exp(gcc[i])) for i in idx]
        for i, (c, h) in enumerate(probs):
            p_s[c, h] = jnp.exp(gl[i]) * eye128 - _dot(kdt[i], w[i])
        for i, (c, h) in enumerate(probs):
            qq_s[c, h] = _dot(kdt[i], u[i])
        for i, (c, h) in enumerate(probs):
            r_s[c, h] = q[i] * jnp.exp(gcc[i]) - _dot(attn[i], w[i])
        for i, (c, h) in enumerate(probs):
            zz_s[c, h] = _dot(attn[i], u[i])
        return carry

    lax.fori_loop(0, nc // GDN_CHUNKS_PER_STEP, chunk_body, 0)

    def scan_body(c, carry):
        rows = pl.ds(pl.multiple_of(c * CHUNK, CHUNK), CHUNK)
        s = [state_s[h] for h in range(GDN_HEADS)]
        s_new = [_dot3(p_s[c, h], s[h]) for h in range(GDN_HEADS)]
        o = [_dot(r_s[c, h], s[h]) for h in range(GDN_HEADS)]
        for h in range(GDN_HEADS):
            state_s[h] = s_new[h] + qq_s[c, h]
            oc_s[rows, 128 * h:128 * h + 128] = o[h] + zz_s[c, h]
        return carry

    lax.fori_loop(0, nc, scan_body, 0)

    o = oc_s[...]
    z = z_ref[0]
    ng = ng_ref[...]
    for h in range(GDN_HEADS):
        sl = slice(128 * h, 128 * h + 128)
        oh = o[:, sl]
        oh = oh * lax.rsqrt(jnp.mean(oh * oh, axis=-1, keepdims=True) + RMS_EPS) * ng[:, sl]
        o_ref[0, :, sl] = (oh * _silu(z[:, sl])).astype(o_ref.dtype)


def _gdn(proj_a, proj_ba, conv_w, a_log, dt_bias, norm_g, tt=512):
    b, t, _ = proj_a.shape
    nc = tt // CHUNK
    w3 = 3 * GDN_WIDTH
    rep = lambda p: jnp.pad(p.astype(F32), (GDN_HEADS, 128 - 2 * GDN_HEADS)).reshape(1, 128)
    small = lambda shape: pl.BlockSpec(shape, lambda i, j: (0, 0))
    return pl.pallas_call(
        functools.partial(_gdn_kernel, tt=tt),
        grid=(b, t // tt),
        in_specs=[
            pl.BlockSpec((1, tt, w3), lambda i, j: (i, j, 0)),
            pl.BlockSpec((1, tt, GDN_WIDTH), lambda i, j: (i, j, 3)),
            pl.BlockSpec((1, tt, 128), lambda i, j: (i, j, 0)),
            small((4, w3)), small((1, 128)), small((1, 128)), small((1, GDN_WIDTH)),
        ],
        out_specs=pl.BlockSpec((1, tt, GDN_WIDTH), lambda i, j: (i, j, 0)),
        out_shape=jax.ShapeDtypeStruct((b, t, GDN_WIDTH), BF16),
        scratch_shapes=[
            pltpu.VMEM((tt + 8, w3), F32),
            pltpu.VMEM((GDN_HEADS, 128, 128), F32),
            pltpu.VMEM((tt, GDN_WIDTH), F32), pltpu.VMEM((tt, GDN_WIDTH), F32),
            pltpu.VMEM((tt, GDN_WIDTH), F32), pltpu.VMEM((tt, GDN_WIDTH), F32),
            pltpu.VMEM((tt, GDN_WIDTH), F32),
            pltpu.VMEM((nc, GDN_HEADS, 128, 128), F32), pltpu.VMEM((nc, GDN_HEADS, 128, 128), F32),
            pltpu.VMEM((nc, GDN_HEADS, CHUNK, 128), F32), pltpu.VMEM((nc, GDN_HEADS, CHUNK, 128), F32),
            pltpu.VMEM((tt, GDN_WIDTH), F32),
        ],
        compiler_params=pltpu.CompilerParams(
            dimension_semantics=("arbitrary", "arbitrary"), vmem_limit_bytes=VMEM_LIMIT_BYTES),
        name="gdn",
    )(proj_a, proj_a, proj_ba, conv_w.astype(F32), rep(a_log), rep(dt_bias),
      jnp.tile(norm_g.astype(F32), GDN_HEADS).reshape(1, GDN_WIDTH))


def _rwkv_kernel(h_ref, mu_ref, w0_ref, a0_ref, kk_ref, ka_ref, rk_ref, lng_ref, lnb_ref,
                 wa_ref, g2_ref, o_ref,
                 ext_s, state_s, r_s, kn_s, k2_s, v_s, a_s, lw_s, lc_s, p_s, qq_s, rh_s, yc_s, y_s,
                 *, tt):
    nc = tt // CHUNK
    npair = RWKV_WIDTH // 128
    t = pl.program_id(1)

    @pl.when(t == 0)
    def _():
        ext_s[0:8, :] = jnp.zeros((8, RWKV_IN), F32)
        state_s[...] = jnp.zeros_like(state_s)

    raw = h_ref[0]
    ext_s[8:8 + tt, :] = raw
    prev = ext_s[7:7 + tt, :]
    ext_s[0:8, :] = raw[tt - 8:tt, :]
    hl = raw + (prev - raw) * mu_ref[...]
    r = hl[:, 0:512]
    k = hl[:, 512:1024]
    v = hl[:, 1024:1536]
    xwa = hl[:, 1536:1664]
    xg = hl[:, 1664:1792]
    lane128 = _iota2((1, 128), 1)
    xwa = jnp.where(lane128 < 64, jnp.tanh(xwa), xwa)
    lora = _dot3(xwa, wa_ref[...])
    w_log = -_softplus(-(w0_ref[...] + lora[:, 0:512])) - 0.5
    lw = -jnp.exp(w_log)
    a = _sigmoid(a0_ref[...] + lora[:, 512:1024])
    gate = _dot3(_sigmoid(xg), g2_ref[...])
    kk = k * kk_ref[...]
    k2 = k * (1.0 + (a - 1.0) * ka_ref[...])
    kn = kk * lax.rsqrt(_head_sum(kk * kk) + L2_EPS)
    r_s[...] = r
    kn_s[...] = kn
    k2_s[...] = k2
    v_s[...] = v
    a_s[...] = a
    lw_s[...] = lw
    lc_s[...] = _chunk_cumsum(lw)

    eye128 = _eye(128)
    row = _iota2((CHUNK, 128), 0)
    lane = _iota2((CHUNK, 128), 1)
    col = lane & (CHUNK - 1)
    strict = row > col
    incl = row >= col
    eye2 = (row == col).astype(F32)
    head0 = lane < CHUNK
    head0_wide = (_iota2((CHUNK, 256), 1) & 127) < CHUNK
    bi = _iota2((128, 128), 0)
    bj = _iota2((128, 128), 1)
    blockdiag = (bi < 64) == (bj < 64)

    def per_head_rows(x, mask=head0):
        return jnp.concatenate([jnp.where(mask, x, 0.0), jnp.where(mask, 0.0, x)], axis=0)

    def chunk_body(ci, carry):
        units = [(ci * RWKV_CHUNKS_PER_STEP + u, p) for u in range(RWKV_CHUNKS_PER_STEP) for p in range(npair)]
        nu = range(len(units))
        rows = [pl.ds(pl.multiple_of(c * CHUNK, CHUNK), CHUNK) for c, _ in units]
        sls = [slice(128 * p, 128 * p + 128) for _, p in units]
        lcc = [lc_s[rows[u], sls[u]] for u in nu]
        vc = [v_s[rows[u], sls[u]] for u in nu]
        lcl = [x[CHUNK - 1:CHUNK, :] for x in lcc]
        ginv = [jnp.exp(-x) for x in lcc]
        gend = [jnp.exp(lcl[u] - lcc[u]) for u in nu]
        kna = [kn_s[rows[u], sls[u]] * a_s[rows[u], sls[u]] for u in nu]
        at = [-kn_s[rows[u], sls[u]] * jnp.exp(lcc[u] - lw_s[rows[u], sls[u]]) for u in nu]
        rt = [r_s[rows[u], sls[u]] * jnp.exp(lcc[u]) for u in nu]
        ar = [jnp.concatenate([at[u], rt[u]], axis=0) for u in nu]
        xb = [_dot_nt(ar[u], per_head_rows(kna[u] * ginv[u])) for u in nu]
        xk = [_dot_nt(ar[u], per_head_rows(k2_s[rows[u], sls[u]] * ginv[u])) for u in nu]
        a_ab = [jnp.where(strict, x[0:CHUNK], 0.0) for x in xb]
        a_rb = [jnp.where(incl, x[CHUNK:2 * CHUNK], 0.0) for x in xb]
        a_ak = [jnp.where(strict, x[0:CHUNK], 0.0) for x in xk]
        a_rk = [jnp.where(incl, x[CHUNK:2 * CHUNK], 0.0) for x in xk]
        vrows = [per_head_rows(vc[u]) for u in nu]
        akv = [_dot(a_ak[u], vrows[u]) for u in nu]
        arkv = [_dot(a_rk[u], vrows[u]) for u in nu]
        bbt = [(kna[u] * gend[u]).T for u in nu]
        kbt = [(k2_s[rows[u], sls[u]] * gend[u]).T for u in nu]
        ts = [eye2 + a for a in a_ab]
        xs = [_dot(x, per_head_rows(x)) for x in a_ab]
        for step in range(5):
            if step < 4:
                xt = [_dot(jnp.concatenate([x, t], axis=0), per_head_rows(x)) for x, t in zip(xs, ts)]
                xs = [y[0:CHUNK] for y in xt]
                ts = [t + y[CHUNK:2 * CHUNK] for t, y in zip(ts, xt)]
            else:
                ts = [t + _dot(t, per_head_rows(x)) for t, x in zip(ts, xs)]
        au = [_dot(ts[u], per_head_rows(jnp.concatenate([at[u], akv[u]], axis=1), head0_wide)) for u in nu]
        ry = [_dot(a_rb[u], per_head_rows(au[u], head0_wide)) for u in nu]
        kv = [_dot(kbt[u], vc[u]) for u in nu]
        pq = [_dot(bbt[u], au[u]) for u in nu]
        for u, (c, p) in enumerate(units):
            rh_s[c, p] = rt[u] + ry[u][:, 0:128]
            yc_s[c, p] = ry[u][:, 128:256] + arkv[u]
            p_s[c, p] = jnp.where(blockdiag, pq[u][:, 0:128], 0.0) + eye128 * jnp.exp(lcl[u])
            qq_s[c, p] = jnp.where(blockdiag, pq[u][:, 128:256] + kv[u], 0.0)
        return carry

    lax.fori_loop(0, nc // RWKV_CHUNKS_PER_STEP, chunk_body, 0)

    def scan_body(c, carry):
        rows = pl.ds(pl.multiple_of(c * CHUNK, CHUNK), CHUNK)
        s = [state_s[p] for p in range(npair)]
        s_new = [_dot3(p_s[c, p], s[p]) for p in range(npair)]
        y = [_dot(rh_s[c, p], s[p]) for p in range(npair)]
        for p in range(npair):
            state_s[p] = s_new[p] + qq_s[c, p]
            y_s[rows, 128 * p:128 * p + 128] = y[p] + yc_s[c, p]
        return carry

    lax.fori_loop(0, nc, scan_body, 0)

    y = y_s[...]
    mean = _head_sum(y) * (1.0 / RWKV_HEAD_DIM)
    yc = y - mean
    var = _head_sum(yc * yc) * (1.0 / RWKV_HEAD_DIM)
    yn = yc * lax.rsqrt(var + GN_EPS) * lng_ref[...] + lnb_ref[...]
    bonus = _head_sum(r * k2 * rk_ref[...]) * v
    o_ref[0] = ((yn + bonus) * gate).astype(o_ref.dtype)


def _rwkv(h, mu, w0, w2, a0, a2, g2, k_k, k_a, r_k, ln_g, ln_b, tt=512):
    b, t, _ = h.shape
    nc = tt // CHUNK
    npair = RWKV_WIDTH // 128
    row = lambda p: p.astype(F32).reshape(1, -1)
    wa = jnp.zeros((128, 2 * RWKV_WIDTH), F32)
    wa = wa.at[0:64, 0:RWKV_WIDTH].set(w2.astype(F32)).at[64:128, RWKV_WIDTH:].set(a2.astype(F32))
    small = lambda shape: pl.BlockSpec(shape, lambda i, j: (0, 0))
    vec = small((1, RWKV_WIDTH))
    return pl.pallas_call(
        functools.partial(_rwkv_kernel, tt=tt),
        grid=(b, t // tt),
        in_specs=[pl.BlockSpec((1, tt, RWKV_IN), lambda i, j: (i, j, 0)), small((1, RWKV_IN)),
                  vec, vec, vec, vec, vec, vec, vec,
                  small((128, 2 * RWKV_WIDTH)), small((128, RWKV_WIDTH))],
        out_specs=pl.BlockSpec((1, tt, RWKV_WIDTH), lambda i, j: (i, j, 0)),
        out_shape=jax.ShapeDtypeStruct((b, t, RWKV_WIDTH), BF16),
        scratch_shapes=[
            pltpu.VMEM((tt + 8, RWKV_IN), F32),
            pltpu.VMEM((npair, 128, 128), F32),
        ] + [pltpu.VMEM((tt, RWKV_WIDTH), F32)] * 7 + [
            pltpu.VMEM((nc, npair, 128, 128), F32), pltpu.VMEM((nc, npair, 128, 128), F32),
            pltpu.VMEM((nc, npair, CHUNK, 128), F32), pltpu.VMEM((nc, npair, CHUNK, 128), F32),
            pltpu.VMEM((tt, RWKV_WIDTH), F32),
        ],
        compiler_params=pltpu.CompilerParams(
            dimension_semantics=("arbitrary", "arbitrary"), vmem_limit_bytes=VMEM_LIMIT_BYTES),
        name="rwkv7",
    )(h, row(mu), row(w0), row(a0), row(k_k), row(k_a), row(r_k), row(ln_g), row(ln_b),
      wa, g2.astype(F32))


def _sb_kernel(q_ref, k_ref, v_ref, o_ref, acc_s, aux_s, *, bq, bk):
    qi = pl.program_id(2)
    nsub = bq // bk
    q = q_ref[0] * (SB_HEAD_DIM ** -0.5 * LOG2E)
    lane = _iota2((1, 128), 1)
    qm = [jnp.where(lane < 64, q, 0.0).astype(BF16), jnp.where(lane >= 64, q, 0.0).astype(BF16)]
    ti = _iota2((2 * bk, bk), 0)
    tj = _iota2((2 * bk, bk), 1)
    cum_mat = ((ti & (bk - 1)) > tj).astype(BF16)
    acc_s[...] = jnp.zeros_like(acc_s)
    aux_s[...] = jnp.zeros_like(aux_s)

    def load_kv(kb):
        start = pl.multiple_of(kb * bk, bk)
        return (start, k_ref[0, pl.ds(start, bk), :].astype(BF16), v_ref[0, pl.ds(start, bk), :].astype(BF16))

    def stage_scores(item):
        (start, k, v), ra, nr, j, masked = item
        z = lax.dot_general(qm[j][ra:ra + nr], k, (((1,), (1,)), ((), ())), preferred_element_type=F32)
        return z

    def stage_cumsum(item, z):
        (start, k, v), ra, nr, j, masked = item
        rows = slice(ra, ra + nr)
        neg_abs = lax.bitcast_convert_type(
            lax.bitcast_convert_type(z, jnp.uint32) | jnp.uint32(0x80000000), F32)
        lsig = jnp.minimum(z, 0.0) - jnp.log(1.0 + jnp.exp2(neg_abs)) * LOG2E
        l1 = lsig - z
        msk = None
        if masked:
            msk = (start + _iota2((nr, bk), 1)) < (qi * bq + ra + _iota2((nr, bk), 0))
            l1 = jnp.where(msk, l1, 0.0)
        l1_hi = l1.astype(BF16)
        l1_lo = (l1 - l1_hi.astype(F32)).astype(BF16)
        cr = (jnp.dot(jnp.concatenate([l1_hi, l1_lo], axis=1), cum_mat, preferred_element_type=F32)
              + aux_s[j, rows, 0:1])
        aux_s[j, rows, :] = cr + l1
        return lsig, cr, msk

    def stage_values(item, state):
        (start, k, v), ra, nr, j, masked = item
        lsig, cr, msk = state
        att = jnp.exp2(lsig + cr)
        if masked:
            att = jnp.where(msk, att, 0.0)
        acc_s[j, ra:ra + nr, :] += jnp.dot(att.astype(BF16), v, preferred_element_type=F32)

    def run_items(items):
        n = len(items)
        zs = {}
        states = {}
        for s in range(n + SB_SKEW_VALUES):
            if s < n:
                zs[s] = stage_scores(items[s])
            if 0 <= s - SB_SKEW_CUMSUM < n:
                i = s - SB_SKEW_CUMSUM
                states[i] = stage_cumsum(items[i], zs.pop(i))
            if 0 <= s - SB_SKEW_VALUES < n:
                i = s - SB_SKEW_VALUES
                stage_values(items[i], states.pop(i))

    def max_carry(ra, nr):
        r = jnp.maximum(aux_s[0, ra:ra + nr, :], aux_s[1, ra:ra + nr, :])
        r = jnp.max(jnp.where(lane == 0, r, -jnp.inf), axis=0, keepdims=True)
        return jnp.max(r, axis=1, keepdims=True)[0, 0]

    items = []
    for d in range(nsub - 1, -1, -1):
        kv = load_kv(qi * nsub + d)
        ra = d * bk
        while ra < bq:
            nr = SB_ROW_TILE if (bq - ra) % (2 * SB_ROW_TILE) else 2 * SB_ROW_TILE
            on_diagonal = ra == d * bk
            if on_diagonal:
                nr = SB_ROW_TILE
            items += [(kv, ra, nr, 0, on_diagonal), (kv, ra, nr, 1, on_diagonal)]
            ra += nr
    run_items(items)

    def cond(c):
        kb, rmax = c
        return (kb >= 0) & (rmax > SB_LOG2_CUTOFF)

    for ra in range(0, bq, 2 * SB_ROW_TILE):
        nr = 2 * SB_ROW_TILE

        def body(c, ra=ra, nr=nr):
            kb, _ = c
            items = []
            for kv in (load_kv(kb), load_kv(kb - 1)):
                items += [(kv, ra, nr, 0, False), (kv, ra, nr, 1, False)]
            run_items(items)
            return kb - 2, max_carry(ra, nr)

        lax.while_loop(cond, body, (qi * nsub - 1, max_carry(ra, nr)))
    o_ref[0] = jnp.where(lane < 64, acc_s[0], acc_s[1]).astype(o_ref.dtype)


def _stick_breaking(h_sb, bq=512, bk=128):
    b, t, _ = h_sb.shape
    npair = SB_WIDTH // 128
    assert bk == SB_ROW_TILE and bq % (2 * SB_ROW_TILE) == 0 and t % bq == 0
    return pl.pallas_call(
        functools.partial(_sb_kernel, bq=bq, bk=bk),
        grid=(b, npair, t // bq),
        in_specs=[
            pl.BlockSpec((1, bq, 128), lambda i, p, j: (i, j, p)),
            pl.BlockSpec((1, t, 128), lambda i, p, j: (i, 0, npair + p)),
            pl.BlockSpec((1, t, 128), lambda i, p, j: (i, 0, 2 * npair + p)),
        ],
        out_specs=pl.BlockSpec((1, bq, 128), lambda i, p, j: (i, j, p)),
        out_shape=jax.ShapeDtypeStruct((b, t, SB_WIDTH), BF16),
        scratch_shapes=[pltpu.VMEM((2, bq, 128), F32), pltpu.VMEM((2, bq, bk), F32)],
        compiler_params=pltpu.CompilerParams(
            dimension_semantics=("arbitrary", "arbitrary", "arbitrary"),
            vmem_limit_bytes=VMEM_LIMIT_BYTES),
        name="stickbreak",
    )(h_sb, h_sb, h_sb)


def _merge_kernel(x_ref, oa_ref, ob_ref, oc_ref, gt_ref, wa_ref, wb_ref, wc_ref, wo_ref, o_ref):
    d = D_MODEL
    g = _sigmoid(gt_ref[...].astype(F32))
    m = (g[:, 0:d] * _dot(oa_ref[...], wa_ref[...])
         + g[:, d:2 * d] * _dot(ob_ref[...], wb_ref[...])
         + g[:, 2 * d:3 * d] * _dot(oc_ref[...], wc_ref[...]))
    o_ref[...] = x_ref[...] + _dot(m, wo_ref[...])


def _merge(x2d, oa, ob, oc, gates, wa, wb, wc, wo, tm=512):
    n, d = x2d.shape
    rowspec = lambda w: pl.BlockSpec((tm, w), lambda i: (i, 0))
    full = lambda w: pl.BlockSpec(w.shape, lambda i: (0, 0))
    return pl.pallas_call(
        _merge_kernel,
        grid=(n // tm,),
        in_specs=[rowspec(d), rowspec(512), rowspec(512), rowspec(512), rowspec(3 * d),
                  full(wa), full(wb), full(wc), full(wo)],
        out_specs=rowspec(d),
        out_shape=jax.ShapeDtypeStruct((n, d), F32),
        compiler_params=pltpu.CompilerParams(
            dimension_semantics=("arbitrary",), vmem_limit_bytes=VMEM_LIMIT_BYTES),
        name="merge",
    )(x2d, oa, ob, oc, gates, wa, wb, wc, wo)


def _router_kernel(x_ref, g_ref, rw_ref, comb_ref, idx_ref, cnt_ref, run_s):
    @pl.when(pl.program_id(0) == 0)
    def _():
        run_s[...] = jnp.zeros_like(run_s)

    hn = _rms(x_ref[...], g_ref[...])
    logits = _dot3(hn, rw_ref[...])
    tm = logits.shape[0]
    lane = _iota2(logits.shape, 1)
    m1 = jnp.max(logits, axis=-1, keepdims=True)
    i1 = jnp.min(jnp.where(logits == m1, lane, N_EXPERTS), axis=-1, keepdims=True)
    sel1 = lane == i1
    rest = jnp.where(sel1, -jnp.inf, logits)
    m2 = jnp.max(rest, axis=-1, keepdims=True)
    i2 = jnp.min(jnp.where(rest == m2, lane, N_EXPERTS), axis=-1, keepdims=True)
    sel2 = lane == i2
    e2 = jnp.exp(m2 - m1)
    w1 = 1.0 / (1.0 + e2)
    comb_ref[...] = jnp.where(sel1, w1, 0.0) + jnp.where(sel2, e2 * w1, 0.0)

    member = (sel1 | sel2).astype(BF16)
    before = (_iota2((tm, tm), 0) > _iota2((tm, tm), 1)).astype(BF16)
    rank = jnp.dot(before, member, preferred_element_type=F32) + run_s[...]
    r1 = jnp.sum(jnp.where(sel1, rank, 0.0), axis=-1, keepdims=True).astype(jnp.int32)
    r2 = jnp.sum(jnp.where(sel2, rank, 0.0), axis=-1, keepdims=True).astype(jnp.int32)
    run_s[...] += jnp.sum(member.astype(F32), axis=0, keepdims=True)
    cnt_ref[...] = run_s[...]
    idx_ref[...] = jnp.where(lane == 0, i1, jnp.where(lane == 1, i2, jnp.where(lane == 2, r1,
                             jnp.where(lane == 3, r2, 0))))


def _router(x2d, g, rw, tm=512):
    n, d = x2d.shape
    return pl.pallas_call(
        _router_kernel,
        grid=(n // tm,),
        in_specs=[pl.BlockSpec((tm, d), lambda i: (i, 0)), pl.BlockSpec((1, d), lambda i: (0, 0)),
                  pl.BlockSpec(rw.shape, lambda i: (0, 0))],
        out_specs=[pl.BlockSpec((tm, N_EXPERTS), lambda i: (i, 0)),
                   pl.BlockSpec((tm, N_EXPERTS), lambda i: (i, 0)),
                   pl.BlockSpec((1, N_EXPERTS), lambda i: (0, 0))],
        out_shape=[jax.ShapeDtypeStruct((n, N_EXPERTS), F32),
                   jax.ShapeDtypeStruct((n, N_EXPERTS), jnp.int32),
                   jax.ShapeDtypeStruct((1, N_EXPERTS), F32)],
        scratch_shapes=[pltpu.VMEM((1, N_EXPERTS), F32)],
        compiler_params=pltpu.CompilerParams(dimension_semantics=("arbitrary",)),
        name="router",
    )(x2d, g.reshape(1, d), rw.astype(F32))


def _row_copy(src_ref, src_row, dst_ref, dst_row, sem):
    return pltpu.make_async_copy(src_ref.at[pl.ds(src_row, 1), :], dst_ref.at[pl.ds(dst_row, 1), :], sem)


def _drain_rows(n_copies, src_ref, dst_ref, sem):
    def body(t, carry):
        _row_copy(src_ref, 0, dst_ref, 0, sem).wait()
        return carry

    lax.fori_loop(0, n_copies, body, 0, unroll=8)


def _dispatch_kernel(pos1_ref, pos2_ref, x_ref, g_ref, xs_in_ref, xs_ref, hn_s, sems):
    del xs_in_ref
    tm = hn_s.shape[1]
    i = pl.program_id(0)
    slot = i % 2
    hn_s[slot] = _rms(x_ref[...], g_ref[...])
    src = hn_s.at[slot]
    sem = sems.at[slot]

    def issue(t, carry):
        _row_copy(src, t, xs_ref, pos1_ref[t], sem).start()
        _row_copy(src, t, xs_ref, pos2_ref[t], sem).start()
        return carry

    lax.fori_loop(0, tm, issue, 0, unroll=8)

    @pl.when(i > 0)
    def _():
        _drain_rows(2 * tm, hn_s.at[1 - slot], xs_ref, sems.at[1 - slot])

    @pl.when(i == pl.num_programs(0) - 1)
    def _():
        _drain_rows(2 * tm, src, xs_ref, sem)


def _dispatch(x2d, g, pos1, pos2, n_rows, tm=512):
    n, d = x2d.shape
    smem = lambda: pl.BlockSpec((tm,), lambda i: (i,), memory_space=pltpu.SMEM)
    return pl.pallas_call(
        _dispatch_kernel,
        grid=(n // tm,),
        in_specs=[smem(), smem(), pl.BlockSpec((tm, d), lambda i: (i, 0)),
                  pl.BlockSpec((1, d), lambda i: (0, 0)), pl.BlockSpec(memory_space=pl.ANY)],
        out_specs=pl.BlockSpec(memory_space=pl.ANY),
        out_shape=jax.ShapeDtypeStruct((n_rows, d), F32),
        scratch_shapes=[pltpu.VMEM((2, tm, d), F32), pltpu.SemaphoreType.DMA((2,))],
        input_output_aliases={4: 0},
        compiler_params=pltpu.CompilerParams(dimension_semantics=("arbitrary",)),
        name="moe_dispatch",
    )(pos1, pos2, x2d, g.reshape(1, d), jnp.zeros((n_rows, d), F32))


def _grouped_ffn_kernel(te_ref, nt_ref, xs_ref, wg_ref, wu_ref, wd_ref, o_ref, xb_s, acc_s):
    i = pl.program_id(0)
    f = pl.program_id(1)

    @pl.when(f == 0)
    def _():
        xb_s[...] = xs_ref[...].astype(BF16)
        acc_s[...] = jnp.zeros_like(acc_s)

    @pl.when(i < nt_ref[0])
    def _():
        xb = xb_s[...]
        act = _silu(jnp.dot(xb, wg_ref[...], preferred_element_type=F32)) * jnp.dot(
            xb, wu_ref[...], preferred_element_type=F32)
        acc_s[...] += jnp.dot(act.astype(BF16), wd_ref[...], preferred_element_type=F32)

    @pl.when(f == pl.num_programs(1) - 1)
    def _():
        o_ref[...] = acc_s[...]


def _grouped_ffn(xs, tile_expert, n_tiles, wg, wu, wd, tm, tf=1792):
    n_rows, d = xs.shape
    dff = wg.shape[2]
    nf = dff // tf
    fsel = lambda i, f, te, nt: jnp.where(i < nt[0], f, nf - 1)
    return pl.pallas_call(
        _grouped_ffn_kernel,
        grid_spec=pltpu.PrefetchScalarGridSpec(
            num_scalar_prefetch=2,
            grid=(n_rows // tm, nf),
            in_specs=[pl.BlockSpec((tm, d), lambda i, f, te, nt: (i, 0)),
                      pl.BlockSpec((None, d, tf), lambda i, f, te, nt: (te[i], 0, fsel(i, f, te, nt))),
                      pl.BlockSpec((None, d, tf), lambda i, f, te, nt: (te[i], 0, fsel(i, f, te, nt))),
                      pl.BlockSpec((None, tf, d), lambda i, f, te, nt: (te[i], fsel(i, f, te, nt), 0))],
            out_specs=pl.BlockSpec((tm, d), lambda i, f, te, nt: (i, 0)),
            scratch_shapes=[pltpu.VMEM((tm, d), BF16), pltpu.VMEM((tm, d), F32)]),
        out_shape=jax.ShapeDtypeStruct((n_rows, d), F32),
        compiler_params=pltpu.CompilerParams(
            dimension_semantics=("arbitrary", "arbitrary"), vmem_limit_bytes=VMEM_LIMIT_BYTES),
        name="moe_grouped_ffn",
    )(tile_expert, n_tiles, xs, wg, wu, wd)


def _combine_kernel(*refs, final):
    if final:
        pos1_ref, pos2_ref, npos1_ref, npos2_ref, x_ref, comb_ref, fg_ref, ys_ref, o_ref, y1_s, y2_s, sems = refs
    else:
        pos1_ref, pos2_ref, npos1_ref, npos2_ref, x_ref, comb_ref, ys_ref, o_ref, y1_s, y2_s, sems = refs
    tm = y1_s.shape[1]
    i = pl.program_id(0)
    slot = i % 2

    def gather(p1_ref, p2_ref, sl):
        def issue(t, carry):
            _row_copy(ys_ref, p1_ref[t], y1_s.at[sl], t, sems.at[sl]).start()
            _row_copy(ys_ref, p2_ref[t], y2_s.at[sl], t, sems.at[sl]).start()
            return carry

        lax.fori_loop(0, tm, issue, 0, unroll=8)

    @pl.when(i == 0)
    def _():
        gather(pos1_ref, pos2_ref, 0)

    @pl.when(i < pl.num_programs(0) - 1)
    def _():
        gather(npos1_ref, npos2_ref, 1 - slot)

    _drain_rows(2 * tm, ys_ref, y1_s.at[slot], sems.at[slot])
    comb = comb_ref[...]
    w1 = jnp.max(comb, axis=-1, keepdims=True)
    w2 = jnp.sum(comb, axis=-1, keepdims=True) - w1
    out = x_ref[...] + w1 * y1_s[slot] + w2 * y2_s[slot]
    if final:
        out = _rms(out, fg_ref[...])
    o_ref[...] = out


def _combine(x2d, comb, pos1, pos2, ys, final_g=None, tm=512):
    n, d = x2d.shape
    final = final_g is not None
    nsteps = n // tm
    smem = lambda: pl.BlockSpec((tm,), lambda i: (i,), memory_space=pltpu.SMEM)
    smem_next = lambda: pl.BlockSpec((tm,), lambda i: (jnp.minimum(i + 1, nsteps - 1),),
                                     memory_space=pltpu.SMEM)
    in_specs = [smem(), smem(), smem_next(), smem_next(), pl.BlockSpec((tm, d), lambda i: (i, 0)),
                pl.BlockSpec((tm, N_EXPERTS), lambda i: (i, 0))]
    args = [pos1, pos2, pos1, pos2, x2d, comb]
    if final:
        in_specs.append(pl.BlockSpec((1, d), lambda i: (0, 0)))
        args.append(final_g.reshape(1, d))
    in_specs.append(pl.BlockSpec(memory_space=pl.ANY))
    args.append(ys)
    return pl.pallas_call(
        functools.partial(_combine_kernel, final=final),
        grid=(nsteps,),
        in_specs=in_specs,
        out_specs=pl.BlockSpec((tm, d), lambda i: (i, 0)),
        out_shape=jax.ShapeDtypeStruct((n, d), F32),
        scratch_shapes=[pltpu.VMEM((2, tm, d), F32), pltpu.VMEM((2, tm, d), F32),
                        pltpu.SemaphoreType.DMA((2,))],
        compiler_params=pltpu.CompilerParams(dimension_semantics=("arbitrary",)),
        name="moe_combine",
    )(*args)


def _moe(x2d, g, rw, wg, wu, wd, final_g=None, tm=MOE_ROW_TILE):
    n, d = x2d.shape
    comb, idx, cnt = _router(x2d, g, rw)
    counts = cnt[0].astype(jnp.int32)
    padded = ((counts + tm - 1) // tm) * tm
    ends = jnp.cumsum(padded)
    starts = ends - padded
    pos1 = starts[idx[:, 0]] + idx[:, 2]
    pos2 = starts[idx[:, 1]] + idx[:, 3]
    n_rows = 2 * n + N_EXPERTS * tm
    n_tiles = (ends[-1] // tm).reshape(1)
    tile_ids = jnp.arange(n_rows // tm, dtype=jnp.int32)
    tile_expert = jnp.minimum(
        jnp.sum((tile_ids[:, None] >= (ends // tm)[None, :]).astype(jnp.int32), axis=1), N_EXPERTS - 1)
    xs = _dispatch(x2d, g, pos1, pos2, n_rows)
    ys = _grouped_ffn(xs, tile_expert, n_tiles, wg, wu, wd, tm)
    return _combine(x2d, comb, pos1, pos2, ys, final_g=final_g)


def _ffn_kernel(*refs, final):
    if final:
        x_ref, g_ref, wg_ref, wu_ref, wd_ref, fg_ref, o_ref = refs
    else:
        x_ref, g_ref, wg_ref, wu_ref, wd_ref, o_ref = refs
    x = x_ref[...]
    hn = _rms(x, g_ref[...]).astype(BF16)
    act = _silu(jnp.dot(hn, wg_ref[...], preferred_element_type=F32)) * jnp.dot(
        hn, wu_ref[...], preferred_element_type=F32)
    out = x + jnp.dot(act.astype(BF16), wd_ref[...], preferred_element_type=F32)
    if final:
        out = _rms(out, fg_ref[...])
    o_ref[...] = out


def _ffn(x2d, g, wg, wu, wd, final_g=None, tm=512):
    n, d = x2d.shape
    final = final_g is not None
    xmap = lambda i: (i, 0)
    cmap = lambda i: (0, 0)
    resident = lambda w: pl.BlockSpec(w.shape, cmap, pipeline_mode=pl.Buffered(1))
    in_specs = [pl.BlockSpec((tm, d), xmap), pl.BlockSpec((1, d), cmap), resident(wg), resident(wu), resident(wd)]
    args = [x2d, g.reshape(1, d), wg, wu, wd]
    if final:
        in_specs.append(pl.BlockSpec((1, d), cmap))
        args.append(final_g.reshape(1, d))
    return pl.pallas_call(
        functools.partial(_ffn_kernel, final=final),
        grid=(n // tm,),
        in_specs=in_specs,
        out_specs=pl.BlockSpec((tm, d), xmap),
        out_shape=jax.ShapeDtypeStruct((n, d), F32),
        compiler_params=pltpu.CompilerParams(
            dimension_semantics=("arbitrary",), vmem_limit_bytes=VMEM_LIMIT_BYTES),
        name="dense_ffn",
    )(*args)


def kernel(x, norm_mix_g, w_in, gdn_conv_w, gdn_a_log, gdn_dt_bias, gdn_norm_g, rwkv_mu, rwkv_w0, rwkv_w2, rwkv_a0, rwkv_a2, rwkv_g2, rwkv_k_k, rwkv_k_a, rwkv_r_k, rwkv_ln_g, rwkv_ln_b, w_branch_gdn, w_branch_rwkv, w_branch_sb, w_out, norm_ffn_g, ffn_w_gate, ffn_w_up, ffn_w_down, router_w, moe_w_gate, moe_w_up, moe_w_down, final_norm_g):
    b, t, d = x.shape
    n = b * t
    depth = w_in.shape[0]
    x2 = x.reshape(n, d).astype(F32)
    for layer in range(depth):
        w = w_in[layer]
        w_gdn = w[:, 0:2048].astype(BF16)
        w_ba = jnp.pad(w[:, 2048:2056], ((0, 0), (0, 120))).astype(BF16)
        w_rwkv = w[:, 2056:3848].astype(BF16)
        w_sb = w[:, 3848:5384].astype(BF16)
        w_gates = w[:, 5384:8456].astype(BF16)
        g_mix = norm_mix_g[layer].astype(F32)
        p_gdn, p_ba, p_rwkv = _normproj(x2, g_mix, [w_gdn, w_ba, w_rwkv], [F32, F32, F32])
        p_sb, p_gates = _normproj(x2, g_mix, [w_sb, w_gates], [F32, BF16])
        o_a = _gdn(p_gdn.reshape(b, t, -1), p_ba.reshape(b, t, -1), gdn_conv_w[layer], gdn_a_log[layer], gdn_dt_bias[layer],
                   gdn_norm_g[layer])
        o_b = _rwkv(p_rwkv.reshape(b, t, -1), rwkv_mu[layer], rwkv_w0[layer], rwkv_w2[layer],
                    rwkv_a0[layer], rwkv_a2[layer], rwkv_g2[layer], rwkv_k_k[layer], rwkv_k_a[layer],
                    rwkv_r_k[layer].reshape(-1), rwkv_ln_g[layer], rwkv_ln_b[layer])
        o_c = _stick_breaking(p_sb.reshape(b, t, -1))
        x2 = _merge(x2, o_a.reshape(n, -1), o_b.reshape(n, -1), o_c.reshape(n, -1), p_gates,
                    w_branch_gdn[layer].astype(BF16), w_branch_rwkv[layer].astype(BF16),
                    w_branch_sb[layer].astype(BF16), w_out[layer].astype(BF16))
        g_ffn = norm_ffn_g[layer].astype(F32)
        final_g = final_norm_g.astype(F32) if layer == depth - 1 else None
        i = layer // 2
        if layer % 2 == 0:
            x2 = _ffn(x2, g_ffn, ffn_w_gate[i].astype(BF16), ffn_w_up[i].astype(BF16),
                      ffn_w_down[i].astype(BF16), final_g=final_g)
        else:
            x2 = _moe(x2, g_ffn, router_w[i], moe_w_gate[i].astype(BF16), moe_w_up[i].astype(BF16),
                      moe_w_down[i].astype(BF16), final_g=final_g)
    return x2.reshape(b, t, d)
```

```python
import functools

import jax
import jax.numpy as jnp
from jax import lax
from jax.experimental import pallas as pl
from jax.experimental.pallas import tpu as pltpu

F32 = jnp.float32
BF16 = jnp.bfloat16

RMS_EPS = 1e-6
GN_EPS = 64e-5
L2_EPS = 1e-6

D_MODEL = 1024
GDN_HEADS = 4
GDN_HEAD_DIM = 128
GDN_WIDTH = 512
RWKV_WIDTH = 512
RWKV_HEAD_DIM = 64
RWKV_IN = 1792
SB_WIDTH = 512
SB_HEAD_DIM = 64
N_EXPERTS = 8
CHUNK = 64
LOG2E = 1.4426950408889634
SB_LOG2_CUTOFF = -160.0
SB_ROW_TILE = 128
SB_SKEW_CUMSUM = 3
SB_SKEW_VALUES = 6
GDN_CHUNKS_PER_STEP = 8
RWKV_CHUNKS_PER_STEP = 8
MOE_ROW_TILE = 512
VMEM_LIMIT_BYTES = 56 * 1024 * 1024


def _dot(a, b):
    return jnp.dot(a.astype(BF16), b.astype(BF16), preferred_element_type=F32)


def _dot_nt(a, b):
    return lax.dot_general(a.astype(BF16), b.astype(BF16), (((1,), (1,)), ((), ())),
                           preferred_element_type=F32)


def _split2(a):
    hi = a.astype(BF16)
    lo = (a - hi.astype(F32)).astype(BF16)
    return hi, lo


def _split3(a):
    hi = a.astype(BF16)
    r = a - hi.astype(F32)
    mid = r.astype(BF16)
    lo = (r - mid.astype(F32)).astype(BF16)
    return hi, mid, lo


def _dot3(a, b):
    ah, al = _split2(a)
    bh, bl = _split2(b)
    return (jnp.dot(ah, bh, preferred_element_type=F32)
            + jnp.dot(ah, bl, preferred_element_type=F32)
            + jnp.dot(al, bh, preferred_element_type=F32))


def _dot_exact_lhs(m, x, parts=3):
    xs = _split3(x) if parts == 3 else _split2(x)
    out = jnp.dot(m, xs[0], preferred_element_type=F32)
    for p in xs[1:]:
        out = out + jnp.dot(m, p, preferred_element_type=F32)
    return out


def _dot_exact_rhs(x, m, parts=2):
    xs = _split3(x) if parts == 3 else _split2(x)
    out = jnp.dot(xs[0], m, preferred_element_type=F32)
    for p in xs[1:]:
        out = out + jnp.dot(p, m, preferred_element_type=F32)
    return out


def _iota2(shape, dim):
    return lax.broadcasted_iota(jnp.int32, shape, dim)


def _eye(n, dtype=F32):
    return (_iota2((n, n), 0) == _iota2((n, n), 1)).astype(dtype)


def _softplus(x):
    return jnp.maximum(x, 0.0) + jnp.log(1.0 + jnp.exp(-jnp.abs(x)))


def _sigmoid(x):
    return 1.0 / (1.0 + jnp.exp(-x))


def _silu(x):
    return x * _sigmoid(x)


def _rms(x, g):
    return x * lax.rsqrt(jnp.mean(x * x, axis=-1, keepdims=True) + RMS_EPS) * g


def _nilpotent_inverse_many(ns, eye, dot):
    ts = [eye + n for n in ns]
    xs = [dot(n, n) for n in ns]
    for _ in range(4):
        xt = [dot(jnp.concatenate([x, t], axis=0), x) for x, t in zip(xs, ts)]
        xs = [y[0:CHUNK] for y in xt]
        ts = [t + y[CHUNK:2 * CHUNK] for t, y in zip(ts, xt)]
    return [t + dot(t, x) for t, x in zip(ts, xs)]


def _chunk_cumsum(x):
    i = _iota2((128, 128), 0)
    j = _iota2((128, 128), 1)
    m = ((j <= i) & ((i >> 6) == (j >> 6))).astype(BF16)
    return jnp.concatenate(
        [_dot_exact_lhs(m, x[r:r + 128]) for r in range(0, x.shape[0], 128)], axis=0)


def _head_sum(x):
    i = _iota2((128, 128), 0)
    j = _iota2((128, 128), 1)
    m = ((i >> 6) == (j >> 6)).astype(BF16)
    return jnp.concatenate(
        [_dot_exact_rhs(x[:, c:c + 128], m) for c in range(0, x.shape[1], 128)], axis=1)


def _normproj_kernel(x_ref, g_ref, *refs, n_out):
    w_refs = refs[:n_out]
    o_refs = refs[n_out:]
    hn = _rms(x_ref[...], g_ref[...]).astype(BF16)
    for w_ref, o_ref in zip(w_refs, o_refs):
        o_ref[...] = jnp.dot(hn, w_ref[...], preferred_element_type=F32).astype(o_ref.dtype)


def _normproj(x2d, g, weights, out_dtypes, tm=512):
    n, d = x2d.shape
    n_out = len(weights)
    in_specs = [pl.BlockSpec((tm, d), lambda i: (i, 0)), pl.BlockSpec((1, d), lambda i: (0, 0))]
    in_specs += [pl.BlockSpec(w.shape, lambda i: (0, 0), pipeline_mode=pl.Buffered(1)) for w in weights]
    out_specs = [pl.BlockSpec((tm, w.shape[1]), lambda i: (i, 0)) for w in weights]
    out_shape = [jax.ShapeDtypeStruct((n, w.shape[1]), dt) for w, dt in zip(weights, out_dtypes)]
    return pl.pallas_call(
        functools.partial(_normproj_kernel, n_out=n_out),
        grid=(n // tm,),
        in_specs=in_specs,
        out_specs=out_specs,
        out_shape=out_shape,
        compiler_params=pltpu.CompilerParams(
            dimension_semantics=("arbitrary",), vmem_limit_bytes=VMEM_LIMIT_BYTES),
        name="normproj",
    )(x2d, g.reshape(1, d), *weights)


def _gdn_kernel(qkv_ref, z_ref, ba_ref, cw_ref, alog_ref, dtb_ref, ng_ref, o_ref,
                ext_s, state_s, q_s, k_s, v_s, beta_s, gc_s, p_s, qq_s, r_s, zz_s, oc_s, *, tt):
    nc = tt // CHUNK
    w3 = 3 * GDN_WIDTH
    t = pl.program_id(1)

    @pl.when(t == 0)
    def _():
        ext_s[0:8, :] = jnp.zeros((8, w3), F32)
        state_s[...] = jnp.zeros_like(state_s)

    raw = qkv_ref[0]
    ext_s[8:8 + tt, :] = raw
    cw = cw_ref[...]
    y = raw * cw[3:4, :]
    for i in range(3):
        y = y + ext_s[5 + i:5 + i + tt, :] * cw[i:i + 1, :]
    ext_s[0:8, :] = raw[tt - 8:tt, :]
    y = _silu(y)

    for h in range(GDN_HEADS):
        sl = slice(128 * h, 128 * h + 128)
        qh = y[:, 128 * h:128 * h + 128]
        kh = y[:, GDN_WIDTH + 128 * h:GDN_WIDTH + 128 * h + 128]
        vh = y[:, 2 * GDN_WIDTH + 128 * h:2 * GDN_WIDTH + 128 * h + 128]
        qh = qh * lax.rsqrt(jnp.sum(qh * qh, axis=-1, keepdims=True) + L2_EPS) * (GDN_HEAD_DIM ** -0.5)
        kh = kh * lax.rsqrt(jnp.sum(kh * kh, axis=-1, keepdims=True) + L2_EPS)
        q_s[:, sl] = qh
        k_s[:, sl] = kh
        v_s[:, sl] = vh

    ba = ba_ref[0]
    gate = jnp.where(_iota2(ba.shape, 1) < GDN_HEADS, _sigmoid(ba),
                     -jnp.exp(alog_ref[...]) * _softplus(ba + dtb_ref[...]))
    ec = _iota2((128, 2 * GDN_WIDTH), 0)
    el = _iota2((128, 2 * GDN_WIDTH), 1)
    gate = _dot_exact_rhs(gate, (ec == (el >> 7)).astype(BF16), parts=3)
    beta_s[...] = gate[:, 0:GDN_WIDTH]
    gc_s[...] = _chunk_cumsum(gate[:, GDN_WIDTH:2 * GDN_WIDTH])

    eye64 = _eye(CHUNK)
    eye128 = _eye(128)
    ii = _iota2((CHUNK, CHUNK), 0)
    jj = _iota2((CHUNK, CHUNK), 1)

    def chunk_body(ci, carry):
        probs = [(ci * GDN_CHUNKS_PER_STEP + u, h) for u in range(GDN_CHUNKS_PER_STEP)
                 for h in range(GDN_HEADS)]
        rows = [pl.ds(pl.multiple_of(c * CHUNK, CHUNK), CHUNK) for c, _ in probs]
        lanes = [slice(128 * h, 128 * h + 128) for _, h in probs]
        idx = range(len(probs))
        q = [q_s[rows[i], lanes[i]] for i in idx]
        k = [k_s[rows[i], lanes[i]] for i in idx]
        gcc = [gc_s[rows[i], lanes[i]] for i in idx]
        gl = [g[CHUNK - 1:CHUNK, :] for g in gcc]
        gcr = [g.T[0:CHUNK, :] for g in gcc]
        dec_incl = [jnp.exp(jnp.where(ii >= jj, gcc[i][:, 0:CHUNK] - gcr[i], -jnp.inf)) for i in idx]
        kb = [k[i] * beta_s[rows[i], lanes[i]] for i in idx]
        a_mat = [_dot_nt(kb[i], k[i]) for i in idx]
        attn = [_dot_nt(q[i], k[i]) * dec_incl[i] for i in idx]
        kdt = [(k[i] * jnp.exp(gl[i] - gcc[i])).T for i in idx]
        tinv = _nilpotent_inverse_many(
            [-a_mat[i] * jnp.where(ii > jj, dec_incl[i], 0.0) for i in idx], eye64, _dot)
        u = [_dot(tinv[i], v_s[rows[i], lanes[i]] * beta_s[rows[i], lanes[i]]) for i in idx]
        w = [_dot(tinv[i], kb[i] * jnp.exp(gcc[i])) for i in idx]
        for i, (c, h) in enumerate(probs):
            p_s[c, h] = jnp.exp(gl[i]) * eye128 - _dot(kdt[i], w[i])
        for i, (c, h) in enumerate(probs):
            qq_s[c, h] = _dot(kdt[i], u[i])
        for i, (c, h) in enumerate(probs):
            r_s[c, h] = q[i] * jnp.exp(gcc[i]) - _dot(attn[i], w[i])
        for i, (c, h) in enumerate(probs):
            zz_s[c, h] = _dot(attn[i], u[i])
        return carry

    lax.fori_loop(0, nc // GDN_CHUNKS_PER_STEP, chunk_body, 0)

    def scan_body(c, carry):
        rows = pl.ds(pl.multiple_of(c * CHUNK, CHUNK), CHUNK)
        s = [state_s[h] for h in range(GDN_HEADS)]
        s_new = [_dot3(p_s[c, h], s[h]) for h in range(GDN_HEADS)]
        o = [_dot(r_s[c, h], s[h]) for h in range(GDN_HEADS)]
        for h in range(GDN_HEADS):
            state_s[h] = s_new[h] + qq_s[c, h]
            oc_s[rows, 128 * h:128 * h + 128] = o[h] + zz_s[c, h]
        return carry

    lax.fori_loop(0, nc, scan_body, 0)

    o = oc_s[...]
    z = z_ref[0]
    ng = ng_ref[...]
    for h in range(GDN_HEADS):
        sl = slice(128 * h, 128 * h + 128)
        oh = o[:, sl]
        oh = oh * lax.rsqrt(jnp.mean(oh * oh, axis=-1, keepdims=True) + RMS_EPS) * ng[:, sl]
        o_ref[0, :, sl] = (oh * _silu(z[:, sl])).astype(o_ref.dtype)


def _gdn(proj_a, proj_ba, conv_w, a_log, dt_bias, norm_g, tt=512):
    b, t, _ = proj_a.shape
    nc = tt // CHUNK
    w3 = 3 * GDN_WIDTH
    rep = lambda p: jnp.pad(p.astype(F32), (GDN_HEADS, 128 - 2 * GDN_HEADS)).reshape(1, 128)
    small = lambda shape: pl.BlockSpec(shape, lambda i, j: (0, 0))
    return pl.pallas_call(
        functools.partial(_gdn_kernel, tt=tt),
        grid=(b, t // tt),
        in_specs=[
            pl.BlockSpec((1, tt, w3), lambda i, j: (i, j, 0)),
            pl.BlockSpec((1, tt, GDN_WIDTH), lambda i, j: (i, j, 3)),
            pl.BlockSpec((1, tt, 128), lambda i, j: (i, j, 0)),
            small((4, w3)), small((1, 128)), small((1, 128)), small((1, GDN_WIDTH)),
        ],
        out_specs=pl.BlockSpec((1, tt, GDN_WIDTH), lambda i, j: (i, j, 0)),
        out_shape=jax.ShapeDtypeStruct((b, t, GDN_WIDTH), BF16),
        scratch_shapes=[
            pltpu.VMEM((tt + 8, w3), F32),
            pltpu.VMEM((GDN_HEADS, 128, 128), F32),
            pltpu.VMEM((tt, GDN_WIDTH), F32), pltpu.VMEM((tt, GDN_WIDTH), F32),
            pltpu.VMEM((tt, GDN_WIDTH), F32), pltpu.VMEM((tt, GDN_WIDTH), F32),
            pltpu.VMEM((tt, GDN_WIDTH), F32),
            pltpu.VMEM((nc, GDN_HEADS, 128, 128), F32), pltpu.VMEM((nc, GDN_HEADS, 128, 128), F32),
            pltpu.VMEM((nc, GDN_HEADS, CHUNK, 128), F32), pltpu.VMEM((nc, GDN_HEADS, CHUNK, 128), F32),
            pltpu.VMEM((tt, GDN_WIDTH), F32),
        ],
        compiler_params=pltpu.CompilerParams(
            dimension_semantics=("arbitrary", "arbitrary"), vmem_limit_bytes=VMEM_LIMIT_BYTES),
        name="gdn",
    )(proj_a, proj_a, proj_ba, conv_w.astype(F32), rep(a_log), rep(dt_bias),
      jnp.tile(norm_g.astype(F32), GDN_HEADS).reshape(1, GDN_WIDTH))


def _rwkv_kernel(h_ref, mu_ref, w0_ref, a0_ref, kk_ref, ka_ref, rk_ref, lng_ref, lnb_ref,
                 wa_ref, g2_ref, o_ref,
                 ext_s, state_s, r_s, kn_s, k2_s, v_s, a_s, lw_s, lc_s, p_s, qq_s, rh_s, yc_s, y_s,
                 *, tt):
    nc = tt // CHUNK
    npair = RWKV_WIDTH // 128
    t = pl.program_id(1)

    @pl.when(t == 0)
    def _():
        ext_s[0:8, :] = jnp.zeros((8, RWKV_IN), F32)
        state_s[...] = jnp.zeros_like(state_s)

    raw = h_ref[0]
    ext_s[8:8 + tt, :] = raw
    prev = ext_s[7:7 + tt, :]
    ext_s[0:8, :] = raw[tt - 8:tt, :]
    hl = raw + (prev - raw) * mu_ref[...]
    r = hl[:, 0:512]
    k = hl[:, 512:1024]
    v = hl[:, 1024:1536]
    xwa = hl[:, 1536:1664]
    xg = hl[:, 1664:1792]
    lane128 = _iota2((1, 128), 1)
    xwa = jnp.where(lane128 < 64, jnp.tanh(xwa), xwa)
    lora = _dot3(xwa, wa_ref[...])
    w_log = -_softplus(-(w0_ref[...] + lora[:, 0:512])) - 0.5
    lw = -jnp.exp(w_log)
    a = _sigmoid(a0_ref[...] + lora[:, 512:1024])
    gate = _dot3(_sigmoid(xg), g2_ref[...])
    kk = k * kk_ref[...]
    k2 = k * (1.0 + (a - 1.0) * ka_ref[...])
    kn = kk * lax.rsqrt(_head_sum(kk * kk) + L2_EPS)
    r_s[...] = r
    kn_s[...] = kn
    k2_s[...] = k2
    v_s[...] = v
    a_s[...] = a
    lw_s[...] = lw
    lc_s[...] = _chunk_cumsum(lw)

    eye128 = _eye(128)
    row = _iota2((CHUNK, 128), 0)
    lane = _iota2((CHUNK, 128), 1)
    col = lane & (CHUNK - 1)
    strict = row > col
    incl = row >= col
    eye2 = (row == col).astype(F32)
    head0 = lane < CHUNK
    head0_wide = (_iota2((CHUNK, 256), 1) & 127) < CHUNK
    bi = _iota2((128, 128), 0)
    bj = _iota2((128, 128), 1)
    blockdiag = (bi < 64) == (bj < 64)

    def per_head_rows(x, mask=head0):
        return jnp.concatenate([jnp.where(mask, x, 0.0), jnp.where(mask, 0.0, x)], axis=0)

    def chunk_body(ci, carry):
        units = [(ci * RWKV_CHUNKS_PER_STEP + u, p) for u in range(RWKV_CHUNKS_PER_STEP) for p in range(npair)]
        nu = range(len(units))
        rows = [pl.ds(pl.multiple_of(c * CHUNK, CHUNK), CHUNK) for c, _ in units]
        sls = [slice(128 * p, 128 * p + 128) for _, p in units]
        lcc = [lc_s[rows[u], sls[u]] for u in nu]
        vc = [v_s[rows[u], sls[u]] for u in nu]
        lcl = [x[CHUNK - 1:CHUNK, :] for x in lcc]
        ginv = [jnp.exp(-x) for x in lcc]
        gend = [jnp.exp(lcl[u] - lcc[u]) for u in nu]
        kna = [kn_s[rows[u], sls[u]] * a_s[rows[u], sls[u]] for u in nu]
        at = [-kn_s[rows[u], sls[u]] * jnp.exp(lcc[u] - lw_s[rows[u], sls[u]]) for u in nu]
        rt = [r_s[rows[u], sls[u]] * jnp.exp(lcc[u]) for u in nu]
        ar = [jnp.concatenate([at[u], rt[u]], axis=0) for u in nu]
        xb = [_dot_nt(ar[u], per_head_rows(kna[u] * ginv[u])) for u in nu]
        xk = [_dot_nt(ar[u], per_head_rows(k2_s[rows[u], sls[u]] * ginv[u])) for u in nu]
        a_ab = [jnp.where(strict, x[0:CHUNK], 0.0) for x in xb]
        a_rb = [jnp.where(incl, x[CHUNK:2 * CHUNK], 0.0) for x in xb]
        a_ak = [jnp.where(strict, x[0:CHUNK], 0.0) for x in xk]
        a_rk = [jnp.where(incl, x[CHUNK:2 * CHUNK], 0.0) for x in xk]
        vrows = [per_head_rows(vc[u]) for u in nu]
        akv = [_dot(a_ak[u], vrows[u]) for u in nu]
        arkv = [_dot(a_rk[u], vrows[u]) for u in nu]
        bbt = [(kna[u] * gend[u]).T for u in nu]
        kbt = [(k2_s[rows[u], sls[u]] * gend[u]).T for u in nu]
        ts = [eye2 + a for a in a_ab]
        xs = [_dot(x, per_head_rows(x)) for x in a_ab]
        for step in range(5):
            if step < 4:
                xt = [_dot(jnp.concatenate([x, t], axis=0), per_head_rows(x)) for x, t in zip(xs, ts)]
                xs = [y[0:CHUNK] for y in xt]
                ts = [t + y[CHUNK:2 * CHUNK] for t, y in zip(ts, xt)]
            else:
                ts = [t + _dot(t, per_head_rows(x)) for t, x in zip(ts, xs)]
        au = [_dot(ts[u], per_head_rows(jnp.concatenate([at[u], akv[u]], axis=1), head0_wide)) for u in nu]
        ry = [_dot(a_rb[u], per_head_rows(au[u], head0_wide)) for u in nu]
        kv = [_dot(kbt[u], vc[u]) for u in nu]
        pq = [_dot(bbt[u], au[u]) for u in nu]
        for u, (c, p) in enumerate(units):
            rh_s[c, p] = rt[u] + ry[u][:, 0:128]
            yc_s[c, p] = ry[u][:, 128:256] + arkv[u]
            p_s[c, p] = jnp.where(blockdiag, pq[u][:, 0:128], 0.0) + eye128 * jnp.exp(lcl[u])
            qq_s[c, p] = jnp.where(blockdiag, pq[u][:, 128:256] + kv[u], 0.0)
        return carry

    lax.fori_loop(0, nc // RWKV_CHUNKS_PER_STEP, chunk_body, 0)

    def scan_body(c, carry):
        rows = pl.ds(pl.multiple_of(c * CHUNK, CHUNK), CHUNK)
        s = [state_s[p] for p in range(npair)]
        s_new = [_dot3(p_s[c, p], s[p]) for p in range(npair)]
        y = [_dot(rh_s[c, p], s[p]) for p in range(npair)]
        for p in range(npair):
            state_s[p] = s_new[p] + qq_s[c, p]
            y_s[rows, 128 * p:128 * p + 128] = y[p] + yc_s[c, p]
        return carry

    lax.fori_loop(0, nc, scan_body, 0)

    y = y_s[...]
    mean = _head_sum(y) * (1.0 / RWKV_HEAD_DIM)
    yc = y - mean
    var = _head_sum(yc * yc) * (1.0 / RWKV_HEAD_DIM)
    yn = yc * lax.rsqrt(var + GN_EPS) * lng_ref[...] + lnb_ref[...]
    bonus = _head_sum(r * k2 * rk_ref[...]) * v
    o_ref[0] = ((yn + bonus) * gate).astype(o_ref.dtype)


def _rwkv(h, mu, w0, w2, a0, a2, g2, k_k, k_a, r_k, ln_g, ln_b, tt=512):
    b, t, _ = h.shape
    nc = tt // CHUNK
    npair = RWKV_WIDTH // 128
    row = lambda p: p.astype(F32).reshape(1, -1)
    wa = jnp.zeros((128, 2 * RWKV_WIDTH), F32)
    wa = wa.at[0:64, 0:RWKV_WIDTH].set(w2.astype(F32)).at[64:128, RWKV_WIDTH:].set(a2.astype(F32))
    small = lambda shape: pl.BlockSpec(shape, lambda i, j: (0, 0))
    vec = small((1, RWKV_WIDTH))
    return pl.pallas_call(
        functools.partial(_rwkv_kernel, tt=tt),
        grid=(b, t // tt),
        in_specs=[pl.BlockSpec((1, tt, RWKV_IN), lambda i, j: (i, j, 0)), small((1, RWKV_IN)),
                  vec, vec, vec, vec, vec, vec, vec,
                  small((128, 2 * RWKV_WIDTH)), small((128, RWKV_WIDTH))],
        out_specs=pl.BlockSpec((1, tt, RWKV_WIDTH), lambda i, j: (i, j, 0)),
        out_shape=jax.ShapeDtypeStruct((b, t, RWKV_WIDTH), BF16),
        scratch_shapes=[
            pltpu.VMEM((tt + 8, RWKV_IN), F32),
            pltpu.VMEM((npair, 128, 128), F32),
        ] + [pltpu.VMEM((tt, RWKV_WIDTH), F32)] * 7 + [
            pltpu.VMEM((nc, npair, 128, 128), F32), pltpu.VMEM((nc, npair, 128, 128), F32),
            pltpu.VMEM((nc, npair, CHUNK, 128), F32), pltpu.VMEM((nc, npair, CHUNK, 128), F32),
            pltpu.VMEM((tt, RWKV_WIDTH), F32),
        ],
        compiler_params=pltpu.CompilerParams(
            dimension_semantics=("arbitrary", "arbitrary"), vmem_limit_bytes=VMEM_LIMIT_BYTES),
        name="rwkv7",
    )(h, row(mu), row(w0), row(a0), row(k_k), row(k_a), row(r_k), row(ln_g), row(ln_b),
      wa, g2.astype(F32))


def _sb_kernel(q_ref, k_ref, v_ref, o_ref, acc_s, aux_s, *, bq, bk):
    qi = pl.program_id(2)
    nsub = bq // bk
    q = q_ref[0] * (SB_HEAD_DIM ** -0.5 * LOG2E)
    lane = _iota2((1, 128), 1)
    qm = [jnp.where(lane < 64, q, 0.0).astype(BF16), jnp.where(lane >= 64, q, 0.0).astype(BF16)]
    ti = _iota2((2 * bk, bk), 0)
    tj = _iota2((2 * bk, bk), 1)
    cum_mat = ((ti & (bk - 1)) > tj).astype(BF16)
    acc_s[...] = jnp.zeros_like(acc_s)
    aux_s[...] = jnp.zeros_like(aux_s)

    def load_kv(kb):
        start = pl.multiple_of(kb * bk, bk)
        return (start, k_ref[0, pl.ds(start, bk), :].astype(BF16), v_ref[0, pl.ds(start, bk), :].astype(BF16))

    def stage_scores(item):
        (start, k, v), ra, nr, j, masked = item
        z = lax.dot_general(qm[j][ra:ra + nr], k, (((1,), (1,)), ((), ())), preferred_element_type=F32)
        return z

    def stage_cumsum(item, z):
        (start, k, v), ra, nr, j, masked = item
        rows = slice(ra, ra + nr)
        neg_abs = lax.bitcast_convert_type(
            lax.bitcast_convert_type(z, jnp.uint32) | jnp.uint32(0x80000000), F32)
        lsig = jnp.minimum(z, 0.0) - jnp.log(1.0 + jnp.exp2(neg_abs)) * LOG2E
        l1 = lsig - z
        msk = None
        if masked:
            msk = (start + _iota2((nr, bk), 1)) < (qi * bq + ra + _iota2((nr, bk), 0))
            l1 = jnp.where(msk, l1, 0.0)
        l1_hi = l1.astype(BF16)
        l1_lo = (l1 - l1_hi.astype(F32)).astype(BF16)
        cr = (jnp.dot(jnp.concatenate([l1_hi, l1_lo], axis=1), cum_mat, preferred_element_type=F32)
              + aux_s[j, rows, 0:1])
        aux_s[j, rows, :] = cr + l1
        return lsig, cr, msk

    def stage_values(item, state):
        (start, k, v), ra, nr, j, masked = item
        lsig, cr, msk = state
        att = jnp.exp2(lsig + cr)
        if masked:
            att = jnp.where(msk, att, 0.0)
        acc_s[j, ra:ra + nr, :] += jnp.dot(att.astype(BF16), v, preferred_element_type=F32)

    def run_items(items):
        n = len(items)
        zs = {}
        states = {}
        for s in range(n + SB_SKEW_VALUES):
            if s < n:
                zs[s] = stage_scores(items[s])
            if 0 <= s - SB_SKEW_CUMSUM < n:
                i = s - SB_SKEW_CUMSUM
                states[i] = stage_cumsum(items[i], zs.pop(i))
            if 0 <= s - SB_SKEW_VALUES < n:
                i = s - SB_SKEW_VALUES
                stage_values(items[i], states.pop(i))

    def max_carry(ra, nr):
        r = jnp.maximum(aux_s[0, ra:ra + nr, :], aux_s[1, ra:ra + nr, :])
        r = jnp.max(jnp.where(lane == 0, r, -jnp.inf), axis=0, keepdims=True)
        return jnp.max(r, axis=1, keepdims=True)[0, 0]

    items = []
    for d in range(nsub - 1, -1, -1):
        kv = load_kv(qi * nsub + d)
        ra = d * bk
        while ra < bq:
            nr = SB_ROW_TILE if (bq - ra) % (2 * SB_ROW_TILE) else 2 * SB_ROW_TILE
            on_diagonal = ra == d * bk
            if on_diagonal:
                nr = SB_ROW_TILE
            items += [(kv, ra, nr, 0, on_diagonal), (kv, ra, nr, 1, on_diagonal)]
            ra += nr
    run_items(items)

    def cond(c):
        kb, rmax = c
        return (kb >= 0) & (rmax > SB_LOG2_CUTOFF)

    for ra in range(0, bq, 2 * SB_ROW_TILE):
        nr = 2 * SB_ROW_TILE

        def body(c, ra=ra, nr=nr):
            kb, _ = c
            items = []
            for kv in (load_kv(kb), load_kv(kb - 1)):
                items += [(kv, ra, nr, 0, False), (kv, ra, nr, 1, False)]
            run_items(items)
            return kb - 2, max_carry(ra, nr)

        lax.while_loop(cond, body, (qi * nsub - 1, max_carry(ra, nr)))
    o_ref[0] = jnp.where(lane < 64, acc_s[0], acc_s[1]).astype(o_ref.dtype)


def _stick_breaking(h_sb, bq=512, bk=128):
    b, t, _ = h_sb.shape
    npair = SB_WIDTH // 128
    assert bk == SB_ROW_TILE and bq % (2 * SB_ROW_TILE) == 0 and t % bq == 0
    return pl.pallas_call(
        functools.partial(_sb_kernel, bq=bq, bk=bk),
        grid=(b, npair, t // bq),
        in_specs=[
            pl.BlockSpec((1, bq, 128), lambda i, p, j: (i, j, p)),
            pl.BlockSpec((1, t, 128), lambda i, p, j: (i, 0, npair + p)),
            pl.BlockSpec((1, t, 128), lambda i, p, j: (i, 0, 2 * npair + p)),
        ],
        out_specs=pl.BlockSpec((1, bq, 128), lambda i, p, j: (i, j, p)),
        out_shape=jax.ShapeDtypeStruct((b, t, SB_WIDTH), BF16),
        scratch_shapes=[pltpu.VMEM((2, bq, 128), F32), pltpu.VMEM((2, bq, bk), F32)],
        compiler_params=pltpu.CompilerParams(
            dimension_semantics=("arbitrary", "arbitrary", "arbitrary"),
            vmem_limit_bytes=VMEM_LIMIT_BYTES),
        name="stickbreak",
    )(h_sb, h_sb, h_sb)


def _merge_kernel(x_ref, oa_ref, ob_ref, oc_ref, gt_ref, wa_ref, wb_ref, wc_ref, wo_ref, o_ref):
    d = D_MODEL
    g = _sigmoid(gt_ref[...].astype(F32))
    m = (g[:, 0:d] * _dot(oa_ref[...], wa_ref[...])
         + g[:, d:2 * d] * _dot(ob_ref[...], wb_ref[...])
         + g[:, 2 * d:3 * d] * _dot(oc_ref[...], wc_ref[...]))
    o_ref[...] = x_ref[...] + _dot(m, wo_ref[...])


def _merge(x2d, oa, ob, oc, gates, wa, wb, wc, wo, tm=512):
    n, d = x2d.shape
    rowspec = lambda w: pl.BlockSpec((tm, w), lambda i: (i, 0))
    full = lambda w: pl.BlockSpec(w.shape, lambda i: (0, 0))
    return pl.pallas_call(
        _merge_kernel,
        grid=(n // tm,),
        in_specs=[rowspec(d), rowspec(512), rowspec(512), rowspec(512), rowspec(3 * d),
                  full(wa), full(wb), full(wc), full(wo)],
        out_specs=rowspec(d),
        out_shape=jax.ShapeDtypeStruct((n, d), F32),
        compiler_params=pltpu.CompilerParams(
            dimension_semantics=("arbitrary",), vmem_limit_bytes=VMEM_LIMIT_BYTES),
        name="merge",
    )(x2d, oa, ob, oc, gates, wa, wb, wc, wo)


def _router_kernel(x_ref, g_ref, rw_ref, comb_ref, idx_ref, cnt_ref, run_s):
    @pl.when(pl.program_id(0) == 0)
    def _():
        run_s[...] = jnp.zeros_like(run_s)

    hn = _rms(x_ref[...], g_ref[...])
    logits = _dot3(hn, rw_ref[...])
    tm = logits.shape[0]
    lane = _iota2(logits.shape, 1)
    m1 = jnp.max(logits, axis=-1, keepdims=True)
    i1 = jnp.min(jnp.where(logits == m1, lane, N_EXPERTS), axis=-1, keepdims=True)
    sel1 = lane == i1
    rest = jnp.where(sel1, -jnp.inf, logits)
    m2 = jnp.max(rest, axis=-1, keepdims=True)
    i2 = jnp.min(jnp.where(rest == m2, lane, N_EXPERTS), axis=-1, keepdims=True)
    sel2 = lane == i2
    e2 = jnp.exp(m2 - m1)
    w1 = 1.0 / (1.0 + e2)
    comb_ref[...] = jnp.where(sel1, w1, 0.0) + jnp.where(sel2, e2 * w1, 0.0)

    member = (sel1 | sel2).astype(BF16)
    before = (_iota2((tm, tm), 0) > _iota2((tm, tm), 1)).astype(BF16)
    rank = jnp.dot(before, member, preferred_element_type=F32) + run_s[...]
    r1 = jnp.sum(jnp.where(sel1, rank, 0.0), axis=-1, keepdims=True).astype(jnp.int32)
    r2 = jnp.sum(jnp.where(sel2, rank, 0.0), axis=-1, keepdims=True).astype(jnp.int32)
    run_s[...] += jnp.sum(member.astype(F32), axis=0, keepdims=True)
    cnt_ref[...] = run_s[...]
    idx_ref[...] = jnp.where(lane == 0, i1, jnp.where(lane == 1, i2, jnp.where(lane == 2, r1,
                             jnp.where(lane == 3, r2, 0))))


def _router(x2d, g, rw, tm=512):
    n, d = x2d.shape
    return pl.pallas_call(
        _router_kernel,
        grid=(n // tm,),
        in_specs=[pl.BlockSpec((tm, d), lambda i: (i, 0)), pl.BlockSpec((1, d), lambda i: (0, 0)),
                  pl.BlockSpec(rw.shape, lambda i: (0, 0))],
        out_specs=[pl.BlockSpec((tm, N_EXPERTS), lambda i: (i, 0)),
                   pl.BlockSpec((tm, N_EXPERTS), lambda i: (i, 0)),
                   pl.BlockSpec((1, N_EXPERTS), lambda i: (0, 0))],
        out_shape=[jax.ShapeDtypeStruct((n, N_EXPERTS), F32),
                   jax.ShapeDtypeStruct((n, N_EXPERTS), jnp.int32),
                   jax.ShapeDtypeStruct((1, N_EXPERTS), F32)],
        scratch_shapes=[pltpu.VMEM((1, N_EXPERTS), F32)],
        compiler_params=pltpu.CompilerParams(dimension_semantics=("arbitrary",)),
        name="router",
    )(x2d, g.reshape(1, d), rw.astype(F32))


def _row_copy(src_ref, src_row, dst_ref, dst_row, sem):
    return pltpu.make_async_copy(src_ref.at[pl.ds(src_row, 1), :], dst_ref.at[pl.ds(dst_row, 1), :], sem)


def _drain_rows(n_copies, src_ref, dst_ref, sem):
    def body(t, carry):
        _row_copy(src_ref, 0, dst_ref, 0, sem).wait()
        return carry

    lax.fori_loop(0, n_copies, body, 0, unroll=8)


def _dispatch_kernel(pos1_ref, pos2_ref, x_ref, g_ref, xs_in_ref, xs_ref, hn_s, sems):
    del xs_in_ref
    tm = hn_s.shape[1]
    i = pl.program_id(0)
    slot = i % 2
    hn_s[slot] = _rms(x_ref[...], g_ref[...])
    src = hn_s.at[slot]
    sem = sems.at[slot]

    def issue(t, carry):
        _row_copy(src, t, xs_ref, pos1_ref[t], sem).start()
        _row_copy(src, t, xs_ref, pos2_ref[t], sem).start()
        return carry

    lax.fori_loop(0, tm, issue, 0, unroll=8)

    @pl.when(i > 0)
    def _():
        _drain_rows(2 * tm, hn_s.at[1 - slot], xs_ref, sems.at[1 - slot])

    @pl.when(i == pl.num_programs(0) - 1)
    def _():
        _drain_rows(2 * tm, src, xs_ref, sem)


def _dispatch(x2d, g, pos1, pos2, n_rows, tm=512):
    n, d = x2d.shape
    smem = lambda: pl.BlockSpec((tm,), lambda i: (i,), memory_space=pltpu.SMEM)
    return pl.pallas_call(
        _dispatch_kernel,
        grid=(n // tm,),
        in_specs=[smem(), smem(), pl.BlockSpec((tm, d), lambda i: (i, 0)),
                  pl.BlockSpec((1, d), lambda i: (0, 0)), pl.BlockSpec(memory_space=pl.ANY)],
        out_specs=pl.BlockSpec(memory_space=pl.ANY),
        out_shape=jax.ShapeDtypeStruct((n_rows, d), F32),
        scratch_shapes=[pltpu.VMEM((2, tm, d), F32), pltpu.SemaphoreType.DMA((2,))],
        input_output_aliases={4: 0},
        compiler_params=pltpu.CompilerParams(dimension_semantics=("arbitrary",)),
        name="moe_dispatch",
    )(pos1, pos2, x2d, g.reshape(1, d), jnp.zeros((n_rows, d), F32))


def _grouped_ffn_kernel(te_ref, nt_ref, xs_ref, wg_ref, wu_ref, wd_ref, o_ref, xb_s, acc_s):
    i = pl.program_id(0)
    f = pl.program_id(1)

    @pl.when(f == 0)
    def _():
        xb_s[...] = xs_ref[...].astype(BF16)
        acc_s[...] = jnp.zeros_like(acc_s)

    @pl.when(i < nt_ref[0])
    def _():
        xb = xb_s[...]
        act = _silu(jnp.dot(xb, wg_ref[...], preferred_element_type=F32)) * jnp.dot(
            xb, wu_ref[...], preferred_element_type=F32)
        acc_s[...] += jnp.dot(act.astype(BF16), wd_ref[...], preferred_element_type=F32)

    @pl.when(f == pl.num_programs(1) - 1)
    def _():
        o_ref[...] = acc_s[...]


def _grouped_ffn(xs, tile_expert, n_tiles, wg, wu, wd, tm, tf=1792):
    n_rows, d = xs.shape
    dff = wg.shape[2]
    nf = dff // tf
    fsel = lambda i, f, te, nt: jnp.where(i < nt[0], f, nf - 1)
    return pl.pallas_call(
        _grouped_ffn_kernel,
        grid_spec=pltpu.PrefetchScalarGridSpec(
            num_scalar_prefetch=2,
            grid=(n_rows // tm, nf),
            in_specs=[pl.BlockSpec((tm, d), lambda i, f, te, nt: (i, 0)),
                      pl.BlockSpec((None, d, tf), lambda i, f, te, nt: (te[i], 0, fsel(i, f, te, nt))),
                      pl.BlockSpec((None, d, tf), lambda i, f, te, nt: (te[i], 0, fsel(i, f, te, nt))),
                      pl.BlockSpec((None, tf, d), lambda i, f, te, nt: (te[i], fsel(i, f, te, nt), 0))],
            out_specs=pl.BlockSpec((tm, d), lambda i, f, te, nt: (i, 0)),
            scratch_shapes=[pltpu.VMEM((tm, d), BF16), pltpu.VMEM((tm, d), F32)]),
        out_shape=jax.ShapeDtypeStruct((n_rows, d), F32),
        compiler_params=pltpu.CompilerParams(
            dimension_semantics=("arbitrary", "arbitrary"), vmem_limit_bytes=VMEM_LIMIT_BYTES),
        name="moe_grouped_ffn",
    )(tile_expert, n_tiles, xs, wg, wu, wd)


def _combine_kernel(*refs, final):
    if final:
        pos1_ref, pos2_ref, npos1_ref, npos2_ref, x_ref, comb_ref, fg_ref, ys_ref, o_ref, y1_s, y2_s, sems = refs
    else:
        pos1_ref, pos2_ref, npos1_ref, npos2_ref, x_ref, comb_ref, ys_ref, o_ref, y1_s, y2_s, sems = refs
    tm = y1_s.shape[1]
    i = pl.program_id(0)
    slot = i % 2

    def gather(p1_ref, p2_ref, sl):
        def issue(t, carry):
            _row_copy(ys_ref, p1_ref[t], y1_s.at[sl], t, sems.at[sl]).start()
            _row_copy(ys_ref, p2_ref[t], y2_s.at[sl], t, sems.at[sl]).start()
            return carry

        lax.fori_loop(0, tm, issue, 0, unroll=8)

    @pl.when(i == 0)
    def _():
        gather(pos1_ref, pos2_ref, 0)

    @pl.when(i < pl.num_programs(0) - 1)
    def _():
        gather(npos1_ref, npos2_ref, 1 - slot)

    _drain_rows(2 * tm, ys_ref, y1_s.at[slot], sems.at[slot])
    comb = comb_ref[...]
    w1 = jnp.max(comb, axis=-1, keepdims=True)
    w2 = jnp.sum(comb, axis=-1, keepdims=True) - w1
    out = x_ref[...] + w1 * y1_s[slot] + w2 * y2_s[slot]
    if final:
        out = _rms(out, fg_ref[...])
    o_ref[...] = out


def _combine(x2d, comb, pos1, pos2, ys, final_g=None, tm=512):
    n, d = x2d.shape
    final = final_g is not None
    nsteps = n // tm
    smem = lambda: pl.BlockSpec((tm,), lambda i: (i,), memory_space=pltpu.SMEM)
    smem_next = lambda: pl.BlockSpec((tm,), lambda i: (jnp.minimum(i + 1, nsteps - 1),),
                                     memory_space=pltpu.SMEM)
    in_specs = [smem(), smem(), smem_next(), smem_next(), pl.BlockSpec((tm, d), lambda i: (i, 0)),
                pl.BlockSpec((tm, N_EXPERTS), lambda i: (i, 0))]
    args = [pos1, pos2, pos1, pos2, x2d, comb]
    if final:
        in_specs.append(pl.BlockSpec((1, d), lambda i: (0, 0)))
        args.append(final_g.reshape(1, d))
    in_specs.append(pl.BlockSpec(memory_space=pl.ANY))
    args.append(ys)
    return pl.pallas_call(
        functools.partial(_combine_kernel, final=final),
        grid=(nsteps,),
        in_specs=in_specs,
        out_specs=pl.BlockSpec((tm, d), lambda i: (i, 0)),
        out_shape=jax.ShapeDtypeStruct((n, d), F32),
        scratch_shapes=[pltpu.VMEM((2, tm, d), F32), pltpu.VMEM((2, tm, d), F32),
                        pltpu.SemaphoreType.DMA((2,))],
        compiler_params=pltpu.CompilerParams(dimension_semantics=("arbitrary",)),
        name="moe_combine",
    )(*args)


def _moe(x2d, g, rw, wg, wu, wd, final_g=None, tm=MOE_ROW_TILE):
    n, d = x2d.shape
    comb, idx, cnt = _router(x2d, g, rw)
    counts = cnt[0].astype(jnp.int32)
    padded = ((counts + tm - 1) // tm) * tm
    ends = jnp.cumsum(padded)
    starts = ends - padded
    pos1 = starts[idx[:, 0]] + idx[:, 2]
    pos2 = starts[idx[:, 1]] + idx[:, 3]
    n_rows = 2 * n + N_EXPERTS * tm
    n_tiles = (ends[-1] // tm).reshape(1)
    tile_ids = jnp.arange(n_rows // tm, dtype=jnp.int32)
    tile_expert = jnp.minimum(
        jnp.sum((tile_ids[:, None] >= (ends // tm)[None, :]).astype(jnp.int32), axis=1), N_EXPERTS - 1)
    xs = _dispatch(x2d, g, pos1, pos2, n_rows)
    ys = _grouped_ffn(xs, tile_expert, n_tiles, wg, wu, wd, tm)
    return _combine(x2d, comb, pos1, pos2, ys, final_g=final_g)


def _ffn_kernel(*refs, final):
    if final:
        x_ref, g_ref, wg_ref, wu_ref, wd_ref, fg_ref, o_ref = refs
    else:
        x_ref, g_ref, wg_ref, wu_ref, wd_ref, o_ref = refs
    x = x_ref[...]
    hn = _rms(x, g_ref[...]).astype(BF16)
    act = _silu(jnp.dot(hn, wg_ref[...], preferred_element_type=F32)) * jnp.dot(
        hn, wu_ref[...], preferred_element_type=F32)
    out = x + jnp.dot(act.astype(BF16), wd_ref[...], preferred_element_type=F32)
    if final:
        out = _rms(out, fg_ref[...])
    o_ref[...] = out


def _ffn(x2d, g, wg, wu, wd, final_g=None, tm=512):
    n, d = x2d.shape
    final = final_g is not None
    xmap = lambda i: (i, 0)
    cmap = lambda i: (0, 0)
    resident = lambda w: pl.BlockSpec(w.shape, cmap, pipeline_mode=pl.Buffered(1))
    in_specs = [pl.BlockSpec((tm, d), xmap), pl.BlockSpec((1, d), cmap), resident(wg), resident(wu), resident(wd)]
    args = [x2d, g.reshape(1, d), wg, wu, wd]
    if final:
        in_specs.append(pl.BlockSpec((1, d), cmap))
        args.append(final_g.reshape(1, d))
    return pl.pallas_call(
        functools.partial(_ffn_kernel, final=final),
        grid=(n // tm,),
        in_specs=in_specs,
        out_specs=pl.BlockSpec((tm, d), xmap),
        out_shape=jax.ShapeDtypeStruct((n, d), F32),
        compiler_params=pltpu.CompilerParams(
            dimension_semantics=("arbitrary",), vmem_limit_bytes=VMEM_LIMIT_BYTES),
        name="dense_ffn",
    )(*args)


def kernel(x, norm_mix_g, w_in, gdn_conv_w, gdn_a_log, gdn_dt_bias, gdn_norm_g, rwkv_mu, rwkv_w0, rwkv_w2, rwkv_a0, rwkv_a2, rwkv_g2, rwkv_k_k, rwkv_k_a, rwkv_r_k, rwkv_ln_g, rwkv_ln_b, w_branch_gdn, w_branch_rwkv, w_branch_sb, w_out, norm_ffn_g, ffn_w_gate, ffn_w_up, ffn_w_down, router_w, moe_w_gate, moe_w_up, moe_w_down, final_norm_g):
    b, t, d = x.shape
    n = b * t
    depth = w_in.shape[0]
    x2 = x.reshape(n, d).astype(F32)
    for layer in range(depth):
        w = w_in[layer]
        w_gdn = w[:, 0:2048].astype(BF16)
        w_ba = jnp.pad(w[:, 2048:2056], ((0, 0), (0, 120))).astype(BF16)
        w_rwkv = w[:, 2056:3848].astype(BF16)
        w_sb = w[:, 3848:5384].astype(BF16)
        w_gates = w[:, 5384:8456].astype(BF16)
        g_mix = norm_mix_g[layer].astype(F32)
        p_gdn, p_ba, p_rwkv, p_sb, p_gates = _normproj(
            x2, g_mix, [w_gdn, w_ba, w_rwkv, w_sb, w_gates], [F32, F32, F32, F32, BF16])
        o_a = _gdn(p_gdn.reshape(b, t, -1), p_ba.reshape(b, t, -1), gdn_conv_w[layer], gdn_a_log[layer], gdn_dt_bias[layer],
                   gdn_norm_g[layer])
        o_b = _rwkv(p_rwkv.reshape(b, t, -1), rwkv_mu[layer], rwkv_w0[layer], rwkv_w2[layer],
                    rwkv_a0[layer], rwkv_a2[layer], rwkv_g2[layer], rwkv_k_k[layer], rwkv_k_a[layer],
                    rwkv_r_k[layer].reshape(-1), rwkv_ln_g[layer], rwkv_ln_b[layer])
        o_c = _stick_breaking(p_sb.reshape(b, t, -1))
        x2 = _merge(x2, o_a.reshape(n, -1), o_b.reshape(n, -1), o_c.reshape(n, -1), p_gates,
                    w_branch_gdn[layer].astype(BF16), w_branch_rwkv[layer].astype(BF16),
                    w_branch_sb[layer].astype(BF16), w_out[layer].astype(BF16))
        g_ffn = norm_ffn_g[layer].astype(F32)
        final_g = final_norm_g.astype(F32) if layer == depth - 1 else None
        i = layer // 2
        if layer % 2 == 0:
            x2 = _ffn(x2, g_ffn, ffn_w_gate[i].astype(BF16), ffn_w_up[i].astype(BF16),
                      ffn_w_down[i].astype(BF16), final_g=final_g)
        else:
            x2 = _moe(x2, g_ffn, router_w[i], moe_w_gate[i].astype(BF16), moe_w_up[i].astype(BF16),
                      moe_w_down[i].astype(BF16), final_g=final_g)
    return x2.reshape(b, t, d)
```
